```python
import math
import jax, jax.numpy as jnp
from jax import lax
import numpy as np

D_MODEL = 1024
BATCH = 16
SEQ = 2048
DEPTH = 4

N_MIXERS = 2
PLE_DIM = 256
EPS = 1e-6
NEG_INF = -1e30
H_A = 8
D_NOPE = 128
D_V = 128
D_CQ = 256
D_C = 256
H_I = 8
D_I = 128
TOPK_MAX = 256
Q_BLOCK = 128
N_BUCKETS = 32
MAX_DISTANCE = 128
A_WIDTH = H_A * D_V
A_IN = D_CQ + D_C + D_I + H_I + A_WIDTH
H_B = 8
D_K = 128
D_VB = 128
CONV_W = 4
CHUNK = 64
B_QKV = H_B * (2 * D_K + D_VB)
B_WIDTH = H_B * D_VB
B_IN = B_QKV + 2 * H_B + B_WIDTH
N_LAYERS_A = len(range(0, DEPTH, N_MIXERS))
N_LAYERS_B = len(range(1, DEPTH, N_MIXERS))

kernel_name = "hybrid_dsa_gated_deltanet_trunk"


def rms_norm(x, gain=None):
    xf = x.astype(jnp.float32)
    y = xf * lax.rsqrt(jnp.mean(xf * xf, axis=-1, keepdims=True) + EPS)
    if gain is not None:
        y = y * gain.astype(jnp.float32)
    return y.astype(x.dtype)


def l2_norm(x):
    return x * lax.rsqrt(jnp.sum(x * x, axis=-1, keepdims=True) + EPS)


def t5_bucket(rel):
    max_exact = N_BUCKETS // 2
    rel = jnp.maximum(rel, 0)
    rel_f = jnp.maximum(rel, 1).astype(jnp.float32)
    log_ratio = jnp.log(rel_f / max_exact) / math.log(MAX_DISTANCE / max_exact)
    large = max_exact + (log_ratio * (N_BUCKETS - max_exact)).astype(jnp.int32)
    large = jnp.minimum(large, N_BUCKETS - 1)
    return jnp.where(rel < max_exact, rel, large)


def causal_conv(x, conv_w):
    c = x.shape[-1]
    w = conv_w.shape[0]
    return lax.conv_general_dilated(
        x, conv_w[:, None, :].astype(x.dtype), window_strides=(1,), padding=[(w - 1, 0)],
        dimension_numbers=("NWC", "WIO", "NWC"), feature_group_count=c)


def sparse_latent_attention(q_lat, q_idx, w_idx, c_kv, k_idx, rel_bias, topk):
    b, s = q_lat.shape[0], q_lat.shape[1]
    nb = s // Q_BLOCK
    key_pos = jnp.arange(s, dtype=jnp.int32)

    def to_blocks(a):
        return jnp.moveaxis(a.reshape(b, nb, Q_BLOCK, *a.shape[2:]), 1, 0)

    def one_block(args):
        qb, qib, wb, start = args
        qpos = start + jnp.arange(Q_BLOCK, dtype=jnp.int32)
        idx_logits = jnp.einsum("bqhd,bsd->bqhs", qib, k_idx)
        score = jnp.einsum("bqhs,bqh->bqs", jax.nn.relu(idx_logits), wb).astype(jnp.float32)
        causal = key_pos[None, :] <= qpos[:, None]
        score = jnp.where(causal[None], score, NEG_INF)
        _, sel = lax.top_k(score, topk)
        valid = sel <= qpos[None, :, None]
        c_sel = jax.vmap(lambda c, i: c[i])(c_kv, sel)
        logits = jnp.einsum("bqhc,bqkc->bqhk", qb, c_sel).astype(jnp.float32)
        bias = rel_bias[t5_bucket(qpos[None, :, None] - sel)]
        logits = logits + jnp.moveaxis(bias, -1, 2).astype(jnp.float32)
        logits = jnp.where(valid[:, :, None, :], logits, NEG_INF)
        prob = jax.nn.softmax(logits, axis=-1).astype(c_sel.dtype)
        return jnp.einsum("bqhk,bqkc->bqhc", prob, c_sel)

    starts = jnp.arange(nb, dtype=jnp.int32) * Q_BLOCK
    out = lax.map(one_block, (to_blocks(q_lat), to_blocks(q_idx), to_blocks(w_idx), starts))
    return jnp.moveaxis(out, 0, 1).reshape(b, s, H_A, D_C)


def dsa_mixer(h, w_in, g_cq, w_uq, w_uk, g_q, g_kv, w_iq, w_uv, w_out, rel_bias):
    b, s, _ = h.shape
    proj = h @ w_in
    c_q, c_kv, k_idx, w_idx, z = jnp.split(
        proj, [D_CQ, D_CQ + D_C, D_CQ + D_C + D_I, D_CQ + D_C + D_I + H_I], axis=-1)
    c_q = rms_norm(c_q, g_cq)
    q_nope = jnp.einsum("bsc,chd->bshd", c_q, w_uq)
    q_lat = jnp.einsum("bshd,hdl->bshl", q_nope, w_uk)
    q_lat = rms_norm(q_lat, g_q) * (D_C ** -0.5)
    c_kv = rms_norm(c_kv, g_kv)
    q_idx = jnp.einsum("bsc,chd->bshd", c_q, w_iq) * (D_I ** -0.5)
    k_idx = rms_norm(k_idx)
    w_idx = w_idx * (H_I ** -0.5)
    topk = min(TOPK_MAX, s // 4)
    o_lat = sparse_latent_attention(q_lat, q_idx, w_idx, c_kv, k_idx, rel_bias, topk)
    o = jnp.einsum("bshl,hlv->bshv", o_lat, w_uv).reshape(b, s, A_WIDTH)
    return (o * jax.nn.silu(z)) @ w_out


def chunked_gated_delta_rule(q, k, v, g, beta):
    b, s, h, dk = q.shape
    dv = v.shape[-1]
    n = s // CHUNK

    def chunk4(a):
        return a.reshape(b, n, CHUNK, h, a.shape[-1]).transpose(0, 3, 1, 2, 4)

    qc, kc, vc = chunk4(q), chunk4(k), chunk4(v)
    gc = jnp.cumsum(g.reshape(b, n, CHUNK, h).transpose(0, 3, 1, 2), axis=-1)
    bc = beta.reshape(b, n, CHUNK, h).transpose(0, 3, 1, 2)
    t_idx = jnp.arange(CHUNK)
    incl = t_idx[:, None] >= t_idx[None, :]
    strict = t_idx[:, None] > t_idx[None, :]
    diff = gc[..., :, None] - gc[..., None, :]
    decay = jnp.where(incl, jnp.exp(jnp.where(incl, diff, 0.0)), 0.0)
    kb = kc * bc[..., None]
    lower = jnp.where(strict, jnp.einsum("bhntd,bhnjd->bhntj", kb, kc) * decay, 0.0)
    a_mat = lower + jnp.eye(CHUNK, dtype=jnp.float32)
    rhs = jnp.concatenate([vc * bc[..., None], kb * jnp.exp(gc)[..., None]], axis=-1)
    sol = lax.linalg.triangular_solve(a_mat, rhs, left_side=True, lower=True)
    u_pre, w_dec = sol[..., :dv], sol[..., dv:]
    aqk = jnp.where(incl, jnp.einsum("bhntd,bhnjd->bhntj", qc, kc) * decay, 0.0)
    q_dec = qc * jnp.exp(gc)[..., None]
    k_dec = kc * jnp.exp(gc[..., -1:] - gc)[..., None]
    c_dec = jnp.exp(gc[..., -1])

    def step(state, xs):
        u_p, w_d, a_qk, q_d, k_d, cd = xs
        u = u_p - jnp.einsum("bhck,bhkv->bhcv", w_d, state)
        o = jnp.einsum("bhck,bhkv->bhcv", q_d, state) + jnp.einsum("bhct,bhtv->bhcv", a_qk, u)
        state = state * cd[..., None, None] + jnp.einsum("bhck,bhcv->bhkv", k_d, u)
        return state, o

    xs = tuple(jnp.moveaxis(a, 2, 0) for a in (u_pre, w_dec, aqk, q_dec, k_dec, c_dec))
    state0 = jnp.zeros((b, h, dk, dv), jnp.float32)
    _, o = lax.scan(step, state0, xs)
    return o.transpose(1, 0, 3, 2, 4).reshape(b, s, h, dv)


def gdn_mixer(h, w_in, conv_w, a_log, dt_bias, g_o, w_out):
    b, s, _ = h.shape
    proj = h @ w_in
    qkv, beta_raw, a_raw, z = jnp.split(proj, [B_QKV, B_QKV + H_B, B_QKV + 2 * H_B], axis=-1)
    qkv = jax.nn.silu(causal_conv(qkv, conv_w))
    q, k, v = jnp.split(qkv, [H_B * D_K, 2 * H_B * D_K], axis=-1)
    q = l2_norm(q.reshape(b, s, H_B, D_K).astype(jnp.float32)) * (D_K ** -0.5)
    k = l2_norm(k.reshape(b, s, H_B, D_K).astype(jnp.float32))
    v = v.reshape(b, s, H_B, D_VB).astype(jnp.float32)
    beta = jax.nn.sigmoid(beta_raw.astype(jnp.float32))
    g = -jnp.exp(a_log.astype(jnp.float32)) * jax.nn.softplus(
        a_raw.astype(jnp.float32) + dt_bias.astype(jnp.float32))
    o = chunked_gated_delta_rule(q, k, v, g, beta).astype(h.dtype)
    o = rms_norm(o, g_o).reshape(b, s, B_WIDTH)
    return (o * jax.nn.silu(z)) @ w_out


def setup_inputs(seed: int = 0) -> dict:
    key = jax.random.key(seed)
    ks = jax.random.split(key, 24)
    f32 = jnp.float32

    def dense(k, shape, fan_in):
        return jax.random.normal(k, shape, f32) * (fan_in ** -0.5)

    def gain(k, shape):
        return 1.0 + 0.05 * jax.random.normal(k, shape, f32)

    dt = jnp.exp(jax.random.uniform(ks[17], (N_LAYERS_B, H_B), f32, math.log(1e-3), math.log(1e-1)))
    return {
        "x": jax.random.normal(ks[0], (BATCH, SEQ, D_MODEL), f32),
        "p": jax.random.normal(ks[1], (DEPTH, BATCH, SEQ, PLE_DIM), f32),
        "norm_w": gain(ks[2], (DEPTH, D_MODEL)),
        "a_w_in": dense(ks[3], (N_LAYERS_A, D_MODEL, A_IN), D_MODEL),
        "a_g_cq": gain(ks[4], (N_LAYERS_A, D_CQ)),
        "a_w_uq": dense(ks[5], (N_LAYERS_A, D_CQ, H_A, D_NOPE), D_CQ),
        "a_w_uk": dense(ks[6], (N_LAYERS_A, H_A, D_NOPE, D_C), D_NOPE),
        "a_g_q": gain(ks[7], (N_LAYERS_A, D_C)),
        "a_g_kv": gain(ks[8], (N_LAYERS_A, D_C)),
        "a_w_iq": dense(ks[9], (N_LAYERS_A, D_CQ, H_I, D_I), D_CQ),
        "a_w_uv": dense(ks[10], (N_LAYERS_A, H_A, D_C, D_V), D_C),
        "a_w_out": dense(ks[11], (N_LAYERS_A, A_WIDTH, D_MODEL), A_WIDTH),
        "rel_bias": 0.5 * jax.random.normal(ks[12], (N_BUCKETS, H_A), f32),
        "b_w_in": dense(ks[13], (N_LAYERS_B, D_MODEL, B_IN), D_MODEL),
        "b_conv_w": dense(ks[14], (N_LAYERS_B, CONV_W, B_QKV), CONV_W),
        "b_a_log": jnp.log(jax.random.uniform(ks[15], (N_LAYERS_B, H_B), f32, 1.0, 16.0)),
        "b_dt_bias": dt + jnp.log(-jnp.expm1(-dt)),
        "b_g_o": gain(ks[16], (N_LAYERS_B, D_VB)),
        "b_w_out": dense(ks[18], (N_LAYERS_B, B_WIDTH, D_MODEL), B_WIDTH),
        "ple_norm": gain(ks[19], (DEPTH, D_MODEL)),
        "ple_w_gate": dense(ks[20], (DEPTH, D_MODEL, D_MODEL), D_MODEL),
        "ple_w_proj": dense(ks[21], (DEPTH, PLE_DIM, D_MODEL), PLE_DIM),
    }


def reference(x, p, norm_w, a_w_in, a_g_cq, a_w_uq, a_w_uk, a_g_q, a_g_kv, a_w_iq, a_w_uv,
              a_w_out, rel_bias, b_w_in, b_conv_w, b_a_log, b_dt_bias, b_g_o, b_w_out,
              ple_norm, ple_w_gate, ple_w_proj):
    for i in range(DEPTH):
        h = rms_norm(x, norm_w[i])
        j = i // N_MIXERS
        if i % N_MIXERS == 0:
            y = dsa_mixer(h, a_w_in[j], a_g_cq[j], a_w_uq[j], a_w_uk[j], a_g_q[j], a_g_kv[j],
                          a_w_iq[j], a_w_uv[j], a_w_out[j], rel_bias)
        else:
            y = gdn_mixer(h, b_w_in[j], b_conv_w[j], b_a_log[j], b_dt_bias[j], b_g_o[j], b_w_out[j])
        x = x + y
        gate = jax.nn.sigmoid(rms_norm(x, ple_norm[i]) @ ple_w_gate[i])
        x = x + gate * (p[i] @ ple_w_proj[i])
    return x
```

```python
import functools
import math

import jax
import jax.numpy as jnp
from jax import lax
from jax.experimental import pallas as pl
from jax.experimental.pallas import tpu as pltpu

F32 = jnp.float32
BF16 = jnp.bfloat16
I32 = jnp.int32

EPS = 1e-6
NEG_INF = -1e30
INT_MIN = -(2 ** 31)

LANES = 128
VMEM_LIMIT_BYTES = 56 * 1024 * 1024

H_A = 8
D_C = 256
D_CQ = 256
D_I = 128
H_I = 8
D_NOPE = 128
D_V = 128
TOPK_MAX = 256
N_BUCKETS = 32
MAX_DISTANCE = 128
Q_TILE = 128
K_TILE = 256
H_B = 8
D_K = 128
D_VB = 128
CONV_W = 4
CHUNK = 64
GDN_TILE = 256
GDN_HEADS = 2


def _dot(a, b):
    return jnp.dot(a, b, preferred_element_type=F32)


def _dot_nt(a, b):
    return lax.dot_general(a, b, (((1,), (1,)), ((), ())), preferred_element_type=F32)


def _dot_tn(a, b):
    return lax.dot_general(a, b, (((0,), (0,)), ((), ())), preferred_element_type=F32)


def _rms(x, gain=None):
    y = x * lax.rsqrt(jnp.mean(x * x, axis=-1, keepdims=True) + EPS)
    return y if gain is None else y * gain


def _sigmoid(x):
    return 1.0 / (1.0 + jnp.exp(-x))


def _softplus(x):
    return jnp.maximum(x, 0.0) + jnp.log1p(jnp.exp(-jnp.abs(x)))


def _params(*semantics):
    return pltpu.CompilerParams(dimension_semantics=semantics, vmem_limit_bytes=VMEM_LIMIT_BYTES)


def _const_spec(shape):
    nd = len(shape)
    return pl.BlockSpec(shape, lambda *_: (0,) * nd)


def _a_proj_kernel(x_ref, nw_ref, wlat_ref, wwt_ref, wz_ref, gcq_ref, gkv_ref, wuq_ref, wuk_ref, gq_ref, wiq_ref,
                   qlat_ref, qidx_ref, wt_ref, ckv_ref, kidx_ref, z_ref):
    h = _rms(x_ref[...], nw_ref[...]).astype(BF16)
    lat = _dot(h, wlat_ref[...])
    cq = _rms(lat[:, :D_CQ], gcq_ref[...]).astype(BF16)
    ckv_ref[...] = _rms(lat[:, D_CQ:D_CQ + D_C], gkv_ref[...]).astype(BF16)
    kidx_ref[...] = _rms(lat[:, D_CQ + D_C:]).astype(BF16)
    wt_ref[...] = _dot_nt(wwt_ref[...], h) * (H_I ** -0.5)
    z_ref[...] = _dot(h, wz_ref[...])
    qn = _dot(cq, wuq_ref[...]).astype(BF16)
    for hh in range(H_A):
        ql = _dot(qn[:, hh * D_NOPE:(hh + 1) * D_NOPE], wuk_ref[hh])
        qlat_ref[hh] = (_rms(ql, gq_ref[...]) * (D_C ** -0.5)).astype(BF16)
    qi = _dot(cq, wiq_ref[...]) * (D_I ** -0.5)
    for hh in range(H_I):
        qidx_ref[hh] = qi[:, hh * D_I:(hh + 1) * D_I].astype(BF16)


def _a_proj(x2, nw, wlat, wwt, wz, gcq, gkv, wuq, wuk, gq, wiq, tile):
    n, d = x2.shape
    grid = (n // tile,)
    tok = lambda w: pl.BlockSpec((tile, w), lambda i: (i, 0))
    return pl.pallas_call(
        _a_proj_kernel,
        grid=grid,
        in_specs=[tok(d), _const_spec(nw.shape), _const_spec(wlat.shape), _const_spec(wwt.shape),
                  _const_spec(wz.shape), _const_spec(gcq.shape), _const_spec(gkv.shape), _const_spec(wuq.shape),
                  _const_spec(wuk.shape), _const_spec(gq.shape), _const_spec(wiq.shape)],
        out_specs=[pl.BlockSpec((H_A, tile, D_C), lambda i: (0, i, 0)),
                   pl.BlockSpec((H_I, tile, D_I), lambda i: (0, i, 0)),
                   pl.BlockSpec((H_I, tile), lambda i: (0, i)),
                   tok(D_C), tok(D_I), tok(H_A * D_V)],
        out_shape=[jax.ShapeDtypeStruct((H_A, n, D_C), BF16),
                   jax.ShapeDtypeStruct((H_I, n, D_I), BF16),
                   jax.ShapeDtypeStruct((H_I, n), F32),
                   jax.ShapeDtypeStruct((n, D_C), BF16),
                   jax.ShapeDtypeStruct((n, D_I), BF16),
                   jax.ShapeDtypeStruct((n, H_A * D_V), F32)],
        compiler_params=_params("parallel"),
        name="a_proj",
    )(x2, nw, wlat, wwt, wz, gcq, gkv, wuq, wuk, gq, wiq)


def _a_attn_kernel(qidx_ref, wt_ref, kidx_ref, qlat_ref, ckv_ref, z_ref, bt_ref, wuv_ref,
                   y_ref, keys_ref, maskb_ref, lg_ref, mx_ref, ss_ref, oacc_ref, *, topk):
    qb = pl.program_id(1)
    n_kt = lax.shift_right_logical(qb, 1) + 1
    rows = H_A * Q_TILE
    qidx = qidx_ref[...].reshape(H_I * Q_TILE, D_I)
    qlat = qlat_ref[...].reshape(rows, D_C)
    q_pos = qb * Q_TILE + lax.broadcasted_iota(I32, (K_TILE, Q_TILE), 1)

    def idx_body(c, carry):
        k0 = pl.multiple_of(c * K_TILE, K_TILE)
        lg = _dot_nt(kidx_ref[pl.ds(k0, K_TILE), :], qidx)
        sc = jnp.zeros((K_TILE, Q_TILE), F32)
        for hh in range(H_I):
            sc = sc + jnp.maximum(lg[:, hh * Q_TILE:(hh + 1) * Q_TILE], 0.0) * wt_ref[hh:hh + 1, :]
        bits = lax.bitcast_convert_type(sc, I32)
        sign = lax.shift_right_arithmetic(bits, 31)
        skey = (bits ^ (sign & 0x7FFFFFFF)) - sign
        key_pos = k0 + lax.broadcasted_iota(I32, (K_TILE, Q_TILE), 0)
        skey = jnp.where(key_pos <= q_pos, skey, INT_MIN)
        keys_ref[c] = skey.reshape(K_TILE // 8, 8, Q_TILE)
        return carry

    lax.fori_loop(0, n_kt, idx_body, 0)

    def bit_body(i, thr):
        cand = thr + lax.shift_left(jnp.int32(1), 31 - i)

        def cnt_body(c, acc):
            return acc + jnp.sum((keys_ref[c] >= cand[None]).astype(I32), axis=0)

        acc = lax.fori_loop(0, n_kt, cnt_body, jnp.zeros((8, Q_TILE), I32))
        tot = jnp.sum(acc, axis=0, keepdims=True)
        return jnp.where(tot >= topk, cand, thr)

    thr = lax.fori_loop(0, 32, bit_body, jnp.full((8, Q_TILE), INT_MIN, I32))
    thr = jnp.maximum(thr, INT_MIN + 1)

    def mask_body(c, carry):
        sel = keys_ref[c] >= thr[None]
        neg = jnp.where(sel, 0.0, NEG_INF).reshape(K_TILE, Q_TILE)
        maskb_ref[c] = neg.T
        return carry

    lax.fori_loop(0, n_kt, mask_body, 0)

    mx_ref[...] = jnp.full(mx_ref.shape, NEG_INF, F32)

    def qk_body(c, carry):
        k0 = pl.multiple_of(c * K_TILE, K_TILE)
        lg = _dot_nt(qlat, ckv_ref[pl.ds(k0, K_TILE), :])
        t_idx = jnp.minimum(qb - 2 * c, 3)
        mb = maskb_ref[c]
        for hh in range(H_A):
            l = lg[hh * Q_TILE:(hh + 1) * Q_TILE, :] + (mb + bt_ref[t_idx, hh])
            lg_ref[c, hh] = l
            mx_ref[hh] = jnp.maximum(mx_ref[hh], l)
        return carry

    lax.fori_loop(0, n_kt, qk_body, 0)

    m = jnp.max(mx_ref[...], axis=-1, keepdims=True)
    ss_ref[...] = jnp.zeros(ss_ref.shape, F32)
    oacc_ref[...] = jnp.zeros(oacc_ref.shape, F32)

    def pv_body(c, carry):
        k0 = pl.multiple_of(c * K_TILE, K_TILE)
        p = jnp.exp(lg_ref[c] - m)
        ss_ref[...] += p
        oacc_ref[...] += _dot(p.reshape(rows, K_TILE).astype(BF16), ckv_ref[pl.ds(k0, K_TILE), :])
        return carry

    lax.fori_loop(0, n_kt, pv_body, 0)
    denom = jnp.sum(ss_ref[...], axis=-1, keepdims=True)
    o_lat = oacc_ref[...].reshape(H_A, Q_TILE, D_C) / denom

    for hh in range(H_A):
        oh = _dot(o_lat[hh].astype(BF16), wuv_ref[hh])
        zz = z_ref[:, hh * D_V:(hh + 1) * D_V]
        y_ref[:, hh * D_V:(hh + 1) * D_V] = (oh * (zz * _sigmoid(zz))).astype(BF16)


def _a_attn(qidx, wt, kidx, qlat, ckv, z, bt, wuv, batch, seq):
    n = batch * seq
    nq = seq // Q_TILE
    n_kt = seq // K_TILE
    topk = min(TOPK_MAX, seq // 4)
    grid = (batch, nq)
    kern = functools.partial(_a_attn_kernel, topk=topk)
    return pl.pallas_call(
        kern,
        grid=grid,
        in_specs=[pl.BlockSpec((H_I, Q_TILE, D_I), lambda b, q: (0, b * nq + q, 0)),
                  pl.BlockSpec((H_I, Q_TILE), lambda b, q: (0, b * nq + q)),
                  pl.BlockSpec((seq, D_I), lambda b, q: (b, 0)),
                  pl.BlockSpec((H_A, Q_TILE, D_C), lambda b, q: (0, b * nq + q, 0)),
                  pl.BlockSpec((seq, D_C), lambda b, q: (b, 0)),
                  pl.BlockSpec((Q_TILE, H_A * D_V), lambda b, q: (b * nq + q, 0)),
                  _const_spec(bt.shape), _const_spec(wuv.shape)],
        out_specs=pl.BlockSpec((Q_TILE, H_A * D_V), lambda b, q: (b * nq + q, 0)),
        out_shape=jax.ShapeDtypeStruct((n, H_A * D_V), BF16),
        scratch_shapes=[pltpu.VMEM((n_kt, K_TILE // 8, 8, Q_TILE), I32),
                        pltpu.VMEM((n_kt, Q_TILE, K_TILE), F32),
                        pltpu.VMEM((n_kt, H_A, Q_TILE, K_TILE), F32),
                        pltpu.VMEM((H_A, Q_TILE, K_TILE), F32),
                        pltpu.VMEM((H_A, Q_TILE, K_TILE), F32),
                        pltpu.VMEM((H_A * Q_TILE, D_C), F32)],
        compiler_params=_params("parallel", "arbitrary"),
        name="a_attn",
    )(qidx, wt, kidx, qlat, ckv, z, bt, wuv)


def _b_proj_kernel(x_ref, nw_ref, wqkv_ref, wba_ref, wbat_ref, wz_ref, cw_ref, alog_ref, dtb_ref, alogc_ref, dtbc_ref,
                   q_ref, k_ref, v_ref, beta_ref, g_ref, gt_ref, z_ref, buf_ref, *, tile):
    s = pl.program_id(1)
    h = _rms(x_ref[...], nw_ref[...]).astype(BF16)
    z_ref[...] = _dot(h, wz_ref[...])
    ba = _dot(h, wba_ref[...])
    beta_ref[...] = _sigmoid(ba[:, :H_B])
    g_ref[...] = -jnp.exp(alog_ref[...]) * _softplus(ba[:, H_B:2 * H_B] + dtb_ref[...])
    bat = _dot_nt(wbat_ref[...], h)
    gt_ref[...] = -jnp.exp(alogc_ref[...]) * _softplus(bat[H_B:, :] + dtbc_ref[...])

    @pl.when(s == 0)
    def _():
        buf_ref[0:8, :] = jnp.zeros((8, buf_ref.shape[1]), F32)

    width = H_B * D_K
    for sec, out_ref in enumerate((q_ref, k_ref, v_ref)):
        cols = slice(sec * width, (sec + 1) * width)
        pre = _dot(h, wqkv_ref[:, cols])
        buf_ref[8:8 + tile, cols] = pre
        acc = pre * cw_ref[CONV_W - 1:CONV_W, cols]
        for w in range(CONV_W - 1):
            acc = acc + buf_ref[8 - (CONV_W - 1) + w:8 - (CONV_W - 1) + w + tile, cols] * cw_ref[w:w + 1, cols]
        buf_ref[0:8, cols] = buf_ref[tile:tile + 8, cols]
        y = acc * _sigmoid(acc)
        if sec == 2:
            out_ref[...] = y
        else:
            scale = (D_K ** -0.5) if sec == 0 else 1.0
            for hh in range(H_B):
                yy = y[:, hh * D_K:(hh + 1) * D_K]
                nrm = lax.rsqrt(jnp.sum(yy * yy, axis=-1, keepdims=True) + EPS)
                out_ref[:, hh * D_K:(hh + 1) * D_K] = yy * (nrm * scale)


def _b_proj(x2, nw, wqkv, wba, wbat, wz, cw, alog, dtb, batch, seq, tile):
    n, d = x2.shape
    ns = seq // tile
    grid = (batch, ns)
    tok = lambda w: pl.BlockSpec((tile, w), lambda b, s: (b * ns + s, 0))
    width = H_B * D_K
    kern = functools.partial(_b_proj_kernel, tile=tile)
    return pl.pallas_call(
        kern,
        grid=grid,
        in_specs=[tok(d), _const_spec(nw.shape), _const_spec(wqkv.shape), _const_spec(wba.shape),
                  _const_spec(wbat.shape), _const_spec(wz.shape), _const_spec(cw.shape),
                  _const_spec((1, H_B)), _const_spec((1, H_B)), _const_spec((H_B, 1)), _const_spec((H_B, 1))],
        out_specs=[tok(width), tok(width), tok(width), tok(H_B), tok(H_B),
                   pl.BlockSpec((H_B, tile), lambda b, s: (0, b * ns + s)), tok(width)],
        out_shape=[jax.ShapeDtypeStruct((n, width), F32)] * 3
        + [jax.ShapeDtypeStruct((n, H_B), F32)] * 2
        + [jax.ShapeDtypeStruct((H_B, n), F32), jax.ShapeDtypeStruct((n, width), F32)],
        scratch_shapes=[pltpu.VMEM((tile + 8, 3 * width), F32)],
        compiler_params=_params("parallel", "arbitrary"),
        name="b_proj",
    )(x2, nw, wqkv, wba, wbat, wz, cw, alog.reshape(1, H_B), dtb.reshape(1, H_B),
      alog.reshape(H_B, 1), dtb.reshape(H_B, 1))


def _b_gdn_kernel(q_ref, k_ref, v_ref, z_ref, beta_ref, g_ref, gt_ref, go_ref, y_ref, state_ref):
    hb = pl.program_id(1)
    s = pl.program_id(2)
    t = GDN_TILE
    n_chunks = t // CHUNK

    @pl.when(s == 0)
    def _():
        state_ref[...] = jnp.zeros(state_ref.shape, F32)

    r = lax.broadcasted_iota(I32, (t, t), 0)
    c = lax.broadcasted_iota(I32, (t, t), 1)
    same = lax.shift_right_logical(r, int(math.log2(CHUNK))) == lax.shift_right_logical(c, int(math.log2(CHUNK)))
    incl = same & (r >= c)
    strict = same & (r > c)
    upper = same & (r <= c)
    eye = (r == c).astype(F32)
    lane8 = lax.broadcasted_iota(I32, (t, H_B), 1)
    sub8 = lax.broadcasted_iota(I32, (H_B, t), 0)

    for j in range(GDN_HEADS):
        head = hb * GDN_HEADS + j
        hs = slice(j * D_K, (j + 1) * D_K)
        g_col = jnp.sum(jnp.where(lane8 == head, g_ref[...], 0.0), axis=1, keepdims=True)
        b_col = jnp.sum(jnp.where(lane8 == head, beta_ref[...], 0.0), axis=1, keepdims=True)
        g_row = jnp.sum(jnp.where(sub8 == head, gt_ref[...], 0.0), axis=0, keepdims=True)
        g_rows = jnp.broadcast_to(g_row, (t, t))
        gc_col = jnp.sum(jnp.where(incl, g_rows, 0.0), axis=1, keepdims=True)
        gl_col = jnp.sum(jnp.where(same, g_rows, 0.0), axis=1, keepdims=True)
        gc_row = jnp.sum(jnp.where(upper, jnp.broadcast_to(g_col, (t, t)), 0.0), axis=0, keepdims=True)
        decay = jnp.where(incl, jnp.exp(jnp.minimum(gc_col - gc_row, 0.0)), 0.0)
        eg_col = jnp.exp(gc_col)
        ekl_col = jnp.exp(gl_col - gc_col)
        egl_col = jnp.exp(gl_col)

        qf = q_ref[:, hs]
        kf = k_ref[:, hs]
        vf = v_ref[:, hs]
        kb = kf * b_col
        k16 = kf.astype(BF16)
        kb16 = kb.astype(BF16)
        lmat = jnp.where(strict, _dot_nt(kb16, k16) * decay, 0.0)
        aqk = jnp.where(incl, _dot_nt(qf.astype(BF16), k16) * decay, 0.0).astype(BF16)

        l16 = lmat.astype(BF16)
        tinv = eye - lmat
        xpow = _dot(l16, l16)
        for step in range(1, int(math.log2(CHUNK))):
            x16 = xpow.astype(BF16)
            tinv = tinv + _dot(tinv.astype(BF16), x16)
            if step + 1 < int(math.log2(CHUNK)):
                xpow = _dot(x16, x16)

        rhs = jnp.concatenate([(vf * b_col).astype(BF16), (kb * eg_col).astype(BF16)], axis=1)
        sol = _dot(tinv.astype(BF16), rhs)
        u_pre = sol[:, :D_VB]
        w_dec = sol[:, D_VB:].astype(BF16)
        q_dec = (qf * eg_col).astype(BF16)
        k_dec = (kf * ekl_col).astype(BF16)

        state = state_ref[j]
        outs = []
        for ci in range(n_chunks):
            rs = slice(ci * CHUNK, (ci + 1) * CHUNK)
            s16 = state.astype(BF16)
            u = u_pre[rs] - _dot(w_dec[rs], s16)
            u16 = u.astype(BF16)
            outs.append(_dot(q_dec[rs], s16) + _dot(aqk[rs, rs], u16))
            state = state * egl_col[ci * CHUNK:ci * CHUNK + 1, :] + _dot_tn(k_dec[rs], u16)
        state_ref[j] = state
        o = jnp.concatenate(outs, axis=0)
        zz = z_ref[:, hs]
        y_ref[:, hs] = (_rms(o, go_ref[...]) * (zz * _sigmoid(zz))).astype(BF16)


def _b_gdn(q, k, v, z, beta, g, gt, go, batch, seq):
    n = batch * seq
    ns = seq // GDN_TILE
    nh = H_B // GDN_HEADS
    grid = (batch, nh, ns)
    wide = pl.BlockSpec((GDN_TILE, GDN_HEADS * D_K), lambda b, h, s: (b * ns + s, h))
    narrow = pl.BlockSpec((GDN_TILE, H_B), lambda b, h, s: (b * ns + s, 0))
    return pl.pallas_call(
        _b_gdn_kernel,
        grid=grid,
        in_specs=[wide, wide, wide, wide, narrow, narrow,
                  pl.BlockSpec((H_B, GDN_TILE), lambda b, h, s: (0, b * ns + s)),
                  _const_spec(go.shape)],
        out_specs=wide,
        out_shape=jax.ShapeDtypeStruct((n, H_B * D_VB), BF16),
        scratch_shapes=[pltpu.VMEM((GDN_HEADS, D_K, D_VB), F32)],
        compiler_params=_params("parallel", "parallel", "arbitrary"),
        name="b_gdn",
    )(q, k, v, z, beta, g, gt, go)


def _out_ple_kernel(x_ref, y_ref, p_ref, wout_ref, pn_ref, wgate_ref, wproj_ref, o_ref):
    x1 = x_ref[...] + _dot(y_ref[...], wout_ref[...])
    hn = _rms(x1, pn_ref[...]).astype(BF16)
    gate = _sigmoid(_dot(hn, wgate_ref[...]))
    o_ref[...] = x1 + gate * _dot(p_ref[...].astype(BF16), wproj_ref[...])


def _out_ple(x2, y, p2, wout, pn, wgate, wproj, tile):
    n, d = x2.shape
    tok = lambda w: pl.BlockSpec((tile, w), lambda i: (i, 0))
    return pl.pallas_call(
        _out_ple_kernel,
        grid=(n // tile,),
        in_specs=[tok(d), tok(y.shape[1]), tok(p2.shape[1]), _const_spec(wout.shape), _const_spec(pn.shape),
                  _const_spec(wgate.shape), _const_spec(wproj.shape)],
        out_specs=tok(d),
        out_shape=jax.ShapeDtypeStruct((n, d), F32),
        compiler_params=_params("parallel"),
        name="out_ple",
    )(x2, y, p2, wout, pn, wgate, wproj)


def _t5_bucket(rel):
    max_exact = N_BUCKETS // 2
    rel = jnp.maximum(rel, 0)
    rel_f = jnp.maximum(rel, 1).astype(F32)
    log_ratio = jnp.log(rel_f / max_exact) / math.log(MAX_DISTANCE / max_exact)
    large = max_exact + (log_ratio * (N_BUCKETS - max_exact)).astype(I32)
    large = jnp.minimum(large, N_BUCKETS - 1)
    return jnp.where(rel < max_exact, rel, large)


def _bias_tiles(rel_bias):
    assert 3 * Q_TILE - (K_TILE - 1) >= MAX_DISTANCE
    i = jnp.arange(Q_TILE, dtype=I32)[None, :, None]
    j = jnp.arange(K_TILE, dtype=I32)[None, None, :]
    cls = jnp.arange(4, dtype=I32)[:, None, None]
    bucket = _t5_bucket(cls * Q_TILE + i - j)
    return jnp.moveaxis(rel_bias.astype(F32)[bucket], -1, 1)


def kernel(x, p, norm_w, a_w_in, a_g_cq, a_w_uq, a_w_uk, a_g_q, a_g_kv, a_w_iq, a_w_uv, a_w_out, rel_bias, b_w_in, b_conv_w, b_a_log, b_dt_bias, b_g_o, b_w_out, ple_norm, ple_w_gate, ple_w_proj):
    batch, seq, d_model = x.shape
    depth = p.shape[0]
    n = batch * seq
    assert seq % K_TILE == 0 and seq % GDN_TILE == 0 and H_B % GDN_HEADS == 0
    x2 = x.reshape(n, d_model)
    p2 = p.reshape(depth, n, p.shape[-1])
    bt = _bias_tiles(rel_bias)
    row = lambda a: a.reshape(1, -1).astype(F32)
    n_lat = D_CQ + D_C + D_I
    b_qkv = H_B * (2 * D_K + D_VB)
    for i in range(depth):
        j = i // 2
        if i % 2 == 0:
            w_in = a_w_in[j]
            wlat = w_in[:, :n_lat].astype(BF16)
            wwt = w_in[:, n_lat:n_lat + H_I].T.astype(BF16)
            wz = w_in[:, n_lat + H_I:].astype(BF16)
            qlat, qidx, wt, ckv, kidx, z = _a_proj(
                x2, row(norm_w[i]), wlat, wwt, wz, row(a_g_cq[j]), row(a_g_kv[j]),
                a_w_uq[j].reshape(D_CQ, H_A * D_NOPE).astype(BF16), a_w_uk[j].astype(BF16), row(a_g_q[j]),
                a_w_iq[j].reshape(D_CQ, H_I * D_I).astype(BF16), tile=512)
            y = _a_attn(qidx, wt, kidx, qlat, ckv, z, bt, a_w_uv[j].astype(BF16), batch, seq)
            w_out = a_w_out[j]
        else:
            w_in = b_w_in[j]
            wqkv = w_in[:, :b_qkv].astype(BF16)
            w_ba = w_in[:, b_qkv:b_qkv + 2 * H_B]
            wba = jnp.pad(w_ba, ((0, 0), (0, LANES - 2 * H_B))).astype(BF16)
            wz = w_in[:, b_qkv + 2 * H_B:].astype(BF16)
            q, k, v, beta, g, gt, z = _b_proj(
                x2, row(norm_w[i]), wqkv, wba, w_ba.T.astype(BF16), wz, b_conv_w[j].astype(F32),
                b_a_log[j].astype(F32), b_dt_bias[j].astype(F32), batch, seq, tile=256)
            y = _b_gdn(q, k, v, z, beta, g, gt, row(b_g_o[j]), batch, seq)
            w_out = b_w_out[j]
        x2 = _out_ple(x2, y, p2[i], w_out.astype(BF16), row(ple_norm[i]), ple_w_gate[i].astype(BF16),
                      ple_w_proj[i].astype(BF16), tile=512)
    return x2.reshape(batch, seq, d_model)
```

```python
import functools
import math

import jax
import jax.numpy as jnp
from jax import lax
from jax.experimental import pallas as pl
from jax.experimental.pallas import tpu as pltpu

F32 = jnp.float32
BF16 = jnp.bfloat16
I32 = jnp.int32

EPS = 1e-6
NEG_INF = -1e30
INT_MIN = -(2 ** 31)

LANES = 128
VMEM_LIMIT_BYTES = 56 * 1024 * 1024

H_A = 8
D_C = 256
D_CQ = 256
D_I = 128
H_I = 8
D_NOPE = 128
D_V = 128
TOPK_MAX = 256
N_BUCKETS = 32
MAX_DISTANCE = 128
Q_TILE = 128
K_TILE = 256
H_B = 8
D_K = 128
D_VB = 128
CONV_W = 4
CHUNK = 64
GDN_TILE = 256
GDN_HEADS = 4


def _dot(a, b):
    return jnp.dot(a, b, preferred_element_type=F32)


def _dot_nt(a, b):
    return lax.dot_general(a, b, (((1,), (1,)), ((), ())), preferred_element_type=F32)


def _dot_tn(a, b):
    return lax.dot_general(a, b, (((0,), (0,)), ((), ())), preferred_element_type=F32)


def _rms(x, gain=None):
    y = x * lax.rsqrt(jnp.mean(x * x, axis=-1, keepdims=True) + EPS)
    return y if gain is None else y * gain


def _sigmoid(x):
    return 1.0 / (1.0 + jnp.exp(-x))


def _softplus(x):
    return jnp.maximum(x, 0.0) + jnp.log1p(jnp.exp(-jnp.abs(x)))


def _params(*semantics):
    return pltpu.CompilerParams(dimension_semantics=semantics, vmem_limit_bytes=VMEM_LIMIT_BYTES)


def _const_spec(shape):
    nd = len(shape)
    return pl.BlockSpec(shape, lambda *_: (0,) * nd)


def _a_proj_kernel(x_ref, nw_ref, wlat_ref, wwt_ref, wz_ref, gcq_ref, gkv_ref, wuq_ref, wuk_ref, gq_ref, wiq_ref,
                   qlat_ref, qidx_ref, wt_ref, ckv_ref, kidx_ref, z_ref):
    h = _rms(x_ref[...], nw_ref[...]).astype(BF16)
    lat = _dot(h, wlat_ref[...])
    cq = _rms(lat[:, :D_CQ], gcq_ref[...]).astype(BF16)
    ckv_ref[...] = _rms(lat[:, D_CQ:D_CQ + D_C], gkv_ref[...]).astype(BF16)
    kidx_ref[...] = _rms(lat[:, D_CQ + D_C:]).astype(BF16)
    wt_ref[...] = _dot_nt(wwt_ref[...], h) * (H_I ** -0.5)
    z_ref[...] = _dot(h, wz_ref[...])
    qn = _dot(cq, wuq_ref[...]).astype(BF16)
    for hh in range(H_A):
        ql = _dot(qn[:, hh * D_NOPE:(hh + 1) * D_NOPE], wuk_ref[hh])
        qlat_ref[hh] = (_rms(ql, gq_ref[...]) * (D_C ** -0.5)).astype(BF16)
    qi = _dot(cq, wiq_ref[...]) * (D_I ** -0.5)
    for hh in range(H_I):
        qidx_ref[hh] = qi[:, hh * D_I:(hh + 1) * D_I].astype(BF16)


def _a_proj(x2, nw, wlat, wwt, wz, gcq, gkv, wuq, wuk, gq, wiq, tile):
    n, d = x2.shape
    grid = (n // tile,)
    tok = lambda w: pl.BlockSpec((tile, w), lambda i: (i, 0))
    return pl.pallas_call(
        _a_proj_kernel,
        grid=grid,
        in_specs=[tok(d), _const_spec(nw.shape), _const_spec(wlat.shape), _const_spec(wwt.shape),
                  _const_spec(wz.shape), _const_spec(gcq.shape), _const_spec(gkv.shape), _const_spec(wuq.shape),
                  _const_spec(wuk.shape), _const_spec(gq.shape), _const_spec(wiq.shape)],
        out_specs=[pl.BlockSpec((H_A, tile, D_C), lambda i: (0, i, 0)),
                   pl.BlockSpec((H_I, tile, D_I), lambda i: (0, i, 0)),
                   pl.BlockSpec((H_I, tile), lambda i: (0, i)),
                   tok(D_C), tok(D_I), tok(H_A * D_V)],
        out_shape=[jax.ShapeDtypeStruct((H_A, n, D_C), BF16),
                   jax.ShapeDtypeStruct((H_I, n, D_I), BF16),
                   jax.ShapeDtypeStruct((H_I, n), F32),
                   jax.ShapeDtypeStruct((n, D_C), BF16),
                   jax.ShapeDtypeStruct((n, D_I), BF16),
                   jax.ShapeDtypeStruct((n, H_A * D_V), F32)],
        compiler_params=_params("parallel"),
        name="a_proj",
    )(x2, nw, wlat, wwt, wz, gcq, gkv, wuq, wuk, gq, wiq)


def _a_attn_kernel(qidx_ref, wt_ref, kidx_ref, qlat_ref, ckv_ref, z_ref, bt_ref, wuv_ref,
                   y_ref, keys_ref, maskb_ref, lg_ref, mx_ref, ss_ref, oacc_ref, *, topk):
    qb = pl.program_id(1)
    n_kt = lax.shift_right_logical(qb, 1) + 1
    rows = H_A * Q_TILE
    qidx = qidx_ref[...].reshape(H_I * Q_TILE, D_I)
    qlat = qlat_ref[...].reshape(rows, D_C)
    q_pos = qb * Q_TILE + lax.broadcasted_iota(I32, (K_TILE, Q_TILE), 1)

    def idx_body(c, carry):
        k0 = pl.multiple_of(c * K_TILE, K_TILE)
        lg = _dot_nt(kidx_ref[pl.ds(k0, K_TILE), :], qidx)
        sc = jnp.zeros((K_TILE, Q_TILE), F32)
        for hh in range(H_I):
            sc = sc + jnp.maximum(lg[:, hh * Q_TILE:(hh + 1) * Q_TILE], 0.0) * wt_ref[hh:hh + 1, :]
        bits = lax.bitcast_convert_type(sc, I32)
        sign = lax.shift_right_arithmetic(bits, 31)
        skey = (bits ^ (sign & 0x7FFFFFFF)) - sign
        key_pos = k0 + lax.broadcasted_iota(I32, (K_TILE, Q_TILE), 0)
        skey = jnp.where(key_pos <= q_pos, skey, INT_MIN)
        keys_ref[c] = skey.reshape(K_TILE // 8, 8, Q_TILE)
        return carry

    lax.fori_loop(0, n_kt, idx_body, 0)

    def bit_body(i, thr):
        cand = thr + lax.shift_left(jnp.int32(1), 31 - i)

        def cnt_body(c, acc):
            return acc + jnp.sum((keys_ref[c] >= cand[None]).astype(I32), axis=0)

        acc = lax.fori_loop(0, n_kt, cnt_body, jnp.zeros((8, Q_TILE), I32))
        tot = jnp.sum(acc, axis=0, keepdims=True)
        return jnp.where(tot >= topk, cand, thr)

    thr = lax.fori_loop(0, 32, bit_body, jnp.full((8, Q_TILE), INT_MIN, I32))
    thr = jnp.maximum(thr, INT_MIN + 1)

    def mask_body(c, carry):
        sel = keys_ref[c] >= thr[None]
        neg = jnp.where(sel, 0.0, NEG_INF).reshape(K_TILE, Q_TILE)
        maskb_ref[c] = neg.T
        return carry

    lax.fori_loop(0, n_kt, mask_body, 0)

    mx_ref[...] = jnp.full(mx_ref.shape, NEG_INF, F32)

    def qk_body(c, carry):
        k0 = pl.multiple_of(c * K_TILE, K_TILE)
        lg = _dot_nt(qlat, ckv_ref[pl.ds(k0, K_TILE), :])
        t_idx = jnp.minimum(qb - 2 * c, 3)
        mb = maskb_ref[c]
        for hh in range(H_A):
            l = lg[hh * Q_TILE:(hh + 1) * Q_TILE, :] + (mb + bt_ref[t_idx, hh])
            lg_ref[c, hh] = l
            mx_ref[hh] = jnp.maximum(mx_ref[hh], l)
        return carry

    lax.fori_loop(0, n_kt, qk_body, 0)

    m = jnp.max(mx_ref[...], axis=-1, keepdims=True)
    ss_ref[...] = jnp.zeros(ss_ref.shape, F32)
    oacc_ref[...] = jnp.zeros(oacc_ref.shape, F32)

    def pv_body(c, carry):
        k0 = pl.multiple_of(c * K_TILE, K_TILE)
        p = jnp.exp(lg_ref[c] - m)
        ss_ref[...] += p
        oacc_ref[...] += _dot(p.reshape(rows, K_TILE).astype(BF16), ckv_ref[pl.ds(k0, K_TILE), :])
        return carry

    lax.fori_loop(0, n_kt, pv_body, 0)
    denom = jnp.sum(ss_ref[...], axis=-1, keepdims=True)
    o_lat = oacc_ref[...].reshape(H_A, Q_TILE, D_C) / denom

    for hh in range(H_A):
        oh = _dot(o_lat[hh].astype(BF16), wuv_ref[hh])
        zz = z_ref[:, hh * D_V:(hh + 1) * D_V]
        y_ref[:, hh * D_V:(hh + 1) * D_V] = (oh * (zz * _sigmoid(zz))).astype(BF16)


def _a_attn(qidx, wt, kidx, qlat, ckv, z, bt, wuv, batch, seq):
    n = batch * seq
    nq = seq // Q_TILE
    n_kt = seq // K_TILE
    topk = min(TOPK_MAX, seq // 4)
    grid = (batch, nq)
    kern = functools.partial(_a_attn_kernel, topk=topk)
    return pl.pallas_call(
        kern,
        grid=grid,
        in_specs=[pl.BlockSpec((H_I, Q_TILE, D_I), lambda b, q: (0, b * nq + q, 0)),
                  pl.BlockSpec((H_I, Q_TILE), lambda b, q: (0, b * nq + q)),
                  pl.BlockSpec((seq, D_I), lambda b, q: (b, 0)),
                  pl.BlockSpec((H_A, Q_TILE, D_C), lambda b, q: (0, b * nq + q, 0)),
                  pl.BlockSpec((seq, D_C), lambda b, q: (b, 0)),
                  pl.BlockSpec((Q_TILE, H_A * D_V), lambda b, q: (b * nq + q, 0)),
                  _const_spec(bt.shape), _const_spec(wuv.shape)],
        out_specs=pl.BlockSpec((Q_TILE, H_A * D_V), lambda b, q: (b * nq + q, 0)),
        out_shape=jax.ShapeDtypeStruct((n, H_A * D_V), BF16),
        scratch_shapes=[pltpu.VMEM((n_kt, K_TILE // 8, 8, Q_TILE), I32),
                        pltpu.VMEM((n_kt, Q_TILE, K_TILE), F32),
                        pltpu.VMEM((n_kt, H_A, Q_TILE, K_TILE), F32),
                        pltpu.VMEM((H_A, Q_TILE, K_TILE), F32),
                        pltpu.VMEM((H_A, Q_TILE, K_TILE), F32),
                        pltpu.VMEM((H_A * Q_TILE, D_C), F32)],
        compiler_params=_params("parallel", "arbitrary"),
        name="a_attn",
    )(qidx, wt, kidx, qlat, ckv, z, bt, wuv)


def _b_proj_kernel(x_ref, nw_ref, wqkv_ref, wba_ref, wbat_ref, wz_ref, cw_ref, alog_ref, dtb_ref, alogc_ref, dtbc_ref,
                   q_ref, k_ref, v_ref, beta_ref, g_ref, gt_ref, z_ref, buf_ref, *, tile):
    s = pl.program_id(1)
    h = _rms(x_ref[...], nw_ref[...]).astype(BF16)
    z_ref[...] = _dot(h, wz_ref[...])
    ba = _dot(h, wba_ref[...])
    beta_ref[...] = _sigmoid(ba[:, :H_B])
    g_ref[...] = -jnp.exp(alog_ref[...]) * _softplus(ba[:, H_B:2 * H_B] + dtb_ref[...])
    bat = _dot_nt(wbat_ref[...], h)
    gt_ref[...] = -jnp.exp(alogc_ref[...]) * _softplus(bat[H_B:, :] + dtbc_ref[...])

    @pl.when(s == 0)
    def _():
        buf_ref[0:8, :] = jnp.zeros((8, buf_ref.shape[1]), F32)

    width = H_B * D_K
    for sec, out_ref in enumerate((q_ref, k_ref, v_ref)):
        cols = slice(sec * width, (sec + 1) * width)
        pre = _dot(h, wqkv_ref[:, cols])
        buf_ref[8:8 + tile, cols] = pre
        acc = pre * cw_ref[CONV_W - 1:CONV_W, cols]
        for w in range(CONV_W - 1):
            acc = acc + buf_ref[8 - (CONV_W - 1) + w:8 - (CONV_W - 1) + w + tile, cols] * cw_ref[w:w + 1, cols]
        buf_ref[0:8, cols] = buf_ref[tile:tile + 8, cols]
        y = acc * _sigmoid(acc)
        if sec == 2:
            out_ref[...] = y
        else:
            scale = (D_K ** -0.5) if sec == 0 else 1.0
            for hh in range(H_B):
                yy = y[:, hh * D_K:(hh + 1) * D_K]
                nrm = lax.rsqrt(jnp.sum(yy * yy, axis=-1, keepdims=True) + EPS)
                out_ref[:, hh * D_K:(hh + 1) * D_K] = yy * (nrm * scale)


def _b_proj(x2, nw, wqkv, wba, wbat, wz, cw, alog, dtb, batch, seq, tile):
    n, d = x2.shape
    ns = seq // tile
    grid = (batch, ns)
    tok = lambda w: pl.BlockSpec((tile, w), lambda b, s: (b * ns + s, 0))
    width = H_B * D_K
    kern = functools.partial(_b_proj_kernel, tile=tile)
    return pl.pallas_call(
        kern,
        grid=grid,
        in_specs=[tok(d), _const_spec(nw.shape), _const_spec(wqkv.shape), _const_spec(wba.shape),
                  _const_spec(wbat.shape), _const_spec(wz.shape), _const_spec(cw.shape),
                  _const_spec((1, H_B)), _const_spec((1, H_B)), _const_spec((H_B, 1)), _const_spec((H_B, 1))],
        out_specs=[tok(width), tok(width), tok(width), tok(H_B), tok(H_B),
                   pl.BlockSpec((H_B, tile), lambda b, s: (0, b * ns + s)), tok(width)],
        out_shape=[jax.ShapeDtypeStruct((n, width), F32)] * 3
        + [jax.ShapeDtypeStruct((n, H_B), F32)] * 2
        + [jax.ShapeDtypeStruct((H_B, n), F32), jax.ShapeDtypeStruct((n, width), F32)],
        scratch_shapes=[pltpu.VMEM((tile + 8, 3 * width), F32)],
        compiler_params=_params("parallel", "arbitrary"),
        name="b_proj",
    )(x2, nw, wqkv, wba, wbat, wz, cw, alog.reshape(1, H_B), dtb.reshape(1, H_B),
      alog.reshape(H_B, 1), dtb.reshape(H_B, 1))


def _b_gdn_kernel(q_ref, k_ref, v_ref, z_ref, beta_ref, g_ref, gt_ref, go_ref, y_ref, state_ref):
    hb = pl.program_id(1)
    s = pl.program_id(2)
    t = GDN_TILE
    n_chunks = t // CHUNK
    log_c = int(math.log2(CHUNK))
    heads = range(GDN_HEADS)

    @pl.when(s == 0)
    def _():
        state_ref[...] = jnp.zeros(state_ref.shape, F32)

    r = lax.broadcasted_iota(I32, (t, t), 0)
    c = lax.broadcasted_iota(I32, (t, t), 1)
    xs = r ^ c
    same = xs < CHUNK
    incl = same & (r >= c)
    upper = same & (r <= c)
    eye = (r == c).astype(F32)
    level = [(lax.shift_right_logical(xs, lb) == 1) & ((r & (1 << lb)) != 0) for lb in range(log_c)]
    lane8 = lax.broadcasted_iota(I32, (t, H_B), 1)
    sub8 = lax.broadcasted_iota(I32, (H_B, t), 0)

    b_col, decay, eg_col, ekl_col, egl_col = [], [], [], [], []
    for j in heads:
        head = hb * GDN_HEADS + j
        g_col = jnp.sum(jnp.where(lane8 == head, g_ref[...], 0.0), axis=1, keepdims=True)
        b_col.append(jnp.sum(jnp.where(lane8 == head, beta_ref[...], 0.0), axis=1, keepdims=True))
        g_row = jnp.sum(jnp.where(sub8 == head, gt_ref[...], 0.0), axis=0, keepdims=True)
        g_rows = jnp.broadcast_to(g_row, (t, t))
        gc_col = jnp.sum(jnp.where(incl, g_rows, 0.0), axis=1, keepdims=True)
        gl_col = jnp.sum(jnp.where(same, g_rows, 0.0), axis=1, keepdims=True)
        gc_row = jnp.sum(jnp.where(upper, jnp.broadcast_to(g_col, (t, t)), 0.0), axis=0, keepdims=True)
        decay.append(jnp.where(incl, jnp.exp(jnp.minimum(gc_col - gc_row, 0.0)), 0.0))
        eg_col.append(jnp.exp(gc_col))
        ekl_col.append(jnp.exp(gl_col - gc_col))
        egl_col.append(jnp.exp(gl_col))

    lmat, aqk, rhs, q_dec, k_dec = [], [], [], [], []
    for j in heads:
        hs = slice(j * D_K, (j + 1) * D_K)
        qf, kf, vf = q_ref[:, hs], k_ref[:, hs], v_ref[:, hs]
        kb = kf * b_col[j]
        k16 = kf.astype(BF16)
        lmat.append(_dot_nt(kb.astype(BF16), k16) * decay[j])
        aqk.append(jnp.where(incl, _dot_nt(qf.astype(BF16), k16) * decay[j], 0.0).astype(BF16))
        rhs.append(jnp.concatenate([(vf * b_col[j]).astype(BF16), (kb * eg_col[j]).astype(BF16)], axis=1))
        q_dec.append(qf * eg_col[j])
        k_dec.append((kf * ekl_col[j]).astype(BF16))

    tinv = [eye - jnp.where(level[0], lmat[j], 0.0) for j in heads]
    for lb in range(1, log_c):
        t16 = [tinv[j].astype(BF16) for j in heads]
        y16 = [_dot(jnp.where(level[lb], lmat[j], 0.0).astype(BF16), t16[j]).astype(BF16) for j in heads]
        tinv = [tinv[j] - _dot(t16[j], y16[j]) for j in heads]

    sol16 = [_dot(tinv[j].astype(BF16), rhs[j]).astype(BF16) for j in heads]
    aux = [_dot(aqk[j], sol16[j]) for j in heads]
    q_eff = [(q_dec[j] - aux[j][:, D_VB:]).astype(BF16) for j in heads]
    kw = [[_dot_tn(k_dec[j][ci * CHUNK:(ci + 1) * CHUNK], sol16[j][ci * CHUNK:(ci + 1) * CHUNK])
           for ci in range(n_chunks)] for j in heads]

    state = [state_ref[j] for j in heads]
    outs = [[] for _ in heads]
    for ci in range(n_chunks):
        rs = slice(ci * CHUNK, (ci + 1) * CHUNK)
        for j in heads:
            s16 = state[j].astype(BF16)
            outs[j].append(_dot(q_eff[j][rs], s16) + aux[j][rs, :D_VB])
            state[j] = (state[j] * egl_col[j][ci * CHUNK:ci * CHUNK + 1, :] + kw[j][ci][:, :D_VB]
                        - _dot(kw[j][ci][:, D_VB:].astype(BF16), s16))

    for j in heads:
        hs = slice(j * D_K, (j + 1) * D_K)
        state_ref[j] = state[j]
        o = jnp.concatenate(outs[j], axis=0)
        zz = z_ref[:, hs]
        y_ref[:, hs] = (_rms(o, go_ref[...]) * (zz * _sigmoid(zz))).astype(BF16)


def _b_gdn(q, k, v, z, beta, g, gt, go, batch, seq):
    n = batch * seq
    ns = seq // GDN_TILE
    nh = H_B // GDN_HEADS
    grid = (batch, nh, ns)
    wide = pl.BlockSpec((GDN_TILE, GDN_HEADS * D_K), lambda b, h, s: (b * ns + s, h))
    narrow = pl.BlockSpec((GDN_TILE, H_B), lambda b, h, s: (b * ns + s, 0))
    return pl.pallas_call(
        _b_gdn_kernel,
        grid=grid,
        in_specs=[wide, wide, wide, wide, narrow, narrow,
                  pl.BlockSpec((H_B, GDN_TILE), lambda b, h, s: (0, b * ns + s)),
                  _const_spec(go.shape)],
        out_specs=wide,
        out_shape=jax.ShapeDtypeStruct((n, H_B * D_VB), BF16),
        scratch_shapes=[pltpu.VMEM((GDN_HEADS, D_K, D_VB), F32)],
        compiler_params=_params("parallel", "parallel", "arbitrary"),
        name="b_gdn",
    )(q, k, v, z, beta, g, gt, go)


def _out_ple_kernel(x_ref, y_ref, p_ref, wout_ref, pn_ref, wgate_ref, wproj_ref, o_ref):
    x1 = x_ref[...] + _dot(y_ref[...], wout_ref[...])
    hn = _rms(x1, pn_ref[...]).astype(BF16)
    gate = _sigmoid(_dot(hn, wgate_ref[...]))
    o_ref[...] = x1 + gate * _dot(p_ref[...].astype(BF16), wproj_ref[...])


def _out_ple(x2, y, p2, wout, pn, wgate, wproj, tile):
    n, d = x2.shape
    tok = lambda w: pl.BlockSpec((tile, w), lambda i: (i, 0))
    return pl.pallas_call(
        _out_ple_kernel,
        grid=(n // tile,),
        in_specs=[tok(d), tok(y.shape[1]), tok(p2.shape[1]), _const_spec(wout.shape), _const_spec(pn.shape),
                  _const_spec(wgate.shape), _const_spec(wproj.shape)],
        out_specs=tok(d),
        out_shape=jax.ShapeDtypeStruct((n, d), F32),
        compiler_params=_params("parallel"),
        name="out_ple",
    )(x2, y, p2, wout, pn, wgate, wproj)


def _t5_bucket(rel):
    max_exact = N_BUCKETS // 2
    rel = jnp.maximum(rel, 0)
    rel_f = jnp.maximum(rel, 1).astype(F32)
    log_ratio = jnp.log(rel_f / max_exact) / math.log(MAX_DISTANCE / max_exact)
    large = max_exact + (log_ratio * (N_BUCKETS - max_exact)).astype(I32)
    large = jnp.minimum(large, N_BUCKETS - 1)
    return jnp.where(rel < max_exact, rel, large)


def _bias_tiles(rel_bias):
    assert 3 * Q_TILE - (K_TILE - 1) >= MAX_DISTANCE
    dist = jnp.arange(-(K_TILE - 1), 4 * Q_TILE, dtype=I32)
    table = rel_bias.astype(F32)[_t5_bucket(dist)].T
    span = Q_TILE + K_TILE
    tiles = []
    for cls in range(4):
        w = table[:, cls * Q_TILE:cls * Q_TILE + span]
        skew = jnp.tile(w, (1, Q_TILE + 1))[:, :Q_TILE * (span + 1)].reshape(H_A, Q_TILE, span + 1)
        tiles.append(skew[:, :, :K_TILE][:, :, ::-1])
    return jnp.stack(tiles)


def kernel(x, p, norm_w, a_w_in, a_g_cq, a_w_uq, a_w_uk, a_g_q, a_g_kv, a_w_iq, a_w_uv, a_w_out, rel_bias, b_w_in, b_conv_w, b_a_log, b_dt_bias, b_g_o, b_w_out, ple_norm, ple_w_gate, ple_w_proj):
    batch, seq, d_model = x.shape
    depth = p.shape[0]
    n = batch * seq
    assert seq % K_TILE == 0 and seq % GDN_TILE == 0 and H_B % GDN_HEADS == 0
    x2 = x.reshape(n, d_model)
    p2 = p.reshape(depth, n, p.shape[-1])
    bt = _bias_tiles(rel_bias)
    row = lambda a: a.reshape(1, -1).astype(F32)
    n_lat = D_CQ + D_C + D_I
    b_qkv = H_B * (2 * D_K + D_VB)
    for i in range(depth):
        j = i // 2
        if i % 2 == 0:
            w_in = a_w_in[j]
            wlat = w_in[:, :n_lat].astype(BF16)
            wwt = w_in[:, n_lat:n_lat + H_I].T.astype(BF16)
            wz = w_in[:, n_lat + H_I:].astype(BF16)
            qlat, qidx, wt, ckv, kidx, z = _a_proj(
                x2, row(norm_w[i]), wlat, wwt, wz, row(a_g_cq[j]), row(a_g_kv[j]),
                a_w_uq[j].reshape(D_CQ, H_A * D_NOPE).astype(BF16), a_w_uk[j].astype(BF16), row(a_g_q[j]),
                a_w_iq[j].reshape(D_CQ, H_I * D_I).astype(BF16), tile=512)
            y = _a_attn(qidx, wt, kidx, qlat, ckv, z, bt, a_w_uv[j].astype(BF16), batch, seq)
            w_out = a_w_out[j]
        else:
            w_in = b_w_in[j]
            wqkv = w_in[:, :b_qkv].astype(BF16)
            w_ba = w_in[:, b_qkv:b_qkv + 2 * H_B]
            wba = jnp.pad(w_ba, ((0, 0), (0, LANES - 2 * H_B))).astype(BF16)
            wz = w_in[:, b_qkv + 2 * H_B:].astype(BF16)
            q, k, v, beta, g, gt, z = _b_proj(
                x2, row(norm_w[i]), wqkv, wba, w_ba.T.astype(BF16), wz, b_conv_w[j].astype(F32),
                b_a_log[j].astype(F32), b_dt_bias[j].astype(F32), batch, seq, tile=256)
            y = _b_gdn(q, k, v, z, beta, g, gt, row(b_g_o[j]), batch, seq)
            w_out = b_w_out[j]
        x2 = _out_ple(x2, y, p2[i], w_out.astype(BF16), row(ple_norm[i]), ple_w_gate[i].astype(BF16),
                      ple_w_proj[i].astype(BF16), tile=512)
    return x2.reshape(batch, seq, d_model)
```

```python
import functools
import math

import jax
import jax.numpy as jnp
from jax import lax
from jax.experimental import pallas as pl
from jax.experimental.pallas import tpu as pltpu

F32 = jnp.float32
BF16 = jnp.bfloat16
I32 = jnp.int32
I16 = jnp.int16

EPS = 1e-6
NEG_INF = -1e30
LOG2_E = math.log2(math.e)
INT_MIN = -(2 ** 31)

LANES = 128
VMEM_LIMIT_BYTES = 56 * 1024 * 1024

H_A = 8
D_C = 256
D_CQ = 256
D_I = 128
H_I = 8
D_NOPE = 128
D_V = 128
TOPK_MAX = 256
N_BUCKETS = 32
MAX_DISTANCE = 128
Q_TILE = 256
K_TILE = 256
BIAS_CLASSES = -(-(MAX_DISTANCE + K_TILE - 1) // Q_TILE) + 1
H_B = 8
D_K = 128
D_VB = 128
CONV_W = 4
CHUNK = 64
GDN_TILE = 256
GDN_HEADS = 4


def _dot(a, b):
    return jnp.dot(a, b, preferred_element_type=F32)


def _dot_nt(a, b):
    return lax.dot_general(a, b, (((1,), (1,)), ((), ())), preferred_element_type=F32)


def _dot_tn(a, b):
    return lax.dot_general(a, b, (((0,), (0,)), ((), ())), preferred_element_type=F32)


def _rms(x, gain=None):
    y = x * lax.rsqrt(jnp.mean(x * x, axis=-1, keepdims=True) + EPS)
    return y if gain is None else y * gain


def _sigmoid(x):
    return 1.0 / (1.0 + jnp.exp(-x))


def _softplus(x):
    return jnp.maximum(x, 0.0) + jnp.log1p(jnp.exp(-jnp.abs(x)))


def _params(*semantics):
    return pltpu.CompilerParams(dimension_semantics=semantics, vmem_limit_bytes=VMEM_LIMIT_BYTES)


def _const_spec(shape):
    nd = len(shape)
    return pl.BlockSpec(shape, lambda *_: (0,) * nd)


def _a_proj_kernel(x_ref, nw_ref, wlat_ref, wwt_ref, wz_ref, gcq_ref, gkv_ref, wuq_ref, wuk_ref, gq_ref, wiq_ref,
                   qlat_ref, qidx_ref, wt_ref, ckv_ref, kidx_ref, z_ref):
    h = _rms(x_ref[...], nw_ref[...]).astype(BF16)
    lat = _dot(h, wlat_ref[...])
    cq = _rms(lat[:, :D_CQ], gcq_ref[...]).astype(BF16)
    ckv_ref[...] = _rms(lat[:, D_CQ:D_CQ + D_C], gkv_ref[...]).astype(BF16)
    kidx_ref[...] = _rms(lat[:, D_CQ + D_C:]).astype(BF16)
    wt_ref[...] = _dot_nt(wwt_ref[...], h) * (H_I ** -0.5)
    z_ref[...] = _dot(h, wz_ref[...])
    qn = _dot(cq, wuq_ref[...]).astype(BF16)
    for hh in range(H_A):
        ql = _dot(qn[:, hh * D_NOPE:(hh + 1) * D_NOPE], wuk_ref[hh])
        qlat_ref[hh] = (_rms(ql, gq_ref[...]) * (D_C ** -0.5 * LOG2_E)).astype(BF16)
    qi = _dot(cq, wiq_ref[...]) * (D_I ** -0.5)
    for hh in range(H_I):
        qidx_ref[hh] = qi[:, hh * D_I:(hh + 1) * D_I].astype(BF16)


def _a_proj(x2, nw, wlat, wwt, wz, gcq, gkv, wuq, wuk, gq, wiq, tile):
    n, d = x2.shape
    grid = (n // tile,)
    tok = lambda w: pl.BlockSpec((tile, w), lambda i: (i, 0))
    return pl.pallas_call(
        _a_proj_kernel,
        grid=grid,
        in_specs=[tok(d), _const_spec(nw.shape), _const_spec(wlat.shape), _const_spec(wwt.shape),
                  _const_spec(wz.shape), _const_spec(gcq.shape), _const_spec(gkv.shape), _const_spec(wuq.shape),
                  _const_spec(wuk.shape), _const_spec(gq.shape), _const_spec(wiq.shape)],
        out_specs=[pl.BlockSpec((H_A, tile, D_C), lambda i: (0, i, 0)),
                   pl.BlockSpec((H_I, tile, D_I), lambda i: (0, i, 0)),
                   pl.BlockSpec((H_I, tile), lambda i: (0, i)),
                   tok(D_C), tok(D_I), tok(H_A * D_V)],
        out_shape=[jax.ShapeDtypeStruct((H_A, n, D_C), BF16),
                   jax.ShapeDtypeStruct((H_I, n, D_I), BF16),
                   jax.ShapeDtypeStruct((H_I, n), F32),
                   jax.ShapeDtypeStruct((n, D_C), BF16),
                   jax.ShapeDtypeStruct((n, D_I), BF16),
                   jax.ShapeDtypeStruct((n, H_A * D_V), F32)],
        compiler_params=_params("parallel"),
        name="a_proj",
    )(x2, nw, wlat, wwt, wz, gcq, gkv, wuq, wuk, gq, wiq)


def _a_attn_kernel(qidx_ref, wt_ref, kidx_ref, qlat_ref, ckv_ref, z_ref, bt_ref, wuv_ref,
                   y_ref, keys_ref, hi_ref, lo_ref, lg_ref, mx_ref, ss_ref, oacc_ref, *, topk, n_q):
    qb = pl.program_id(1)
    n_kt = qb + 1
    rows = H_A * Q_TILE
    half = K_TILE // 2
    qidx = qidx_ref[...].reshape(H_I * Q_TILE, D_I)
    qlat = qlat_ref[...].reshape(rows, D_C)

    def for_tile_groups(body):
        def pair(i, carry):
            body(2 * i, 2)
            return carry

        lax.fori_loop(0, lax.shift_right_logical(n_kt, 1), pair, 0)

        @pl.when((n_kt & 1) == 1)
        def _():
            body(n_kt - 1, 1)

    def idx_tiles(c0, cnt):
        k0 = pl.multiple_of(c0 * K_TILE, K_TILE)
        lg = _dot_nt(kidx_ref[pl.ds(k0, cnt * K_TILE), :], qidx)
        sc = jnp.zeros((cnt * K_TILE, Q_TILE), F32)
        for hh in range(H_I):
            sc = sc + jnp.maximum(lg[:, hh * Q_TILE:(hh + 1) * Q_TILE], 0.0) * wt_ref[hh:hh + 1, :]
        bits = lax.bitcast_convert_type(sc, I32)
        sign = lax.shift_right_arithmetic(bits, 31)
        skey = (bits ^ (sign & 0x7FFFFFFF)) - sign
        key_pos = k0 + lax.broadcasted_iota(I32, (cnt * K_TILE, Q_TILE), 0)
        q_pos = qb * Q_TILE + lax.broadcasted_iota(I32, (cnt * K_TILE, Q_TILE), 1)
        skey = jnp.where(key_pos <= q_pos, skey, INT_MIN)
        hi = lax.shift_right_arithmetic(skey, 16).astype(I16)
        lo = ((skey & 0xFFFF) - 32768).astype(I16)
        for u in range(cnt):
            ks = slice(u * K_TILE, (u + 1) * K_TILE)
            keys_ref[c0 + u] = skey[ks]
            hi_ref[c0 + u] = hi[ks].reshape(K_TILE // 16, 16, Q_TILE)
            lo_ref[c0 + u] = lo[ks].reshape(K_TILE // 16, 16, Q_TILE)

    for_tile_groups(idx_tiles)

    one = jnp.ones((), BF16)
    zero = jnp.zeros((), BF16)

    def threshold(n_tiles):
        def count(ref16, bound16, strict):
            acc = jnp.zeros((16, Q_TILE), BF16)
            for c in range(n_tiles):
                tile = ref16[c]
                hit = jnp.where(tile > bound16[None] if strict else tile >= bound16[None], one, zero)
                for u in range(K_TILE // 16):
                    acc = acc + hit[u]
            return jnp.sum(acc.astype(F32), axis=0, keepdims=True)

        def kth_largest(ref16, need):
            def bit_body(i, thr):
                cand = thr + lax.shift_left(jnp.int32(1), 15 - i)
                return jnp.where(count(ref16, cand.astype(I16), False) >= need, cand, thr)

            return lax.fori_loop(0, 16, bit_body, jnp.full((16, Q_TILE), -32768, I32))

        hi_thr = kth_largest(hi_ref, float(topk))
        hi_thr16 = hi_thr.astype(I16)
        need_lo = float(topk) - count(hi_ref, hi_thr16, True)
        for c in range(n_tiles):
            lo_ref[c] = jnp.where(hi_ref[c] == hi_thr16[None], lo_ref[c], jnp.int16(-32768))
        lo_thr = kth_largest(lo_ref, need_lo)
        return hi_thr[0:1] * 65536 + (lo_thr[0:1] + 32768)

    thr = lax.switch(qb, [functools.partial(threshold, n) for n in range(1, n_q + 1)])
    thr = jnp.maximum(thr, INT_MIN + 1)

    mx_ref[...] = jnp.full(mx_ref.shape, NEG_INF, F32)

    def qk_tiles(c0, cnt):
        k0 = pl.multiple_of(c0 * K_TILE, K_TILE)
        lg = _dot_nt(qlat, ckv_ref[pl.ds(k0, cnt * K_TILE), :])
        for u in range(cnt):
            c = c0 + u
            mb = jnp.where(keys_ref[c] >= thr, 0.0, NEG_INF).T
            t_idx = jnp.minimum(qb - c, BIAS_CLASSES - 1)
            for hh in range(H_A):
                l = lg[hh * Q_TILE:(hh + 1) * Q_TILE, u * K_TILE:(u + 1) * K_TILE] + (mb + bt_ref[t_idx, hh])
                lg_ref[c, hh] = l
                mx_ref[hh] = jnp.maximum(mx_ref[hh], jnp.maximum(l[:, :half], l[:, half:]))

    for_tile_groups(qk_tiles)

    m = jnp.max(mx_ref[...], axis=-1, keepdims=True)
    ss_ref[...] = jnp.zeros(ss_ref.shape, F32)
    oacc_ref[...] = jnp.zeros(oacc_ref.shape, F32)

    def pv_tiles(c0, cnt):
        k0 = pl.multiple_of(c0 * K_TILE, K_TILE)
        ps = [jnp.exp2(lg_ref[c0 + u] - m) for u in range(cnt)]
        part = ps[0][:, :, :half] + ps[0][:, :, half:]
        for p in ps[1:]:
            part = part + (p[:, :, :half] + p[:, :, half:])
        ss_ref[...] += part
        p16 = jnp.concatenate([p.reshape(rows, K_TILE).astype(BF16) for p in ps], axis=1)
        oacc_ref[...] += _dot(p16, ckv_ref[pl.ds(k0, cnt * K_TILE), :])

    for_tile_groups(pv_tiles)
    denom = jnp.sum(ss_ref[...], axis=-1, keepdims=True)
    o_lat = oacc_ref[...].reshape(H_A, Q_TILE, D_C) / denom

    for hh in range(H_A):
        oh = _dot(o_lat[hh].astype(BF16), wuv_ref[hh])
        zz = z_ref[:, hh * D_V:(hh + 1) * D_V]
        y_ref[:, hh * D_V:(hh + 1) * D_V] = (oh * (zz * _sigmoid(zz))).astype(BF16)


def _a_attn(qidx, wt, kidx, qlat, ckv, z, bt, wuv, batch, seq):
    n = batch * seq
    nq = seq // Q_TILE
    n_kt = seq // K_TILE
    topk = min(TOPK_MAX, seq // 4)
    grid = (batch, nq)
    kern = functools.partial(_a_attn_kernel, topk=topk, n_q=nq)
    return pl.pallas_call(
        kern,
        grid=grid,
        in_specs=[pl.BlockSpec((H_I, Q_TILE, D_I), lambda b, q: (0, b * nq + q, 0)),
                  pl.BlockSpec((H_I, Q_TILE), lambda b, q: (0, b * nq + q)),
                  pl.BlockSpec((seq, D_I), lambda b, q: (b, 0)),
                  pl.BlockSpec((H_A, Q_TILE, D_C), lambda b, q: (0, b * nq + q, 0)),
                  pl.BlockSpec((seq, D_C), lambda b, q: (b, 0)),
                  pl.BlockSpec((Q_TILE, H_A * D_V), lambda b, q: (b * nq + q, 0)),
                  pl.BlockSpec(bt.shape, lambda b, q: (0, 0, 0, 0), pipeline_mode=pl.Buffered(1)),
                  pl.BlockSpec(wuv.shape, lambda b, q: (0, 0, 0), pipeline_mode=pl.Buffered(1))],
        out_specs=pl.BlockSpec((Q_TILE, H_A * D_V), lambda b, q: (b * nq + q, 0)),
        out_shape=jax.ShapeDtypeStruct((n, H_A * D_V), BF16),
        scratch_shapes=[pltpu.VMEM((n_kt, K_TILE, Q_TILE), I32),
                        pltpu.VMEM((n_kt, K_TILE // 16, 16, Q_TILE), I16),
                        pltpu.VMEM((n_kt, K_TILE // 16, 16, Q_TILE), I16),
                        pltpu.VMEM((n_kt, H_A, Q_TILE, K_TILE), F32),
                        pltpu.VMEM((H_A, Q_TILE, K_TILE // 2), F32),
                        pltpu.VMEM((H_A, Q_TILE, K_TILE // 2), F32),
                        pltpu.VMEM((H_A * Q_TILE, D_C), F32)],
        compiler_params=_params("parallel", "arbitrary"),
        name="a_attn",
    )(qidx, wt, kidx, qlat, ckv, z, bt, wuv)


def _b_proj_kernel(x_ref, nw_ref, wqkv_ref, wba_ref, wbat_ref, wz_ref, cw_ref, alog_ref, dtb_ref, alogc_ref, dtbc_ref,
                   q_ref, k_ref, v_ref, beta_ref, g_ref, gt_ref, z_ref, buf_ref, *, tile):
    s = pl.program_id(1)
    h = _rms(x_ref[...], nw_ref[...]).astype(BF16)
    z_ref[...] = _dot(h, wz_ref[...])
    ba = _dot(h, wba_ref[...])
    beta_ref[...] = _sigmoid(ba[:, :H_B])
    g_ref[...] = -jnp.exp(alog_ref[...]) * _softplus(ba[:, H_B:2 * H_B] + dtb_ref[...])
    bat = _dot_nt(wbat_ref[...], h)
    gt_ref[...] = -jnp.exp(alogc_ref[...]) * _softplus(bat[H_B:, :] + dtbc_ref[...])

    @pl.when(s == 0)
    def _():
        buf_ref[0:8, :] = jnp.zeros((8, buf_ref.shape[1]), F32)

    width = H_B * D_K
    for sec, out_ref in enumerate((q_ref, k_ref, v_ref)):
        cols = slice(sec * width, (sec + 1) * width)
        pre = _dot(h, wqkv_ref[:, cols])
        buf_ref[8:8 + tile, cols] = pre
        acc = pre * cw_ref[CONV_W - 1:CONV_W, cols]
        for w in range(CONV_W - 1):
            acc = acc + buf_ref[8 - (CONV_W - 1) + w:8 - (CONV_W - 1) + w + tile, cols] * cw_ref[w:w + 1, cols]
        buf_ref[0:8, cols] = buf_ref[tile:tile + 8, cols]
        y = acc * _sigmoid(acc)
        if sec == 2:
            out_ref[...] = y
        else:
            scale = (D_K ** -0.5) if sec == 0 else 1.0
            for hh in range(H_B):
                yy = y[:, hh * D_K:(hh + 1) * D_K]
                nrm = lax.rsqrt(jnp.sum(yy * yy, axis=-1, keepdims=True) + EPS)
                out_ref[:, hh * D_K:(hh + 1) * D_K] = yy * (nrm * scale)


def _b_proj(x2, nw, wqkv, wba, wbat, wz, cw, alog, dtb, batch, seq, tile):
    n, d = x2.shape
    ns = seq // tile
    grid = (batch, ns)
    tok = lambda w: pl.BlockSpec((tile, w), lambda b, s: (b * ns + s, 0))
    width = H_B * D_K
    kern = functools.partial(_b_proj_kernel, tile=tile)
    return pl.pallas_call(
        kern,
        grid=grid,
        in_specs=[tok(d), _const_spec(nw.shape), _const_spec(wqkv.shape), _const_spec(wba.shape),
                  _const_spec(wbat.shape), _const_spec(wz.shape), _const_spec(cw.shape),
                  _const_spec((1, H_B)), _const_spec((1, H_B)), _const_spec((H_B, 1)), _const_spec((H_B, 1))],
        out_specs=[tok(width), tok(width), tok(width), tok(H_B), tok(H_B),
                   pl.BlockSpec((H_B, tile), lambda b, s: (0, b * ns + s)), tok(width)],
        out_shape=[jax.ShapeDtypeStruct((n, width), F32)] * 3
        + [jax.ShapeDtypeStruct((n, H_B), F32)] * 2
        + [jax.ShapeDtypeStruct((H_B, n), F32), jax.ShapeDtypeStruct((n, width), F32)],
        scratch_shapes=[pltpu.VMEM((tile + 8, 3 * width), F32)],
        compiler_params=_params("parallel", "arbitrary"),
        name="b_proj",
    )(x2, nw, wqkv, wba, wbat, wz, cw, alog.reshape(1, H_B), dtb.reshape(1, H_B),
      alog.reshape(H_B, 1), dtb.reshape(H_B, 1))


def _b_gdn_kernel(q_ref, k_ref, v_ref, z_ref, beta_ref, g_ref, gt_ref, go_ref, y_ref, state_ref):
    hb = pl.program_id(1)
    s = pl.program_id(2)
    t = GDN_TILE
    n_chunks = t // CHUNK
    log_c = int(math.log2(CHUNK))
    heads = range(GDN_HEADS)

    @pl.when(s == 0)
    def _():
        state_ref[...] = jnp.zeros(state_ref.shape, F32)

    r = lax.broadcasted_iota(I32, (t, t), 0)
    c = lax.broadcasted_iota(I32, (t, t), 1)
    xs = r ^ c
    same = xs < CHUNK
    incl = same & (r >= c)
    upper = same & (r <= c)
    eye = (r == c).astype(F32)
    level = [(lax.shift_right_logical(xs, lb) == 1) & ((r & (1 << lb)) != 0) for lb in range(log_c)]
    lane8 = lax.broadcasted_iota(I32, (t, H_B), 1)
    sub8 = lax.broadcasted_iota(I32, (H_B, t), 0)

    b_col, decay, eg_col, ekl_col, egl_col = [], [], [], [], []
    for j in heads:
        head = hb * GDN_HEADS + j
        g_col = jnp.sum(jnp.where(lane8 == head, g_ref[...], 0.0), axis=1, keepdims=True)
        b_col.append(jnp.sum(jnp.where(lane8 == head, beta_ref[...], 0.0), axis=1, keepdims=True))
        g_row = jnp.sum(jnp.where(sub8 == head, gt_ref[...], 0.0), axis=0, keepdims=True)
        g_rows = jnp.broadcast_to(g_row, (t, t))
        gc_col = jnp.sum(jnp.where(incl, g_rows, 0.0), axis=1, keepdims=True)
        gl_col = jnp.sum(jnp.where(same, g_rows, 0.0), axis=1, keepdims=True)
        gc_row = jnp.sum(jnp.where(upper, jnp.broadcast_to(g_col, (t, t)), 0.0), axis=0, keepdims=True)
        decay.append(jnp.where(incl, jnp.exp(jnp.minimum(gc_col - gc_row, 0.0)), 0.0))
        eg_col.append(jnp.exp(gc_col))
        ekl_col.append(jnp.exp(gl_col - gc_col))
        egl_col.append(jnp.exp(gl_col))

    lmat, aqk, rhs, q_dec, k_dec = [], [], [], [], []
    for j in heads:
        hs = slice(j * D_K, (j + 1) * D_K)
        qf, kf, vf = q_ref[:, hs], k_ref[:, hs], v_ref[:, hs]
        kb = kf * b_col[j]
        k16 = kf.astype(BF16)
        lmat.append(_dot_nt(kb.astype(BF16), k16) * decay[j])
        aqk.append(jnp.where(incl, _dot_nt(qf.astype(BF16), k16) * decay[j], 0.0).astype(BF16))
        rhs.append(jnp.concatenate([(vf * b_col[j]).astype(BF16), (kb * eg_col[j]).astype(BF16)], axis=1))
        q_dec.append(qf * eg_col[j])
        k_dec.append((kf * ekl_col[j]).astype(BF16))

    tinv = [eye - jnp.where(level[0], lmat[j], 0.0) for j in heads]
    for lb in range(1, log_c):
        t16 = [tinv[j].astype(BF16) for j in heads]
        y16 = [_dot(jnp.where(level[lb], lmat[j], 0.0).astype(BF16), t16[j]).astype(BF16) for j in heads]
        tinv = [tinv[j] - _dot(t16[j], y16[j]) for j in heads]

    sol16 = [_dot(tinv[j].astype(BF16), rhs[j]).astype(BF16) for j in heads]
    aux = [_dot(aqk[j], sol16[j]) for j in heads]
    q_eff = [(q_dec[j] - aux[j][:, D_VB:]).astype(BF16) for j in heads]
    kw = [[_dot_tn(k_dec[j][ci * CHUNK:(ci + 1) * CHUNK], sol16[j][ci * CHUNK:(ci + 1) * CHUNK])
           for ci in range(n_chunks)] for j in heads]

    state = [state_ref[j] for j in heads]
    outs = [[] for _ in heads]
    for ci in range(n_chunks):
        rs = slice(ci * CHUNK, (ci + 1) * CHUNK)
        for j in heads:
            s16 = state[j].astype(BF16)
            outs[j].append(_dot(q_eff[j][rs], s16) + aux[j][rs, :D_VB])
            state[j] = (state[j] * egl_col[j][ci * CHUNK:ci * CHUNK + 1, :] + kw[j][ci][:, :D_VB]
                        - _dot(kw[j][ci][:, D_VB:].astype(BF16), s16))

    for j in heads:
        hs = slice(j * D_K, (j + 1) * D_K)
        state_ref[j] = state[j]
        o = jnp.concatenate(outs[j], axis=0)
        zz = z_ref[:, hs]
        y_ref[:, hs] = (_rms(o, go_ref[...]) * (zz * _sigmoid(zz))).astype(BF16)


def _b_gdn(q, k, v, z, beta, g, gt, go, batch, seq):
    n = batch * seq
    ns = seq // GDN_TILE
    nh = H_B // GDN_HEADS
    grid = (batch, nh, ns)
    wide = pl.BlockSpec((GDN_TILE, GDN_HEADS * D_K), lambda b, h, s: (b * ns + s, h))
    narrow = pl.BlockSpec((GDN_TILE, H_B), lambda b, h, s: (b * ns + s, 0))
    return pl.pallas_call(
        _b_gdn_kernel,
        grid=grid,
        in_specs=[wide, wide, wide, wide, narrow, narrow,
                  pl.BlockSpec((H_B, GDN_TILE), lambda b, h, s: (0, b * ns + s)),
                  _const_spec(go.shape)],
        out_specs=wide,
        out_shape=jax.ShapeDtypeStruct((n, H_B * D_VB), BF16),
        scratch_shapes=[pltpu.VMEM((GDN_HEADS, D_K, D_VB), F32)],
        compiler_params=_params("parallel", "parallel", "arbitrary"),
        name="b_gdn",
    )(q, k, v, z, beta, g, gt, go)


def _out_ple_kernel(x_ref, y_ref, p_ref, wout_ref, pn_ref, wgate_ref, wproj_ref, o_ref):
    x1 = x_ref[...] + _dot(y_ref[...], wout_ref[...])
    hn = _rms(x1, pn_ref[...]).astype(BF16)
    gate = _sigmoid(_dot(hn, wgate_ref[...]))
    o_ref[...] = x1 + gate * _dot(p_ref[...].astype(BF16), wproj_ref[...])


def _out_ple(x2, y, p2, wout, pn, wgate, wproj, tile):
    n, d = x2.shape
    tok = lambda w: pl.BlockSpec((tile, w), lambda i: (i, 0))
    return pl.pallas_call(
        _out_ple_kernel,
        grid=(n // tile,),
        in_specs=[tok(d), tok(y.shape[1]), tok(p2.shape[1]), _const_spec(wout.shape), _const_spec(pn.shape),
                  _const_spec(wgate.shape), _const_spec(wproj.shape)],
        out_specs=tok(d),
        out_shape=jax.ShapeDtypeStruct((n, d), F32),
        compiler_params=_params("parallel"),
        name="out_ple",
    )(x2, y, p2, wout, pn, wgate, wproj)


def _t5_bucket(rel):
    max_exact = N_BUCKETS // 2
    rel = jnp.maximum(rel, 0)
    rel_f = jnp.maximum(rel, 1).astype(F32)
    log_ratio = jnp.log(rel_f / max_exact) / math.log(MAX_DISTANCE / max_exact)
    large = max_exact + (log_ratio * (N_BUCKETS - max_exact)).astype(I32)
    large = jnp.minimum(large, N_BUCKETS - 1)
    return jnp.where(rel < max_exact, rel, large)


def _bias_tiles(rel_bias):
    dist = jnp.arange(-(K_TILE - 1), BIAS_CLASSES * Q_TILE + 1, dtype=I32)
    table = (rel_bias.astype(F32) * LOG2_E)[_t5_bucket(dist)].T
    span = Q_TILE + K_TILE
    tiles = []
    for cls in range(BIAS_CLASSES):
        w = table[:, cls * Q_TILE:cls * Q_TILE + span]
        skew = jnp.tile(w, (1, Q_TILE + 1))[:, :Q_TILE * (span + 1)].reshape(H_A, Q_TILE, span + 1)
        tiles.append(skew[:, :, :K_TILE][:, :, ::-1])
    return jnp.stack(tiles)


def kernel(x, p, norm_w, a_w_in, a_g_cq, a_w_uq, a_w_uk, a_g_q, a_g_kv, a_w_iq, a_w_uv, a_w_out, rel_bias, b_w_in, b_conv_w, b_a_log, b_dt_bias, b_g_o, b_w_out, ple_norm, ple_w_gate, ple_w_proj):
    batch, seq, d_model = x.shape
    depth = p.shape[0]
    n = batch * seq
    assert seq % K_TILE == 0 and seq % GDN_TILE == 0 and H_B % GDN_HEADS == 0
    x2 = x.reshape(n, d_model)
    p2 = p.reshape(depth, n, p.shape[-1])
    bt = _bias_tiles(rel_bias)
    row = lambda a: a.reshape(1, -1).astype(F32)
    n_lat = D_CQ + D_C + D_I
    b_qkv = H_B * (2 * D_K + D_VB)
    for i in range(depth):
        j = i // 2
        if i % 2 == 0:
            w_in = a_w_in[j]
            wlat = w_in[:, :n_lat].astype(BF16)
            wwt = w_in[:, n_lat:n_lat + H_I].T.astype(BF16)
            wz = w_in[:, n_lat + H_I:].astype(BF16)
            qlat, qidx, wt, ckv, kidx, z = _a_proj(
                x2, row(norm_w[i]), wlat, wwt, wz, row(a_g_cq[j]), row(a_g_kv[j]),
                a_w_uq[j].reshape(D_CQ, H_A * D_NOPE).astype(BF16), a_w_uk[j].astype(BF16), row(a_g_q[j]),
                a_w_iq[j].reshape(D_CQ, H_I * D_I).astype(BF16), tile=512)
            y = _a_attn(qidx, wt, kidx, qlat, ckv, z, bt, a_w_uv[j].astype(BF16), batch, seq)
            w_out = a_w_out[j]
        else:
            w_in = b_w_in[j]
            wqkv = w_in[:, :b_qkv].astype(BF16)
            w_ba = w_in[:, b_qkv:b_qkv + 2 * H_B]
            wba = jnp.pad(w_ba, ((0, 0), (0, LANES - 2 * H_B))).astype(BF16)
            wz = w_in[:, b_qkv + 2 * H_B:].astype(BF16)
            q, k, v, beta, g, gt, z = _b_proj(
                x2, row(norm_w[i]), wqkv, wba, w_ba.T.astype(BF16), wz, b_conv_w[j].astype(F32),
                b_a_log[j].astype(F32), b_dt_bias[j].astype(F32), batch, seq, tile=256)
            y = _b_gdn(q, k, v, z, beta, g, gt, row(b_g_o[j]), batch, seq)
            w_out = b_w_out[j]
        x2 = _out_ple(x2, y, p2[i], w_out.astype(BF16), row(ple_norm[i]), ple_w_gate[i].astype(BF16),
                      ple_w_proj[i].astype(BF16), tile=512)
    return x2.reshape(batch, seq, d_model)
```

```python
import functools
import math

import jax
import jax.numpy as jnp
from jax import lax
from jax.experimental import pallas as pl
from jax.experimental.pallas import tpu as pltpu

F32 = jnp.float32
BF16 = jnp.bfloat16
I32 = jnp.int32
I16 = jnp.int16

EPS = 1e-6
NEG_INF = -1e30
LOG2_E = math.log2(math.e)
INT_MIN = -(2 ** 31)

LANES = 128
VMEM_LIMIT_BYTES = 56 * 1024 * 1024

H_A = 8
D_C = 256
D_CQ = 256
D_I = 128
H_I = 8
D_NOPE = 128
D_V = 128
TOPK_MAX = 256
N_BUCKETS = 32
MAX_DISTANCE = 128
Q_TILE = 256
K_TILE = 256
BIAS_CLASSES = -(-(MAX_DISTANCE + K_TILE - 1) // Q_TILE) + 1
H_B = 8
D_K = 128
D_VB = 128
CONV_W = 4
CHUNK = 64
GDN_TILE = 256
GDN_HEADS = 8


def _dot(a, b):
    return jnp.dot(a, b, preferred_element_type=F32)


def _dot_nt(a, b):
    return lax.dot_general(a, b, (((1,), (1,)), ((), ())), preferred_element_type=F32)


def _dot_tn(a, b):
    return lax.dot_general(a, b, (((0,), (0,)), ((), ())), preferred_element_type=F32)


def _rms(x, gain=None):
    y = x * lax.rsqrt(jnp.mean(x * x, axis=-1, keepdims=True) + EPS)
    return y if gain is None else y * gain


def _sigmoid(x):
    return 1.0 / (1.0 + jnp.exp(-x))


def _softplus(x):
    return jnp.maximum(x, 0.0) + jnp.log1p(jnp.exp(-jnp.abs(x)))


def _params(*semantics):
    return pltpu.CompilerParams(dimension_semantics=semantics, vmem_limit_bytes=VMEM_LIMIT_BYTES)


def _const_spec(shape):
    nd = len(shape)
    return pl.BlockSpec(shape, lambda *_: (0,) * nd)


def _a_proj_kernel(x_ref, nw_ref, wlat_ref, wwt_ref, wz_ref, gcq_ref, gkv_ref, wuq_ref, wuk_ref, gq_ref, wiq_ref,
                   qlat_ref, qidx_ref, wt_ref, ckv_ref, kidx_ref, z_ref):
    h = _rms(x_ref[...], nw_ref[...]).astype(BF16)
    lat = _dot(h, wlat_ref[...])
    cq = _rms(lat[:, :D_CQ], gcq_ref[...]).astype(BF16)
    ckv_ref[...] = _rms(lat[:, D_CQ:D_CQ + D_C], gkv_ref[...]).astype(BF16)
    kidx_ref[...] = _rms(lat[:, D_CQ + D_C:]).astype(BF16)
    wt_ref[...] = _dot_nt(wwt_ref[...], h) * (H_I ** -0.5)
    z_ref[...] = _dot(h, wz_ref[...])
    qn = _dot(cq, wuq_ref[...]).astype(BF16)
    for hh in range(H_A):
        ql = _dot(qn[:, hh * D_NOPE:(hh + 1) * D_NOPE], wuk_ref[hh])
        qlat_ref[hh] = (_rms(ql, gq_ref[...]) * (D_C ** -0.5 * LOG2_E)).astype(BF16)
    qi = _dot(cq, wiq_ref[...]) * (D_I ** -0.5)
    for hh in range(H_I):
        qidx_ref[hh] = qi[:, hh * D_I:(hh + 1) * D_I].astype(BF16)


def _a_proj(x2, nw, wlat, wwt, wz, gcq, gkv, wuq, wuk, gq, wiq, tile):
    n, d = x2.shape
    grid = (n // tile,)
    tok = lambda w: pl.BlockSpec((tile, w), lambda i: (i, 0))
    return pl.pallas_call(
        _a_proj_kernel,
        grid=grid,
        in_specs=[tok(d), _const_spec(nw.shape), _const_spec(wlat.shape), _const_spec(wwt.shape),
                  _const_spec(wz.shape), _const_spec(gcq.shape), _const_spec(gkv.shape), _const_spec(wuq.shape),
                  _const_spec(wuk.shape), _const_spec(gq.shape), _const_spec(wiq.shape)],
        out_specs=[pl.BlockSpec((H_A, tile, D_C), lambda i: (0, i, 0)),
                   pl.BlockSpec((H_I, tile, D_I), lambda i: (0, i, 0)),
                   pl.BlockSpec((H_I, tile), lambda i: (0, i)),
                   tok(D_C), tok(D_I), tok(H_A * D_V)],
        out_shape=[jax.ShapeDtypeStruct((H_A, n, D_C), BF16),
                   jax.ShapeDtypeStruct((H_I, n, D_I), BF16),
                   jax.ShapeDtypeStruct((H_I, n), F32),
                   jax.ShapeDtypeStruct((n, D_C), BF16),
                   jax.ShapeDtypeStruct((n, D_I), BF16),
                   jax.ShapeDtypeStruct((n, H_A * D_V), F32)],
        compiler_params=_params("parallel"),
        name="a_proj",
    )(x2, nw, wlat, wwt, wz, gcq, gkv, wuq, wuk, gq, wiq)


def _a_attn_kernel(qidx_ref, wt_ref, kidx_ref, qlat_ref, ckv_ref, z_ref, bt_ref, wuv_ref,
                   y_ref, keys_ref, hi_ref, lo_ref, lg_ref, mx_ref, ss_ref, oacc_ref, *, topk, n_q):
    qb = pl.program_id(1)
    n_kt = qb + 1
    rows = H_A * Q_TILE
    half = K_TILE // 2
    qidx = qidx_ref[...].reshape(H_I * Q_TILE, D_I)
    qlat = qlat_ref[...].reshape(rows, D_C)

    def for_tile_groups(body):
        def pair(i, carry):
            body(2 * i, 2)
            return carry

        lax.fori_loop(0, lax.shift_right_logical(n_kt, 1), pair, 0)

        @pl.when((n_kt & 1) == 1)
        def _():
            body(n_kt - 1, 1)

    def idx_tiles(c0, cnt):
        k0 = pl.multiple_of(c0 * K_TILE, K_TILE)
        lg = _dot_nt(kidx_ref[pl.ds(k0, cnt * K_TILE), :], qidx)
        sc = jnp.zeros((cnt * K_TILE, Q_TILE), F32)
        for hh in range(H_I):
            sc = sc + jnp.maximum(lg[:, hh * Q_TILE:(hh + 1) * Q_TILE], 0.0) * wt_ref[hh:hh + 1, :]
        bits = lax.bitcast_convert_type(sc, I32)
        sign = lax.shift_right_arithmetic(bits, 31)
        skey = (bits ^ (sign & 0x7FFFFFFF)) - sign
        key_pos = k0 + lax.broadcasted_iota(I32, (cnt * K_TILE, Q_TILE), 0)
        q_pos = qb * Q_TILE + lax.broadcasted_iota(I32, (cnt * K_TILE, Q_TILE), 1)
        skey = jnp.where(key_pos <= q_pos, skey, INT_MIN)
        hi = lax.shift_right_arithmetic(skey, 16).astype(I16)
        lo = ((skey & 0xFFFF) - 32768).astype(I16)
        for u in range(cnt):
            ks = slice(u * K_TILE, (u + 1) * K_TILE)
            keys_ref[c0 + u] = skey[ks]
            hi_ref[c0 + u] = hi[ks].reshape(K_TILE // 16, 16, Q_TILE)
            lo_ref[c0 + u] = lo[ks].reshape(K_TILE // 16, 16, Q_TILE)

    for_tile_groups(idx_tiles)

    one = jnp.ones((), BF16)
    zero = jnp.zeros((), BF16)

    def threshold(n_tiles):
        def count(ref16, bound16, strict):
            acc = jnp.zeros((16, Q_TILE), BF16)
            for c in range(n_tiles):
                tile = ref16[c]
                hit = jnp.where(tile > bound16[None] if strict else tile >= bound16[None], one, zero)
                for u in range(K_TILE // 16):
                    acc = acc + hit[u]
            return jnp.sum(acc.astype(F32), axis=0, keepdims=True)

        def kth_largest(ref16, need):
            def bit_body(i, thr):
                cand = thr + lax.shift_left(jnp.int32(1), 15 - i)
                return jnp.where(count(ref16, cand.astype(I16), False) >= need, cand, thr)

            return lax.fori_loop(0, 16, bit_body, jnp.full((16, Q_TILE), -32768, I32))

        hi_thr = kth_largest(hi_ref, float(topk))
        hi_thr16 = hi_thr.astype(I16)
        need_lo = float(topk) - count(hi_ref, hi_thr16, True)
        for c in range(n_tiles):
            lo_ref[c] = jnp.where(hi_ref[c] == hi_thr16[None], lo_ref[c], jnp.int16(-32768))
        lo_thr = kth_largest(lo_ref, need_lo)
        thr = hi_thr[0:1] * 65536 + (lo_thr[0:1] + 32768)
        thr = jnp.maximum(thr, INT_MIN + 1)
        n_sel = jnp.zeros((8, Q_TILE), F32)
        for c in range(n_tiles):
            hit = jnp.where(keys_ref[c] >= thr, 1.0, 0.0)
            n_sel = n_sel + jnp.sum(hit.reshape(K_TILE // 8, 8, Q_TILE), axis=0)
        return thr, jnp.sum(n_sel, axis=0, keepdims=True) - float(topk)

    thr, excess = lax.switch(qb, [functools.partial(threshold, n) for n in range(1, n_q + 1)])

    @pl.when(jnp.max(excess) > 0.0)
    def _():
        kr = lax.broadcasted_iota(I32, (K_TILE, K_TILE), 0)
        kc = lax.broadcasted_iota(I32, (K_TILE, K_TILE), 1)
        later = jnp.where(kc >= kr, 1.0, 0.0).astype(BF16)

        def tie_body(i, seen):
            c = n_kt - 1 - i
            keys = keys_ref[c]
            tie = keys == thr
            rank = _dot(later, jnp.where(tie, 1.0, 0.0).astype(BF16)) + seen
            keys_ref[c] = jnp.where(tie & (rank <= excess), thr - 1, keys)
            return rank[0:1]

        lax.fori_loop(0, n_kt, tie_body, jnp.zeros((1, Q_TILE), F32))

    mx_ref[...] = jnp.full(mx_ref.shape, NEG_INF, F32)

    def qk_tiles(c0, cnt):
        k0 = pl.multiple_of(c0 * K_TILE, K_TILE)
        lg = _dot_nt(qlat, ckv_ref[pl.ds(k0, cnt * K_TILE), :])
        for u in range(cnt):
            c = c0 + u
            mb = jnp.where(keys_ref[c] >= thr, 0.0, NEG_INF).T
            t_idx = jnp.minimum(qb - c, BIAS_CLASSES - 1)
            for hh in range(H_A):
                l = lg[hh * Q_TILE:(hh + 1) * Q_TILE, u * K_TILE:(u + 1) * K_TILE] + (mb + bt_ref[t_idx, hh])
                lg_ref[c, hh] = l
                mx_ref[hh] = jnp.maximum(mx_ref[hh], jnp.maximum(l[:, :half], l[:, half:]))

    for_tile_groups(qk_tiles)

    m = jnp.max(mx_ref[...], axis=-1, keepdims=True)
    ss_ref[...] = jnp.zeros(ss_ref.shape, F32)
    oacc_ref[...] = jnp.zeros(oacc_ref.shape, F32)

    def pv_tiles(c0, cnt):
        k0 = pl.multiple_of(c0 * K_TILE, K_TILE)
        ps = [jnp.exp2(lg_ref[c0 + u] - m) for u in range(cnt)]
        part = ps[0][:, :, :half] + ps[0][:, :, half:]
        for p in ps[1:]:
            part = part + (p[:, :, :half] + p[:, :, half:])
        ss_ref[...] += part
        p16 = jnp.concatenate([p.reshape(rows, K_TILE).astype(BF16) for p in ps], axis=1)
        oacc_ref[...] += _dot(p16, ckv_ref[pl.ds(k0, cnt * K_TILE), :])

    for_tile_groups(pv_tiles)
    denom = jnp.sum(ss_ref[...], axis=-1, keepdims=True)
    o_lat = oacc_ref[...].reshape(H_A, Q_TILE, D_C) / denom

    for hh in range(H_A):
        oh = _dot(o_lat[hh].astype(BF16), wuv_ref[hh])
        zz = z_ref[:, hh * D_V:(hh + 1) * D_V]
        y_ref[:, hh * D_V:(hh + 1) * D_V] = (oh * (zz * _sigmoid(zz))).astype(BF16)


def _a_attn(qidx, wt, kidx, qlat, ckv, z, bt, wuv, batch, seq):
    n = batch * seq
    nq = seq // Q_TILE
    n_kt = seq // K_TILE
    topk = min(TOPK_MAX, seq // 4)
    grid = (batch, nq)
    kern = functools.partial(_a_attn_kernel, topk=topk, n_q=nq)
    return pl.pallas_call(
        kern,
        grid=grid,
        in_specs=[pl.BlockSpec((H_I, Q_TILE, D_I), lambda b, q: (0, b * nq + q, 0)),
                  pl.BlockSpec((H_I, Q_TILE), lambda b, q: (0, b * nq + q)),
                  pl.BlockSpec((seq, D_I), lambda b, q: (b, 0)),
                  pl.BlockSpec((H_A, Q_TILE, D_C), lambda b, q: (0, b * nq + q, 0)),
                  pl.BlockSpec((seq, D_C), lambda b, q: (b, 0)),
                  pl.BlockSpec((Q_TILE, H_A * D_V), lambda b, q: (b * nq + q, 0)),
                  pl.BlockSpec(bt.shape, lambda b, q: (0, 0, 0, 0), pipeline_mode=pl.Buffered(1)),
                  pl.BlockSpec(wuv.shape, lambda b, q: (0, 0, 0), pipeline_mode=pl.Buffered(1))],
        out_specs=pl.BlockSpec((Q_TILE, H_A * D_V), lambda b, q: (b * nq + q, 0)),
        out_shape=jax.ShapeDtypeStruct((n, H_A * D_V), BF16),
        scratch_shapes=[pltpu.VMEM((n_kt, K_TILE, Q_TILE), I32),
                        pltpu.VMEM((n_kt, K_TILE // 16, 16, Q_TILE), I16),
                        pltpu.VMEM((n_kt, K_TILE // 16, 16, Q_TILE), I16),
                        pltpu.VMEM((n_kt, H_A, Q_TILE, K_TILE), F32),
                        pltpu.VMEM((H_A, Q_TILE, K_TILE // 2), F32),
                        pltpu.VMEM((H_A, Q_TILE, K_TILE // 2), F32),
                        pltpu.VMEM((H_A * Q_TILE, D_C), F32)],
        compiler_params=_params("parallel", "arbitrary"),
        name="a_attn",
    )(qidx, wt, kidx, qlat, ckv, z, bt, wuv)


def _b_proj_kernel(x_ref, nw_ref, wqkv_ref, wba_ref, wbat_ref, wz_ref, cw_ref, alog_ref, dtb_ref, alogc_ref, dtbc_ref,
                   q_ref, k_ref, v_ref, beta_ref, g_ref, gt_ref, z_ref, buf_ref, *, tile):
    s = pl.program_id(1)
    h = _rms(x_ref[...], nw_ref[...]).astype(BF16)
    z_ref[...] = _dot(h, wz_ref[...])
    ba = _dot(h, wba_ref[...])
    beta_ref[...] = _sigmoid(ba[:, :H_B])
    g_ref[...] = -jnp.exp(alog_ref[...]) * _softplus(ba[:, H_B:2 * H_B] + dtb_ref[...])
    bat = _dot_nt(wbat_ref[...], h)
    gt_ref[...] = -jnp.exp(alogc_ref[...]) * _softplus(bat[H_B:, :] + dtbc_ref[...])

    @pl.when(s == 0)
    def _():
        buf_ref[0:8, :] = jnp.zeros((8, buf_ref.shape[1]), F32)

    width = H_B * D_K
    for sec, out_ref in enumerate((q_ref, k_ref, v_ref)):
        cols = slice(sec * width, (sec + 1) * width)
        pre = _dot(h, wqkv_ref[:, cols])
        buf_ref[8:8 + tile, cols] = pre
        acc = pre * cw_ref[CONV_W - 1:CONV_W, cols]
        for w in range(CONV_W - 1):
            acc = acc + buf_ref[8 - (CONV_W - 1) + w:8 - (CONV_W - 1) + w + tile, cols] * cw_ref[w:w + 1, cols]
        buf_ref[0:8, cols] = buf_ref[tile:tile + 8, cols]
        y = acc * _sigmoid(acc)
        if sec == 2:
            out_ref[...] = y
        else:
            scale = (D_K ** -0.5) if sec == 0 else 1.0
            for hh in range(H_B):
                yy = y[:, hh * D_K:(hh + 1) * D_K]
                nrm = lax.rsqrt(jnp.sum(yy * yy, axis=-1, keepdims=True) + EPS)
                out_ref[:, hh * D_K:(hh + 1) * D_K] = yy * (nrm * scale)


def _b_proj(x2, nw, wqkv, wba, wbat, wz, cw, alog, dtb, batch, seq, tile):
    n, d = x2.shape
    ns = seq // tile
    grid = (batch, ns)
    tok = lambda w: pl.BlockSpec((tile, w), lambda b, s: (b * ns + s, 0))
    width = H_B * D_K
    kern = functools.partial(_b_proj_kernel, tile=tile)
    return pl.pallas_call(
        kern,
        grid=grid,
        in_specs=[tok(d), _const_spec(nw.shape), _const_spec(wqkv.shape), _const_spec(wba.shape),
                  _const_spec(wbat.shape), _const_spec(wz.shape), _const_spec(cw.shape),
                  _const_spec((1, H_B)), _const_spec((1, H_B)), _const_spec((H_B, 1)), _const_spec((H_B, 1))],
        out_specs=[tok(width), tok(width), tok(width), tok(H_B), tok(H_B),
                   pl.BlockSpec((H_B, tile), lambda b, s: (0, b * ns + s)), tok(width)],
        out_shape=[jax.ShapeDtypeStruct((n, width), F32)] * 3
        + [jax.ShapeDtypeStruct((n, H_B), F32)] * 2
        + [jax.ShapeDtypeStruct((H_B, n), F32), jax.ShapeDtypeStruct((n, width), F32)],
        scratch_shapes=[pltpu.VMEM((tile + 8, 3 * width), F32)],
        compiler_params=_params("parallel", "arbitrary"),
        name="b_proj",
    )(x2, nw, wqkv, wba, wbat, wz, cw, alog.reshape(1, H_B), dtb.reshape(1, H_B),
      alog.reshape(H_B, 1), dtb.reshape(H_B, 1))


def _b_gdn_kernel(q_ref, k_ref, v_ref, z_ref, beta_ref, g_ref, gt_ref, go_ref, y_ref, state_ref):
    hb = pl.program_id(1)
    s = pl.program_id(2)
    t = GDN_TILE
    n_chunks = t // CHUNK
    log_c = int(math.log2(CHUNK))
    heads = range(GDN_HEADS)

    @pl.when(s == 0)
    def _():
        state_ref[...] = jnp.zeros(state_ref.shape, F32)

    r = lax.broadcasted_iota(I32, (t, t), 0)
    c = lax.broadcasted_iota(I32, (t, t), 1)
    xs = r ^ c
    same = xs < CHUNK
    incl = same & (r >= c)
    upper = same & (r <= c)
    eye = (r == c).astype(F32)
    level = [(lax.shift_right_logical(xs, lb) == 1) & ((r & (1 << lb)) != 0) for lb in range(log_c)]
    lane8 = lax.broadcasted_iota(I32, (t, H_B), 1)
    sub8 = lax.broadcasted_iota(I32, (H_B, t), 0)

    b_col, decay, eg_col, ekl_col, egl_col = [], [], [], [], []
    for j in heads:
        head = hb * GDN_HEADS + j
        g_col = jnp.sum(jnp.where(lane8 == head, g_ref[...], 0.0), axis=1, keepdims=True)
        b_col.append(jnp.sum(jnp.where(lane8 == head, beta_ref[...], 0.0), axis=1, keepdims=True))
        g_row = jnp.sum(jnp.where(sub8 == head, gt_ref[...], 0.0), axis=0, keepdims=True)
        g_rows = jnp.broadcast_to(g_row, (t, t))
        gc_col = jnp.sum(jnp.where(incl, g_rows, 0.0), axis=1, keepdims=True)
        gl_col = jnp.sum(jnp.where(same, g_rows, 0.0), axis=1, keepdims=True)
        gc_row = jnp.sum(jnp.where(upper, jnp.broadcast_to(g_col, (t, t)), 0.0), axis=0, keepdims=True)
        decay.append(jnp.where(incl, jnp.exp(jnp.minimum(gc_col - gc_row, 0.0)), 0.0))
        eg_col.append(jnp.exp(gc_col))
        ekl_col.append(jnp.exp(gl_col - gc_col))
        egl_col.append(jnp.exp(gl_col))

    lmat, aqk, rhs, q_dec, k_dec = [], [], [], [], []
    for j in heads:
        hs = slice(j * D_K, (j + 1) * D_K)
        qf, kf, vf = q_ref[:, hs], k_ref[:, hs], v_ref[:, hs]
        kb = kf * b_col[j]
        k16 = kf.astype(BF16)
        gram = _dot_nt(jnp.concatenate([kb.astype(BF16), qf.astype(BF16)], axis=0), k16)
        lmat.append(gram[:t] * decay[j])
        aqk.append(jnp.where(incl, gram[t:] * decay[j], 0.0).astype(BF16))
        rhs.append(jnp.concatenate([(vf * b_col[j]).astype(BF16), (kb * eg_col[j]).astype(BF16)], axis=1))
        q_dec.append(qf * eg_col[j])
        k_dec.append((kf * ekl_col[j]).astype(BF16))

    tinv = [eye - jnp.where(level[0], lmat[j], 0.0) for j in heads]
    for lb in range(1, log_c):
        t16 = [tinv[j].astype(BF16) for j in heads]
        y16 = [_dot(jnp.where(level[lb], lmat[j], 0.0).astype(BF16), t16[j]).astype(BF16) for j in heads]
        tinv = [tinv[j] - _dot(t16[j], y16[j]) for j in heads]

    sol16 = [_dot(tinv[j].astype(BF16), rhs[j]).astype(BF16) for j in heads]
    aux = [_dot(aqk[j], sol16[j]) for j in heads]
    q_eff = [(q_dec[j] - aux[j][:, D_VB:]).astype(BF16) for j in heads]
    kw = [[_dot_tn(k_dec[j][ci * CHUNK:(ci + 1) * CHUNK], sol16[j][ci * CHUNK:(ci + 1) * CHUNK])
           for ci in range(n_chunks)] for j in heads]

    state = [state_ref[j] for j in heads]
    outs = [[] for _ in heads]
    for ci in range(n_chunks):
        rs = slice(ci * CHUNK, (ci + 1) * CHUNK)
        for j in heads:
            s16 = state[j].astype(BF16)
            both = _dot(jnp.concatenate([q_eff[j][rs], kw[j][ci][:, D_VB:].astype(BF16)], axis=0), s16)
            outs[j].append(both[:CHUNK] + aux[j][rs, :D_VB])
            state[j] = state[j] * egl_col[j][ci * CHUNK:ci * CHUNK + 1, :] + kw[j][ci][:, :D_VB] - both[CHUNK:]

    for j in heads:
        hs = slice(j * D_K, (j + 1) * D_K)
        state_ref[j] = state[j]
        o = jnp.concatenate(outs[j], axis=0)
        zz = z_ref[:, hs]
        y_ref[:, hs] = (_rms(o, go_ref[...]) * (zz * _sigmoid(zz))).astype(BF16)


def _b_gdn(q, k, v, z, beta, g, gt, go, batch, seq):
    n = batch * seq
    ns = seq // GDN_TILE
    nh = H_B // GDN_HEADS
    grid = (batch, nh, ns)
    wide = pl.BlockSpec((GDN_TILE, GDN_HEADS * D_K), lambda b, h, s: (b * ns + s, h))
    narrow = pl.BlockSpec((GDN_TILE, H_B), lambda b, h, s: (b * ns + s, 0))
    return pl.pallas_call(
        _b_gdn_kernel,
        grid=grid,
        in_specs=[wide, wide, wide, wide, narrow, narrow,
                  pl.BlockSpec((H_B, GDN_TILE), lambda b, h, s: (0, b * ns + s)),
                  _const_spec(go.shape)],
        out_specs=wide,
        out_shape=jax.ShapeDtypeStruct((n, H_B * D_VB), BF16),
        scratch_shapes=[pltpu.VMEM((GDN_HEADS, D_K, D_VB), F32)],
        compiler_params=_params("parallel", "parallel", "arbitrary"),
        name="b_gdn",
    )(q, k, v, z, beta, g, gt, go)


def _out_ple_kernel(x_ref, y_ref, p_ref, wout_ref, pn_ref, wgate_ref, wproj_ref, o_ref):
    x1 = x_ref[...] + _dot(y_ref[...], wout_ref[...])
    hn = _rms(x1, pn_ref[...]).astype(BF16)
    gate = _sigmoid(_dot(hn, wgate_ref[...]))
    o_ref[...] = x1 + gate * _dot(p_ref[...].astype(BF16), wproj_ref[...])


def _out_ple(x2, y, p2, wout, pn, wgate, wproj, tile):
    n, d = x2.shape
    tok = lambda w: pl.BlockSpec((tile, w), lambda i: (i, 0))
    return pl.pallas_call(
        _out_ple_kernel,
        grid=(n // tile,),
        in_specs=[tok(d), tok(y.shape[1]), tok(p2.shape[1]), _const_spec(wout.shape), _const_spec(pn.shape),
                  _const_spec(wgate.shape), _const_spec(wproj.shape)],
        out_specs=tok(d),
        out_shape=jax.ShapeDtypeStruct((n, d), F32),
        compiler_params=_params("parallel"),
        name="out_ple",
    )(x2, y, p2, wout, pn, wgate, wproj)


def _t5_bucket(rel):
    max_exact = N_BUCKETS // 2
    rel = jnp.maximum(rel, 0)
    rel_f = jnp.maximum(rel, 1).astype(F32)
    log_ratio = jnp.log(rel_f / max_exact) / math.log(MAX_DISTANCE / max_exact)
    large = max_exact + (log_ratio * (N_BUCKETS - max_exact)).astype(I32)
    large = jnp.minimum(large, N_BUCKETS - 1)
    return jnp.where(rel < max_exact, rel, large)


def _bias_tiles(rel_bias):
    dist = jnp.arange(-(K_TILE - 1), BIAS_CLASSES * Q_TILE + 1, dtype=I32)
    table = (rel_bias.astype(F32) * LOG2_E)[_t5_bucket(dist)].T
    span = Q_TILE + K_TILE
    tiles = []
    for cls in range(BIAS_CLASSES):
        w = table[:, cls * Q_TILE:cls * Q_TILE + span]
        skew = jnp.tile(w, (1, Q_TILE + 1))[:, :Q_TILE * (span + 1)].reshape(H_A, Q_TILE, span + 1)
        tiles.append(skew[:, :, :K_TILE][:, :, ::-1])
    return jnp.stack(tiles)


def kernel(x, p, norm_w, a_w_in, a_g_cq, a_w_uq, a_w_uk, a_g_q, a_g_kv, a_w_iq, a_w_uv, a_w_out, rel_bias, b_w_in, b_conv_w, b_a_log, b_dt_bias, b_g_o, b_w_out, ple_norm, ple_w_gate, ple_w_proj):
    batch, seq, d_model = x.shape
    depth = p.shape[0]
    n = batch * seq
    assert seq % K_TILE == 0 and seq % GDN_TILE == 0 and H_B % GDN_HEADS == 0
    x2 = x.reshape(n, d_model)
    p2 = p.reshape(depth, n, p.shape[-1])
    bt = _bias_tiles(rel_bias)
    row = lambda a: a.reshape(1, -1).astype(F32)
    n_lat = D_CQ + D_C + D_I
    b_qkv = H_B * (2 * D_K + D_VB)
    for i in range(depth):
        j = i // 2
        if i % 2 == 0:
            w_in = a_w_in[j]
            wlat = w_in[:, :n_lat].astype(BF16)
            wwt = w_in[:, n_lat:n_lat + H_I].T.astype(BF16)
            wz = w_in[:, n_lat + H_I:].astype(BF16)
            qlat, qidx, wt, ckv, kidx, z = _a_proj(
                x2, row(norm_w[i]), wlat, wwt, wz, row(a_g_cq[j]), row(a_g_kv[j]),
                a_w_uq[j].reshape(D_CQ, H_A * D_NOPE).astype(BF16), a_w_uk[j].astype(BF16), row(a_g_q[j]),
                a_w_iq[j].reshape(D_CQ, H_I * D_I).astype(BF16), tile=512)
            y = _a_attn(qidx, wt, kidx, qlat, ckv, z, bt, a_w_uv[j].astype(BF16), batch, seq)
            w_out = a_w_out[j]
        else:
            w_in = b_w_in[j]
            wqkv = w_in[:, :b_qkv].astype(BF16)
            w_ba = w_in[:, b_qkv:b_qkv + 2 * H_B]
            wba = jnp.pad(w_ba, ((0, 0), (0, LANES - 2 * H_B))).astype(BF16)
            wz = w_in[:, b_qkv + 2 * H_B:].astype(BF16)
            q, k, v, beta, g, gt, z = _b_proj(
                x2, row(norm_w[i]), wqkv, wba, w_ba.T.astype(BF16), wz, b_conv_w[j].astype(F32),
                b_a_log[j].astype(F32), b_dt_bias[j].astype(F32), batch, seq, tile=256)
            y = _b_gdn(q, k, v, z, beta, g, gt, row(b_g_o[j]), batch, seq)
            w_out = b_w_out[j]
        x2 = _out_ple(x2, y, p2[i], w_out.astype(BF16), row(ple_norm[i]), ple_w_gate[i].astype(BF16),
                      ple_w_proj[i].astype(BF16), tile=512)
    return x2.reshape(batch, seq, d_model)
```

```python
import functools
import math

import jax
import jax.numpy as jnp
from jax import lax
from jax.experimental import pallas as pl
from jax.experimental.pallas import tpu as pltpu

F32 = jnp.float32
BF16 = jnp.bfloat16
I32 = jnp.int32
I16 = jnp.int16

EPS = 1e-6
NEG_INF = -1e30
LOG2_E = math.log2(math.e)
INT_MIN = -(2 ** 31)

LANES = 128
VMEM_LIMIT_BYTES = 56 * 1024 * 1024

H_A = 8
D_C = 256
D_CQ = 256
D_I = 128
H_I = 8
D_NOPE = 128
D_V = 128
TOPK_MAX = 256
N_BUCKETS = 32
MAX_DISTANCE = 128
Q_TILE = 256
K_TILE = 256
BIAS_CLASSES = -(-(MAX_DISTANCE + K_TILE - 1) // Q_TILE) + 1
H_B = 8
D_K = 128
D_VB = 128
CONV_W = 4
CHUNK = 64
GDN_TILE = 256
GDN_HEADS = 8


def _dot(a, b):
    return jnp.dot(a, b, preferred_element_type=F32)


def _dot_nt(a, b):
    return lax.dot_general(a, b, (((1,), (1,)), ((), ())), preferred_element_type=F32)


def _dot_tn(a, b):
    return lax.dot_general(a, b, (((0,), (0,)), ((), ())), preferred_element_type=F32)


def _rms(x, gain=None):
    y = x * lax.rsqrt(jnp.mean(x * x, axis=-1, keepdims=True) + EPS)
    return y if gain is None else y * gain


def _sigmoid(x):
    return 1.0 / (1.0 + jnp.exp(-x))


def _softplus(x):
    return jnp.maximum(x, 0.0) + jnp.log1p(jnp.exp(-jnp.abs(x)))


def _params(*semantics):
    return pltpu.CompilerParams(dimension_semantics=semantics, vmem_limit_bytes=VMEM_LIMIT_BYTES)


def _const_spec(shape):
    nd = len(shape)
    return pl.BlockSpec(shape, lambda *_: (0,) * nd)


def _a_proj_kernel(x_ref, nw_ref, wlat_ref, wwt_ref, wz_ref, gcq_ref, gkv_ref, wuq_ref, wuk_ref, gq_ref, wiq_ref,
                   qlat_ref, qidx_ref, wt_ref, ckv_ref, kidx_ref, z_ref):
    h = _rms(x_ref[...], nw_ref[...]).astype(BF16)
    lat = _dot(h, wlat_ref[...])
    cq = _rms(lat[:, :D_CQ], gcq_ref[...]).astype(BF16)
    ckv_ref[...] = _rms(lat[:, D_CQ:D_CQ + D_C], gkv_ref[...]).astype(BF16)
    kidx_ref[...] = _rms(lat[:, D_CQ + D_C:]).astype(BF16)
    wt_ref[...] = _dot_nt(wwt_ref[...], h) * (H_I ** -0.5)
    z_ref[...] = _dot(h, wz_ref[...])
    qn = _dot(cq, wuq_ref[...]).astype(BF16)
    for hh in range(H_A):
        ql = _dot(qn[:, hh * D_NOPE:(hh + 1) * D_NOPE], wuk_ref[hh])
        qlat_ref[hh] = (_rms(ql, gq_ref[...]) * (D_C ** -0.5 * LOG2_E)).astype(BF16)
    qi = _dot(cq, wiq_ref[...]) * (D_I ** -0.5)
    for hh in range(H_I):
        qidx_ref[hh] = qi[:, hh * D_I:(hh + 1) * D_I].astype(BF16)


def _a_proj(x2, nw, wlat, wwt, wz, gcq, gkv, wuq, wuk, gq, wiq, tile):
    n, d = x2.shape
    grid = (n // tile,)
    tok = lambda w: pl.BlockSpec((tile, w), lambda i: (i, 0))
    return pl.pallas_call(
        _a_proj_kernel,
        grid=grid,
        in_specs=[tok(d), _const_spec(nw.shape), _const_spec(wlat.shape), _const_spec(wwt.shape),
                  _const_spec(wz.shape), _const_spec(gcq.shape), _const_spec(gkv.shape), _const_spec(wuq.shape),
                  _const_spec(wuk.shape), _const_spec(gq.shape), _const_spec(wiq.shape)],
        out_specs=[pl.BlockSpec((H_A, tile, D_C), lambda i: (0, i, 0)),
                   pl.BlockSpec((H_I, tile, D_I), lambda i: (0, i, 0)),
                   pl.BlockSpec((H_I, tile), lambda i: (0, i)),
                   tok(D_C), tok(D_I), tok(H_A * D_V)],
        out_shape=[jax.ShapeDtypeStruct((H_A, n, D_C), BF16),
                   jax.ShapeDtypeStruct((H_I, n, D_I), BF16),
                   jax.ShapeDtypeStruct((H_I, n), F32),
                   jax.ShapeDtypeStruct((n, D_C), BF16),
                   jax.ShapeDtypeStruct((n, D_I), BF16),
                   jax.ShapeDtypeStruct((n, H_A * D_V), F32)],
        compiler_params=_params("parallel"),
        name="a_proj",
    )(x2, nw, wlat, wwt, wz, gcq, gkv, wuq, wuk, gq, wiq)


def _a_attn_kernel(qidx_ref, wt_ref, kidx_ref, qlat_ref, ckv_ref, z_ref, bt_ref, wuv_ref,
                   y_ref, keys_ref, hi_ref, lo_ref, lg_ref, mx_ref, ss_ref, oacc_ref, *, topk, n_q):
    qb = pl.program_id(1)
    n_kt = qb + 1
    rows = H_A * Q_TILE
    half = K_TILE // 2
    qidx = qidx_ref[...].reshape(H_I * Q_TILE, D_I)
    qlat = qlat_ref[...].reshape(rows, D_C)

    def for_tile_groups(body):
        def pair(i, carry):
            body(2 * i, 2)
            return carry

        lax.fori_loop(0, lax.shift_right_logical(n_kt, 1), pair, 0)

        @pl.when((n_kt & 1) == 1)
        def _():
            body(n_kt - 1, 1)

    def idx_tiles(c0, cnt):
        k0 = pl.multiple_of(c0 * K_TILE, K_TILE)
        lg = _dot_nt(kidx_ref[pl.ds(k0, cnt * K_TILE), :], qidx)
        sc = jnp.zeros((cnt * K_TILE, Q_TILE), F32)
        for hh in range(H_I):
            sc = sc + jnp.maximum(lg[:, hh * Q_TILE:(hh + 1) * Q_TILE], 0.0) * wt_ref[hh:hh + 1, :]
        bits = lax.bitcast_convert_type(sc, I32)
        sign = lax.shift_right_arithmetic(bits, 31)
        skey = (bits ^ (sign & 0x7FFFFFFF)) - sign
        key_pos = k0 + lax.broadcasted_iota(I32, (cnt * K_TILE, Q_TILE), 0)
        q_pos = qb * Q_TILE + lax.broadcasted_iota(I32, (cnt * K_TILE, Q_TILE), 1)
        skey = jnp.where(key_pos <= q_pos, skey, INT_MIN)
        hi = lax.shift_right_arithmetic(skey, 16).astype(I16)
        lo = ((skey & 0xFFFF) - 32768).astype(I16)
        for u in range(cnt):
            ks = slice(u * K_TILE, (u + 1) * K_TILE)
            keys_ref[c0 + u] = skey[ks]
            hi_ref[c0 + u] = hi[ks].reshape(K_TILE // 16, 16, Q_TILE)
            lo_ref[c0 + u] = lo[ks].reshape(K_TILE // 16, 16, Q_TILE)

    for_tile_groups(idx_tiles)

    one = jnp.ones((), BF16)
    zero = jnp.zeros((), BF16)

    def threshold(n_tiles):
        def count(ref16, bound16, strict):
            acc = jnp.zeros((16, Q_TILE), BF16)
            for c in range(n_tiles):
                tile = ref16[c]
                hit = jnp.where(tile > bound16[None] if strict else tile >= bound16[None], one, zero)
                parts = [hit[u] for u in range(K_TILE // 16)]
                while len(parts) > 1:
                    parts = [parts[u] + parts[u + 1] for u in range(0, len(parts), 2)]
                acc = acc + parts[0]
            return jnp.sum(acc.astype(F32), axis=0, keepdims=True)

        def kth_largest(ref16, need, n_all):
            def bit_body(i, carry):
                thr, n_ge = carry
                cand = thr + lax.shift_left(jnp.int32(1), 15 - i)
                n_cand = count(ref16, cand.astype(I16), False)
                ok = n_cand >= need
                return jnp.where(ok, cand, thr), jnp.where(ok, n_cand, n_ge)

            return lax.fori_loop(0, 16, bit_body, (jnp.full((16, Q_TILE), -32768, I32), n_all))

        n_all = jnp.full((1, Q_TILE), float(n_tiles * K_TILE), F32)
        hi_thr, n_ge_hi = kth_largest(hi_ref, float(topk), n_all)
        hi_thr16 = hi_thr.astype(I16)
        n_gt_hi = count(hi_ref, hi_thr16, True)
        for c in range(n_tiles):
            lo_ref[c] = jnp.where(hi_ref[c] == hi_thr16[None], lo_ref[c], jnp.int16(-32768))
        lo_thr, n_ge_lo = kth_largest(lo_ref, float(topk) - n_gt_hi, n_ge_hi - n_gt_hi)
        thr = hi_thr[0:1] * 65536 + (lo_thr[0:1] + 32768)
        thr = jnp.maximum(thr, INT_MIN + 1)
        return thr, n_gt_hi + n_ge_lo - float(topk)

    thr, excess = lax.switch(qb, [functools.partial(threshold, n) for n in range(1, n_q + 1)])

    @pl.when(jnp.max(excess) > 0.0)
    def _():
        kr = lax.broadcasted_iota(I32, (K_TILE, K_TILE), 0)
        kc = lax.broadcasted_iota(I32, (K_TILE, K_TILE), 1)
        later = jnp.where(kc >= kr, 1.0, 0.0).astype(BF16)

        def tie_body(i, seen):
            c = n_kt - 1 - i
            keys = keys_ref[c]
            tie = keys == thr
            rank = _dot(later, jnp.where(tie, 1.0, 0.0).astype(BF16)) + seen
            keys_ref[c] = jnp.where(tie & (rank <= excess), thr - 1, keys)
            return rank[0:1]

        lax.fori_loop(0, n_kt, tie_body, jnp.zeros((1, Q_TILE), F32))

    mx_ref[...] = jnp.full(mx_ref.shape, NEG_INF, F32)

    def qk_tiles(c0, cnt):
        k0 = pl.multiple_of(c0 * K_TILE, K_TILE)
        lg = _dot_nt(qlat, ckv_ref[pl.ds(k0, cnt * K_TILE), :])
        for u in range(cnt):
            c = c0 + u
            mb = jnp.where(keys_ref[c] >= thr, 0.0, NEG_INF).T
            t_idx = jnp.minimum(qb - c, BIAS_CLASSES - 1)
            for hh in range(H_A):
                l = lg[hh * Q_TILE:(hh + 1) * Q_TILE, u * K_TILE:(u + 1) * K_TILE] + (mb + bt_ref[t_idx, hh])
                lg_ref[c, hh] = l
                mx_ref[hh] = jnp.maximum(mx_ref[hh], jnp.maximum(l[:, :half], l[:, half:]))

    for_tile_groups(qk_tiles)

    m = jnp.max(mx_ref[...], axis=-1, keepdims=True)
    ss_ref[...] = jnp.zeros(ss_ref.shape, F32)
    oacc_ref[...] = jnp.zeros(oacc_ref.shape, F32)

    def pv_tiles(c0, cnt):
        k0 = pl.multiple_of(c0 * K_TILE, K_TILE)
        ps = [jnp.exp2(lg_ref[c0 + u] - m) for u in range(cnt)]
        part = ps[0][:, :, :half] + ps[0][:, :, half:]
        for p in ps[1:]:
            part = part + (p[:, :, :half] + p[:, :, half:])
        ss_ref[...] += part
        p16 = jnp.concatenate([p.reshape(rows, K_TILE).astype(BF16) for p in ps], axis=1)
        oacc_ref[...] += _dot(p16, ckv_ref[pl.ds(k0, cnt * K_TILE), :])

    for_tile_groups(pv_tiles)
    inv_denom = 1.0 / jnp.sum(ss_ref[...], axis=-1, keepdims=True)

    for hh in range(H_A):
        oh = _dot(oacc_ref[hh * Q_TILE:(hh + 1) * Q_TILE, :].astype(BF16), wuv_ref[hh])
        zz = z_ref[:, hh * D_V:(hh + 1) * D_V]
        y_ref[:, hh * D_V:(hh + 1) * D_V] = (oh * inv_denom[hh] * (zz * _sigmoid(zz))).astype(BF16)


def _a_attn(qidx, wt, kidx, qlat, ckv, z, bt, wuv, batch, seq):
    n = batch * seq
    nq = seq // Q_TILE
    n_kt = seq // K_TILE
    topk = min(TOPK_MAX, seq // 4)
    grid = (batch, nq)
    kern = functools.partial(_a_attn_kernel, topk=topk, n_q=nq)
    return pl.pallas_call(
        kern,
        grid=grid,
        in_specs=[pl.BlockSpec((H_I, Q_TILE, D_I), lambda b, q: (0, b * nq + q, 0)),
                  pl.BlockSpec((H_I, Q_TILE), lambda b, q: (0, b * nq + q)),
                  pl.BlockSpec((seq, D_I), lambda b, q: (b, 0)),
                  pl.BlockSpec((H_A, Q_TILE, D_C), lambda b, q: (0, b * nq + q, 0)),
                  pl.BlockSpec((seq, D_C), lambda b, q: (b, 0)),
                  pl.BlockSpec((Q_TILE, H_A * D_V), lambda b, q: (b * nq + q, 0)),
                  pl.BlockSpec(bt.shape, lambda b, q: (0, 0, 0, 0), pipeline_mode=pl.Buffered(1)),
                  pl.BlockSpec(wuv.shape, lambda b, q: (0, 0, 0), pipeline_mode=pl.Buffered(1))],
        out_specs=pl.BlockSpec((Q_TILE, H_A * D_V), lambda b, q: (b * nq + q, 0)),
        out_shape=jax.ShapeDtypeStruct((n, H_A * D_V), BF16),
        scratch_shapes=[pltpu.VMEM((n_kt, K_TILE, Q_TILE), I32),
                        pltpu.VMEM((n_kt, K_TILE // 16, 16, Q_TILE), I16),
                        pltpu.VMEM((n_kt, K_TILE // 16, 16, Q_TILE), I16),
                        pltpu.VMEM((n_kt, H_A, Q_TILE, K_TILE), F32),
                        pltpu.VMEM((H_A, Q_TILE, K_TILE // 2), F32),
                        pltpu.VMEM((H_A, Q_TILE, K_TILE // 2), F32),
                        pltpu.VMEM((H_A * Q_TILE, D_C), F32)],
        compiler_params=_params("parallel", "arbitrary"),
        name="a_attn",
    )(qidx, wt, kidx, qlat, ckv, z, bt, wuv)


def _b_proj_kernel(x_ref, nw_ref, wqkv_ref, wba_ref, wbat_ref, wz_ref, cw_ref, alog_ref, dtb_ref, alogc_ref, dtbc_ref,
                   q_ref, k_ref, v_ref, beta_ref, g_ref, gt_ref, z_ref, buf_ref, *, tile):
    s = pl.program_id(1)
    h = _rms(x_ref[...], nw_ref[...]).astype(BF16)
    z_ref[...] = _dot(h, wz_ref[...])
    ba = _dot(h, wba_ref[...])
    beta_ref[...] = _sigmoid(ba[:, :H_B])
    g_ref[...] = -jnp.exp(alog_ref[...]) * _softplus(ba[:, H_B:2 * H_B] + dtb_ref[...])
    bat = _dot_nt(wbat_ref[...], h)
    gt_ref[...] = -jnp.exp(alogc_ref[...]) * _softplus(bat[H_B:, :] + dtbc_ref[...])

    @pl.when(s == 0)
    def _():
        buf_ref[0:8, :] = jnp.zeros((8, buf_ref.shape[1]), F32)

    width = H_B * D_K
    for sec, out_ref in enumerate((q_ref, k_ref, v_ref)):
        cols = slice(sec * width, (sec + 1) * width)
        pre = _dot(h, wqkv_ref[:, cols])
        buf_ref[8:8 + tile, cols] = pre
        acc = pre * cw_ref[CONV_W - 1:CONV_W, cols]
        for w in range(CONV_W - 1):
            acc = acc + buf_ref[8 - (CONV_W - 1) + w:8 - (CONV_W - 1) + w + tile, cols] * cw_ref[w:w + 1, cols]
        buf_ref[0:8, cols] = buf_ref[tile:tile + 8, cols]
        y = acc * _sigmoid(acc)
        if sec == 2:
            out_ref[...] = y
        else:
            scale = (D_K ** -0.5) if sec == 0 else 1.0
            for hh in range(H_B):
                yy = y[:, hh * D_K:(hh + 1) * D_K]
                nrm = lax.rsqrt(jnp.sum(yy * yy, axis=-1, keepdims=True) + EPS)
                out_ref[:, hh * D_K:(hh + 1) * D_K] = yy * (nrm * scale)


def _b_proj(x2, nw, wqkv, wba, wbat, wz, cw, alog, dtb, batch, seq, tile):
    n, d = x2.shape
    ns = seq // tile
    grid = (batch, ns)
    tok = lambda w: pl.BlockSpec((tile, w), lambda b, s: (b * ns + s, 0))
    width = H_B * D_K
    kern = functools.partial(_b_proj_kernel, tile=tile)
    return pl.pallas_call(
        kern,
        grid=grid,
        in_specs=[tok(d), _const_spec(nw.shape), _const_spec(wqkv.shape), _const_spec(wba.shape),
                  _const_spec(wbat.shape), _const_spec(wz.shape), _const_spec(cw.shape),
                  _const_spec((1, H_B)), _const_spec((1, H_B)), _const_spec((H_B, 1)), _const_spec((H_B, 1))],
        out_specs=[tok(width), tok(width), tok(width), tok(H_B), tok(H_B),
                   pl.BlockSpec((H_B, tile), lambda b, s: (0, b * ns + s)), tok(width)],
        out_shape=[jax.ShapeDtypeStruct((n, width), F32)] * 3
        + [jax.ShapeDtypeStruct((n, H_B), F32)] * 2
        + [jax.ShapeDtypeStruct((H_B, n), F32), jax.ShapeDtypeStruct((n, width), F32)],
        scratch_shapes=[pltpu.VMEM((tile + 8, 3 * width), F32)],
        compiler_params=_params("parallel", "arbitrary"),
        name="b_proj",
    )(x2, nw, wqkv, wba, wbat, wz, cw, alog.reshape(1, H_B), dtb.reshape(1, H_B),
      alog.reshape(H_B, 1), dtb.reshape(H_B, 1))


def _b_gdn_kernel(q_ref, k_ref, v_ref, z_ref, beta_ref, g_ref, gt_ref, go_ref, y_ref, state_ref):
    hb = pl.program_id(1)
    s = pl.program_id(2)
    t = GDN_TILE
    n_chunks = t // CHUNK
    log_c = int(math.log2(CHUNK))
    heads = range(GDN_HEADS)

    @pl.when(s == 0)
    def _():
        state_ref[...] = jnp.zeros(state_ref.shape, F32)

    r = lax.broadcasted_iota(I32, (t, t), 0)
    c = lax.broadcasted_iota(I32, (t, t), 1)
    xs = r ^ c
    same = xs < CHUNK
    incl = same & (r >= c)
    upper = same & (r <= c)
    eye = (r == c).astype(F32)
    level = [(lax.shift_right_logical(xs, lb) == 1) & ((r & (1 << lb)) != 0) for lb in range(log_c)]
    lane8 = lax.broadcasted_iota(I32, (t, H_B), 1)
    sub8 = lax.broadcasted_iota(I32, (H_B, t), 0)

    b_col, decay, eg_col, ekl_col, egl_col = [], [], [], [], []
    for j in heads:
        head = hb * GDN_HEADS + j
        g_col = jnp.sum(jnp.where(lane8 == head, g_ref[...], 0.0), axis=1, keepdims=True)
        b_col.append(jnp.sum(jnp.where(lane8 == head, beta_ref[...], 0.0), axis=1, keepdims=True))
        g_row = jnp.sum(jnp.where(sub8 == head, gt_ref[...], 0.0), axis=0, keepdims=True)
        g_rows = jnp.broadcast_to(g_row, (t, t))
        gc_col = jnp.sum(jnp.where(incl, g_rows, 0.0), axis=1, keepdims=True)
        gl_col = jnp.sum(jnp.where(same, g_rows, 0.0), axis=1, keepdims=True)
        gc_row = jnp.sum(jnp.where(upper, jnp.broadcast_to(g_col, (t, t)), 0.0), axis=0, keepdims=True)
        decay.append(jnp.where(incl, jnp.exp(jnp.minimum(gc_col - gc_row, 0.0)), 0.0))
        eg_col.append(jnp.exp(gc_col))
        ekl_col.append(jnp.exp(gl_col - gc_col))
        egl_col.append(jnp.exp(gl_col))

    lmat, aqk, rhs, q_dec, k_dec = [], [], [], [], []
    for j in heads:
        hs = slice(j * D_K, (j + 1) * D_K)
        qf, kf, vf = q_ref[:, hs], k_ref[:, hs], v_ref[:, hs]
        kb = kf * b_col[j]
        k16 = kf.astype(BF16)
        gram = _dot_nt(jnp.concatenate([kb.astype(BF16), qf.astype(BF16)], axis=0), k16)
        lmat.append(gram[:t] * decay[j])
        aqk.append(jnp.where(incl, gram[t:] * decay[j], 0.0).astype(BF16))
        rhs.append(jnp.concatenate([(vf * b_col[j]).astype(BF16), (kb * eg_col[j]).astype(BF16)], axis=1))
        q_dec.append(qf * eg_col[j])
        k_dec.append((kf * ekl_col[j]).astype(BF16))

    tinv = [eye - jnp.where(level[0], lmat[j], 0.0) for j in heads]
    for lb in range(1, log_c):
        half = 1 << lb
        t16 = [tinv[j].astype(BF16) for j in heads]
        if half < 8:
            y16 = [_dot(jnp.where(level[lb], lmat[j], 0.0).astype(BF16), t16[j]).astype(BF16) for j in heads]
            tinv = [tinv[j] - _dot(t16[j], y16[j]) for j in heads]
            continue
        pairs = t // (2 * half)
        split = lambda a: a.reshape(pairs, 2, half, t)
        lower = lambda a: split(a)[:, 1].reshape(t // 2, t)
        i = lax.broadcasted_iota(I32, (t // 2, t), 0)
        r_low = lax.shift_left(lax.shift_right_logical(i, lb), lb + 1) + half + (i & (half - 1))
        in_b = lax.shift_right_logical(r_low ^ lax.broadcasted_iota(I32, (t // 2, t), 1), lb) == 1
        y_low = [_dot(jnp.where(in_b, lower(lmat[j]), 0.0).astype(BF16), t16[j]) for j in heads]
        zeros = jnp.zeros((pairs, half, t), F32)
        y16 = [jnp.stack([zeros, y_low[j].reshape(pairs, half, t)], axis=1).reshape(t, t).astype(BF16) for j in heads]
        t_low = [lower(tinv[j]) for j in heads]
        t_low = [t_low[j] - _dot(t_low[j].astype(BF16), y16[j]) for j in heads]
        tinv = [jnp.stack([split(tinv[j])[:, 0], t_low[j].reshape(pairs, half, t)], axis=1).reshape(t, t)
                for j in heads]

    sol16 = [_dot(tinv[j].astype(BF16), rhs[j]).astype(BF16) for j in heads]
    aux = [_dot(aqk[j], sol16[j]) for j in heads]
    q_eff = [(q_dec[j] - aux[j][:, D_VB:]).astype(BF16) for j in heads]
    kw = [[_dot_tn(k_dec[j][ci * CHUNK:(ci + 1) * CHUNK], sol16[j][ci * CHUNK:(ci + 1) * CHUNK])
           for ci in range(n_chunks)] for j in heads]

    state = [state_ref[j] for j in heads]
    outs = [[] for _ in heads]
    for ci in range(n_chunks):
        rs = slice(ci * CHUNK, (ci + 1) * CHUNK)
        for j in heads:
            s16 = state[j].astype(BF16)
            both = _dot(jnp.concatenate([q_eff[j][rs], kw[j][ci][:, D_VB:].astype(BF16)], axis=0), s16)
            outs[j].append(both[:CHUNK] + aux[j][rs, :D_VB])
            state[j] = state[j] * egl_col[j][ci * CHUNK:ci * CHUNK + 1, :] + kw[j][ci][:, :D_VB] - both[CHUNK:]

    for j in heads:
        hs = slice(j * D_K, (j + 1) * D_K)
        state_ref[j] = state[j]
        o = jnp.concatenate(outs[j], axis=0)
        zz = z_ref[:, hs]
        y_ref[:, hs] = (_rms(o, go_ref[...]) * (zz * _sigmoid(zz))).astype(BF16)


def _b_gdn(q, k, v, z, beta, g, gt, go, batch, seq):
    n = batch * seq
    ns = seq // GDN_TILE
    nh = H_B // GDN_HEADS
    grid = (batch, nh, ns)
    wide = pl.BlockSpec((GDN_TILE, GDN_HEADS * D_K), lambda b, h, s: (b * ns + s, h))
    narrow = pl.BlockSpec((GDN_TILE, H_B), lambda b, h, s: (b * ns + s, 0))
    return pl.pallas_call(
        _b_gdn_kernel,
        grid=grid,
        in_specs=[wide, wide, wide, wide, narrow, narrow,
                  pl.BlockSpec((H_B, GDN_TILE), lambda b, h, s: (0, b * ns + s)),
                  _const_spec(go.shape)],
        out_specs=wide,
        out_shape=jax.ShapeDtypeStruct((n, H_B * D_VB), BF16),
        scratch_shapes=[pltpu.VMEM((GDN_HEADS, D_K, D_VB), F32)],
        compiler_params=_params("parallel", "parallel", "arbitrary"),
        name="b_gdn",
    )(q, k, v, z, beta, g, gt, go)


def _out_ple_kernel(x_ref, y_ref, p_ref, wout_ref, pn_ref, wgate_ref, wproj_ref, o_ref):
    x1 = x_ref[...] + _dot(y_ref[...], wout_ref[...])
    hn = _rms(x1, pn_ref[...]).astype(BF16)
    gate = _sigmoid(_dot(hn, wgate_ref[...]))
    o_ref[...] = x1 + gate * _dot(p_ref[...].astype(BF16), wproj_ref[...])


def _out_ple(x2, y, p2, wout, pn, wgate, wproj, tile):
    n, d = x2.shape
    tok = lambda w: pl.BlockSpec((tile, w), lambda i: (i, 0))
    return pl.pallas_call(
        _out_ple_kernel,
        grid=(n // tile,),
        in_specs=[tok(d), tok(y.shape[1]), tok(p2.shape[1]), _const_spec(wout.shape), _const_spec(pn.shape),
                  _const_spec(wgate.shape), _const_spec(wproj.shape)],
        out_specs=tok(d),
        out_shape=jax.ShapeDtypeStruct((n, d), F32),
        compiler_params=_params("parallel"),
        name="out_ple",
    )(x2, y, p2, wout, pn, wgate, wproj)


def _t5_bucket(rel):
    max_exact = N_BUCKETS // 2
    rel = jnp.maximum(rel, 0)
    rel_f = jnp.maximum(rel, 1).astype(F32)
    log_ratio = jnp.log(rel_f / max_exact) / math.log(MAX_DISTANCE / max_exact)
    large = max_exact + (log_ratio * (N_BUCKETS - max_exact)).astype(I32)
    large = jnp.minimum(large, N_BUCKETS - 1)
    return jnp.where(rel < max_exact, rel, large)


def _bias_tiles(rel_bias):
    dist = jnp.arange(-(K_TILE - 1), BIAS_CLASSES * Q_TILE + 1, dtype=I32)
    table = (rel_bias.astype(F32) * LOG2_E)[_t5_bucket(dist)].T
    span = Q_TILE + K_TILE
    tiles = []
    for cls in range(BIAS_CLASSES):
        w = table[:, cls * Q_TILE:cls * Q_TILE + span]
        skew = jnp.tile(w, (1, Q_TILE + 1))[:, :Q_TILE * (span + 1)].reshape(H_A, Q_TILE, span + 1)
        tiles.append(skew[:, :, :K_TILE][:, :, ::-1])
    return jnp.stack(tiles)


def kernel(x, p, norm_w, a_w_in, a_g_cq, a_w_uq, a_w_uk, a_g_q, a_g_kv, a_w_iq, a_w_uv, a_w_out, rel_bias, b_w_in, b_conv_w, b_a_log, b_dt_bias, b_g_o, b_w_out, ple_norm, ple_w_gate, ple_w_proj):
    batch, seq, d_model = x.shape
    depth = p.shape[0]
    n = batch * seq
    assert seq % K_TILE == 0 and seq % GDN_TILE == 0 and H_B % GDN_HEADS == 0
    x2 = x.reshape(n, d_model)
    p2 = p.reshape(depth, n, p.shape[-1])
    bt = _bias_tiles(rel_bias)
    row = lambda a: a.reshape(1, -1).astype(F32)
    n_lat = D_CQ + D_C + D_I
    b_qkv = H_B * (2 * D_K + D_VB)
    for i in range(depth):
        j = i // 2
        if i % 2 == 0:
            w_in = a_w_in[j]
            wlat = w_in[:, :n_lat].astype(BF16)
            wwt = w_in[:, n_lat:n_lat + H_I].T.astype(BF16)
            wz = w_in[:, n_lat + H_I:].astype(BF16)
            qlat, qidx, wt, ckv, kidx, z = _a_proj(
                x2, row(norm_w[i]), wlat, wwt, wz, row(a_g_cq[j]), row(a_g_kv[j]),
                a_w_uq[j].reshape(D_CQ, H_A * D_NOPE).astype(BF16), a_w_uk[j].astype(BF16), row(a_g_q[j]),
                a_w_iq[j].reshape(D_CQ, H_I * D_I).astype(BF16), tile=1024)
            y = _a_attn(qidx, wt, kidx, qlat, ckv, z, bt, a_w_uv[j].astype(BF16), batch, seq)
            w_out = a_w_out[j]
        else:
            w_in = b_w_in[j]
            wqkv = w_in[:, :b_qkv].astype(BF16)
            w_ba = w_in[:, b_qkv:b_qkv + 2 * H_B]
            wba = jnp.pad(w_ba, ((0, 0), (0, LANES - 2 * H_B))).astype(BF16)
            wz = w_in[:, b_qkv + 2 * H_B:].astype(BF16)
            q, k, v, beta, g, gt, z = _b_proj(
                x2, row(norm_w[i]), wqkv, wba, w_ba.T.astype(BF16), wz, b_conv_w[j].astype(F32),
                b_a_log[j].astype(F32), b_dt_bias[j].astype(F32), batch, seq, tile=512)
            y = _b_gdn(q, k, v, z, beta, g, gt, row(b_g_o[j]), batch, seq)
            w_out = b_w_out[j]
        x2 = _out_ple(x2, y, p2[i], w_out.astype(BF16), row(ple_norm[i]), ple_w_gate[i].astype(BF16),
                      ple_w_proj[i].astype(BF16), tile=1024)
    return x2.reshape(batch, seq, d_model)
```

```python
import functools
import math

import jax
import jax.numpy as jnp
from jax import lax
from jax.experimental import pallas as pl
from jax.experimental.pallas import tpu as pltpu

F32 = jnp.float32
BF16 = jnp.bfloat16
I32 = jnp.int32
I16 = jnp.int16

EPS = 1e-6
NEG_INF = -1e30
LOG2_E = math.log2(math.e)
INT_MIN = -(2 ** 31)

LANES = 128
VMEM_LIMIT_BYTES = 56 * 1024 * 1024

H_A = 8
D_C = 256
D_CQ = 256
D_I = 128
H_I = 8
D_NOPE = 128
D_V = 128
TOPK_MAX = 256
N_BUCKETS = 32
MAX_DISTANCE = 128
Q_TILE = 256
K_TILE = 256
BIAS_CLASSES = -(-(MAX_DISTANCE + K_TILE - 1) // Q_TILE) + 1
H_B = 8
D_K = 128
D_VB = 128
CONV_W = 4
CHUNK = 64
GDN_TILE = 256
GDN_HEADS = 8


def _dot(a, b):
    return jnp.dot(a, b, preferred_element_type=F32)


def _dot_nt(a, b):
    return lax.dot_general(a, b, (((1,), (1,)), ((), ())), preferred_element_type=F32)


def _dot_tn(a, b):
    return lax.dot_general(a, b, (((0,), (0,)), ((), ())), preferred_element_type=F32)


def _rms(x, gain=None):
    y = x * lax.rsqrt(jnp.mean(x * x, axis=-1, keepdims=True) + EPS)
    return y if gain is None else y * gain


def _sigmoid(x):
    return 1.0 / (1.0 + jnp.exp(-x))


def _softplus(x):
    return jnp.maximum(x, 0.0) + jnp.log1p(jnp.exp(-jnp.abs(x)))


def _params(*semantics):
    return pltpu.CompilerParams(dimension_semantics=semantics, vmem_limit_bytes=VMEM_LIMIT_BYTES)


def _const_spec(shape):
    nd = len(shape)
    return pl.BlockSpec(shape, lambda *_: (0,) * nd)


def _a_proj_kernel(x_ref, nw_ref, wlat_ref, wwt_ref, wz_ref, gcq_ref, gkv_ref, wuq_ref, wuk_ref, gq_ref, wiq_ref,
                   qlat_ref, qidx_ref, wt_ref, ckv_ref, kidx_ref, z_ref):
    h = _rms(x_ref[...], nw_ref[...]).astype(BF16)
    lat = _dot(h, wlat_ref[...])
    cq = _rms(lat[:, :D_CQ], gcq_ref[...]).astype(BF16)
    ckv_ref[...] = _rms(lat[:, D_CQ:D_CQ + D_C], gkv_ref[...]).astype(BF16)
    kidx_ref[...] = _rms(lat[:, D_CQ + D_C:]).astype(BF16)
    wt_ref[...] = _dot_nt(wwt_ref[...], h) * (H_I ** -0.5)
    z_ref[...] = _dot(h, wz_ref[...])
    qn = _dot(cq, wuq_ref[...]).astype(BF16)
    for hh in range(H_A):
        ql = _dot(qn[:, hh * D_NOPE:(hh + 1) * D_NOPE], wuk_ref[hh])
        qlat_ref[hh] = (_rms(ql, gq_ref[...]) * (D_C ** -0.5 * LOG2_E)).astype(BF16)
    qi = _dot(cq, wiq_ref[...]) * (D_I ** -0.5)
    for hh in range(H_I):
        qidx_ref[hh] = qi[:, hh * D_I:(hh + 1) * D_I].astype(BF16)


def _a_proj(x2, nw, wlat, wwt, wz, gcq, gkv, wuq, wuk, gq, wiq, tile):
    n, d = x2.shape
    grid = (n // tile,)
    tok = lambda w: pl.BlockSpec((tile, w), lambda i: (i, 0))
    return pl.pallas_call(
        _a_proj_kernel,
        grid=grid,
        in_specs=[tok(d), _const_spec(nw.shape), _const_spec(wlat.shape), _const_spec(wwt.shape),
                  _const_spec(wz.shape), _const_spec(gcq.shape), _const_spec(gkv.shape), _const_spec(wuq.shape),
                  _const_spec(wuk.shape), _const_spec(gq.shape), _const_spec(wiq.shape)],
        out_specs=[pl.BlockSpec((H_A, tile, D_C), lambda i: (0, i, 0)),
                   pl.BlockSpec((H_I, tile, D_I), lambda i: (0, i, 0)),
                   pl.BlockSpec((H_I, tile), lambda i: (0, i)),
                   tok(D_C), tok(D_I), tok(H_A * D_V)],
        out_shape=[jax.ShapeDtypeStruct((H_A, n, D_C), BF16),
                   jax.ShapeDtypeStruct((H_I, n, D_I), BF16),
                   jax.ShapeDtypeStruct((H_I, n), F32),
                   jax.ShapeDtypeStruct((n, D_C), BF16),
                   jax.ShapeDtypeStruct((n, D_I), BF16),
                   jax.ShapeDtypeStruct((n, H_A * D_V), F32)],
        compiler_params=_params("parallel"),
        name="a_proj",
    )(x2, nw, wlat, wwt, wz, gcq, gkv, wuq, wuk, gq, wiq)


def _a_attn_kernel(qidx_ref, wt_ref, kidx_ref, qlat_ref, ckv_ref, z_ref, bt_ref, wuv_ref,
                   y_ref, keys_ref, hi_ref, lo_ref, lg_ref, mx_ref, ss_ref, oacc_ref, *, topk, n_q):
    qb = pl.program_id(1)
    n_kt = qb + 1
    rows = H_A * Q_TILE
    half = K_TILE // 2
    qidx = qidx_ref[...].reshape(H_I * Q_TILE, D_I)
    qlat = qlat_ref[...].reshape(rows, D_C)

    def for_tile_groups(body):
        def pair(i, carry):
            body(2 * i, 2)
            return carry

        lax.fori_loop(0, lax.shift_right_logical(n_kt, 1), pair, 0)

        @pl.when((n_kt & 1) == 1)
        def _():
            body(n_kt - 1, 1)

    def idx_tiles(c0, cnt):
        k0 = pl.multiple_of(c0 * K_TILE, K_TILE)
        lg = _dot_nt(kidx_ref[pl.ds(k0, cnt * K_TILE), :], qidx)
        sc = jnp.zeros((cnt * K_TILE, Q_TILE), F32)
        for hh in range(H_I):
            sc = sc + jnp.maximum(lg[:, hh * Q_TILE:(hh + 1) * Q_TILE], 0.0) * wt_ref[hh:hh + 1, :]
        bits = lax.bitcast_convert_type(sc, I32)
        sign = lax.shift_right_arithmetic(bits, 31)
        skey = (bits ^ (sign & 0x7FFFFFFF)) - sign
        key_pos = k0 + lax.broadcasted_iota(I32, (cnt * K_TILE, Q_TILE), 0)
        q_pos = qb * Q_TILE + lax.broadcasted_iota(I32, (cnt * K_TILE, Q_TILE), 1)
        skey = jnp.where(key_pos <= q_pos, skey, INT_MIN)
        hi = lax.shift_right_arithmetic(skey, 16).astype(I16)
        lo = ((skey & 0xFFFF) - 32768).astype(I16)
        for u in range(cnt):
            ks = slice(u * K_TILE, (u + 1) * K_TILE)
            keys_ref[c0 + u] = skey[ks]
            hi_ref[c0 + u] = hi[ks].reshape(K_TILE // 16, 16, Q_TILE)
            lo_ref[c0 + u] = lo[ks].reshape(K_TILE // 16, 16, Q_TILE)

    for_tile_groups(idx_tiles)

    one = jnp.ones((), BF16)
    zero = jnp.zeros((), BF16)

    def threshold(n_tiles):
        def count(ref16, bound16, strict):
            acc = jnp.zeros((16, Q_TILE), BF16)
            for c in range(n_tiles):
                tile = ref16[c]
                hit = jnp.where(tile > bound16[None] if strict else tile >= bound16[None], one, zero)
                parts = [hit[u] for u in range(K_TILE // 16)]
                while len(parts) > 1:
                    parts = [parts[u] + parts[u + 1] for u in range(0, len(parts), 2)]
                acc = acc + parts[0]
            return jnp.sum(acc.astype(F32), axis=0, keepdims=True)

        def kth_largest(ref16, need, n_all):
            def bit_body(i, carry):
                thr, n_ge = carry
                cand = thr + lax.shift_left(jnp.int32(1), 15 - i)
                n_cand = count(ref16, cand.astype(I16), False)
                ok = n_cand >= need
                return jnp.where(ok, cand, thr), jnp.where(ok, n_cand, n_ge)

            return lax.fori_loop(0, 16, bit_body, (jnp.full((16, Q_TILE), -32768, I32), n_all))

        n_all = jnp.full((1, Q_TILE), float(n_tiles * K_TILE), F32)
        hi_thr, n_ge_hi = kth_largest(hi_ref, float(topk), n_all)
        hi_thr16 = hi_thr.astype(I16)
        n_gt_hi = count(hi_ref, hi_thr16, True)
        for c in range(n_tiles):
            lo_ref[c] = jnp.where(hi_ref[c] == hi_thr16[None], lo_ref[c], jnp.int16(-32768))
        lo_thr, n_ge_lo = kth_largest(lo_ref, float(topk) - n_gt_hi, n_ge_hi - n_gt_hi)
        thr = hi_thr[0:1] * 65536 + (lo_thr[0:1] + 32768)
        thr = jnp.maximum(thr, INT_MIN + 1)
        return thr, n_gt_hi + n_ge_lo - float(topk)

    thr, excess = lax.switch(qb, [functools.partial(threshold, n) for n in range(1, n_q + 1)])

    @pl.when(jnp.max(excess) > 0.0)
    def _():
        kr = lax.broadcasted_iota(I32, (K_TILE, K_TILE), 0)
        kc = lax.broadcasted_iota(I32, (K_TILE, K_TILE), 1)
        later = jnp.where(kc >= kr, 1.0, 0.0).astype(BF16)

        def tie_body(i, seen):
            c = n_kt - 1 - i
            keys = keys_ref[c]
            tie = keys == thr
            rank = _dot(later, jnp.where(tie, 1.0, 0.0).astype(BF16)) + seen
            keys_ref[c] = jnp.where(tie & (rank <= excess), thr - 1, keys)
            return rank[0:1]

        lax.fori_loop(0, n_kt, tie_body, jnp.zeros((1, Q_TILE), F32))

    mx_ref[...] = jnp.full(mx_ref.shape, NEG_INF, F32)

    def qk_tiles(c0, cnt):
        k0 = pl.multiple_of(c0 * K_TILE, K_TILE)
        lg = _dot_nt(qlat, ckv_ref[pl.ds(k0, cnt * K_TILE), :])
        for u in range(cnt):
            c = c0 + u
            mb = jnp.where(keys_ref[c] >= thr, 0.0, NEG_INF).T
            t_idx = jnp.minimum(qb - c, BIAS_CLASSES - 1)
            for hh in range(H_A):
                l = lg[hh * Q_TILE:(hh + 1) * Q_TILE, u * K_TILE:(u + 1) * K_TILE] + (mb + bt_ref[t_idx, hh])
                lg_ref[c, hh] = l
                mx_ref[hh] = jnp.maximum(mx_ref[hh], jnp.maximum(l[:, :half], l[:, half:]))

    for_tile_groups(qk_tiles)

    m = jnp.max(mx_ref[...], axis=-1, keepdims=True)
    ss_ref[...] = jnp.zeros(ss_ref.shape, F32)
    oacc_ref[...] = jnp.zeros(oacc_ref.shape, F32)

    def pv_tiles(c0, cnt):
        k0 = pl.multiple_of(c0 * K_TILE, K_TILE)
        ps = [jnp.exp2(lg_ref[c0 + u] - m) for u in range(cnt)]
        part = ps[0][:, :, :half] + ps[0][:, :, half:]
        for p in ps[1:]:
            part = part + (p[:, :, :half] + p[:, :, half:])
        ss_ref[...] += part
        p16 = jnp.concatenate([p.reshape(rows, K_TILE).astype(BF16) for p in ps], axis=1)
        oacc_ref[...] += _dot(p16, ckv_ref[pl.ds(k0, cnt * K_TILE), :])

    for_tile_groups(pv_tiles)
    inv_denom = 1.0 / jnp.sum(ss_ref[...], axis=-1, keepdims=True)

    for hh in range(H_A):
        oh = _dot(oacc_ref[hh * Q_TILE:(hh + 1) * Q_TILE, :].astype(BF16), wuv_ref[hh])
        zz = z_ref[:, hh * D_V:(hh + 1) * D_V]
        y_ref[:, hh * D_V:(hh + 1) * D_V] = (oh * inv_denom[hh] * (zz * _sigmoid(zz))).astype(BF16)


def _a_attn(qidx, wt, kidx, qlat, ckv, z, bt, wuv, batch, seq):
    n = batch * seq
    nq = seq // Q_TILE
    n_kt = seq // K_TILE
    topk = min(TOPK_MAX, seq // 4)
    grid = (batch, nq)
    kern = functools.partial(_a_attn_kernel, topk=topk, n_q=nq)
    return pl.pallas_call(
        kern,
        grid=grid,
        in_specs=[pl.BlockSpec((H_I, Q_TILE, D_I), lambda b, q: (0, b * nq + q, 0)),
                  pl.BlockSpec((H_I, Q_TILE), lambda b, q: (0, b * nq + q)),
                  pl.BlockSpec((seq, D_I), lambda b, q: (b, 0)),
                  pl.BlockSpec((H_A, Q_TILE, D_C), lambda b, q: (0, b * nq + q, 0)),
                  pl.BlockSpec((seq, D_C), lambda b, q: (b, 0)),
                  pl.BlockSpec((Q_TILE, H_A * D_V), lambda b, q: (b * nq + q, 0)),
                  pl.BlockSpec(bt.shape, lambda b, q: (0, 0, 0, 0), pipeline_mode=pl.Buffered(1)),
                  pl.BlockSpec(wuv.shape, lambda b, q: (0, 0, 0), pipeline_mode=pl.Buffered(1))],
        out_specs=pl.BlockSpec((Q_TILE, H_A * D_V), lambda b, q: (b * nq + q, 0)),
        out_shape=jax.ShapeDtypeStruct((n, H_A * D_V), BF16),
        scratch_shapes=[pltpu.VMEM((n_kt, K_TILE, Q_TILE), I32),
                        pltpu.VMEM((n_kt, K_TILE // 16, 16, Q_TILE), I16),
                        pltpu.VMEM((n_kt, K_TILE // 16, 16, Q_TILE), I16),
                        pltpu.VMEM((n_kt, H_A, Q_TILE, K_TILE), F32),
                        pltpu.VMEM((H_A, Q_TILE, K_TILE // 2), F32),
                        pltpu.VMEM((H_A, Q_TILE, K_TILE // 2), F32),
                        pltpu.VMEM((H_A * Q_TILE, D_C), F32)],
        compiler_params=_params("parallel", "arbitrary"),
        name="a_attn",
    )(qidx, wt, kidx, qlat, ckv, z, bt, wuv)


def _b_proj_kernel(x_ref, nw_ref, wqkv_ref, wba_ref, wbat_ref, wz_ref, cw_ref, alog_ref, dtb_ref, alogc_ref, dtbc_ref,
                   q_ref, k_ref, v_ref, beta_ref, g_ref, gt_ref, z_ref, buf_ref, *, tile):
    s = pl.program_id(1)
    h = _rms(x_ref[...], nw_ref[...]).astype(BF16)
    z_ref[...] = _dot(h, wz_ref[...])
    ba = _dot(h, wba_ref[...])
    beta_ref[...] = _sigmoid(ba[:, :H_B])
    g_ref[...] = -jnp.exp(alog_ref[...]) * _softplus(ba[:, H_B:2 * H_B] + dtb_ref[...])
    bat = _dot_nt(wbat_ref[...], h)
    gt_ref[...] = -jnp.exp(alogc_ref[...]) * _softplus(bat[H_B:, :] + dtbc_ref[...])

    @pl.when(s == 0)
    def _():
        buf_ref[0:8, :] = jnp.zeros((8, buf_ref.shape[1]), F32)

    width = H_B * D_K
    for sec, out_ref in enumerate((q_ref, k_ref, v_ref)):
        cols = slice(sec * width, (sec + 1) * width)
        pre = _dot(h, wqkv_ref[:, cols])
        buf_ref[8:8 + tile, cols] = pre
        acc = pre * cw_ref[CONV_W - 1:CONV_W, cols]
        for w in range(CONV_W - 1):
            acc = acc + buf_ref[8 - (CONV_W - 1) + w:8 - (CONV_W - 1) + w + tile, cols] * cw_ref[w:w + 1, cols]
        buf_ref[0:8, cols] = buf_ref[tile:tile + 8, cols]
        y = acc * _sigmoid(acc)
        if sec == 2:
            out_ref[...] = y
        else:
            scale = (D_K ** -0.5) if sec == 0 else 1.0
            for hh in range(H_B):
                yy = y[:, hh * D_K:(hh + 1) * D_K]
                nrm = lax.rsqrt(jnp.sum(yy * yy, axis=-1, keepdims=True) + EPS)
                out_ref[:, hh * D_K:(hh + 1) * D_K] = yy * (nrm * scale)


def _b_proj(x2, nw, wqkv, wba, wbat, wz, cw, alog, dtb, batch, seq, tile):
    n, d = x2.shape
    ns = seq // tile
    grid = (batch, ns)
    tok = lambda w: pl.BlockSpec((tile, w), lambda b, s: (b * ns + s, 0))
    width = H_B * D_K
    kern = functools.partial(_b_proj_kernel, tile=tile)
    return pl.pallas_call(
        kern,
        grid=grid,
        in_specs=[tok(d), _const_spec(nw.shape), _const_spec(wqkv.shape), _const_spec(wba.shape),
                  _const_spec(wbat.shape), _const_spec(wz.shape), _const_spec(cw.shape),
                  _const_spec((1, H_B)), _const_spec((1, H_B)), _const_spec((H_B, 1)), _const_spec((H_B, 1))],
        out_specs=[tok(width), tok(width), tok(width), tok(H_B), tok(H_B),
                   pl.BlockSpec((H_B, tile), lambda b, s: (0, b * ns + s)), tok(width)],
        out_shape=[jax.ShapeDtypeStruct((n, width), F32)] * 3
        + [jax.ShapeDtypeStruct((n, H_B), F32)] * 2
        + [jax.ShapeDtypeStruct((H_B, n), F32), jax.ShapeDtypeStruct((n, width), F32)],
        scratch_shapes=[pltpu.VMEM((tile + 8, 3 * width), F32)],
        compiler_params=_params("parallel", "arbitrary"),
        name="b_proj",
    )(x2, nw, wqkv, wba, wbat, wz, cw, alog.reshape(1, H_B), dtb.reshape(1, H_B),
      alog.reshape(H_B, 1), dtb.reshape(H_B, 1))


def _b_gdn_kernel(q_ref, k_ref, v_ref, z_ref, beta_ref, g_ref, gt_ref, go_ref, y_ref, state_ref):
    hb = pl.program_id(1)
    s = pl.program_id(2)
    t = GDN_TILE
    n_chunks = t // CHUNK
    log_c = int(math.log2(CHUNK))
    heads = range(GDN_HEADS)

    @pl.when(s == 0)
    def _():
        state_ref[...] = jnp.zeros(state_ref.shape, F32)

    r = lax.broadcasted_iota(I32, (t, t), 0)
    c = lax.broadcasted_iota(I32, (t, t), 1)
    xs = r ^ c
    same = xs < CHUNK
    incl = same & (r >= c)
    upper = same & (r <= c)
    eye = (r == c).astype(F32)
    level = [(lax.shift_right_logical(xs, lb) == 1) & ((r & (1 << lb)) != 0) for lb in range(log_c)]
    lane8 = lax.broadcasted_iota(I32, (t, H_B), 1)
    sub8 = lax.broadcasted_iota(I32, (H_B, t), 0)

    b_col, decay, eg_col, ekl_col, egl_col = [], [], [], [], []
    for j in heads:
        head = hb * GDN_HEADS + j
        g_col = jnp.sum(jnp.where(lane8 == head, g_ref[...], 0.0), axis=1, keepdims=True)
        b_col.append(jnp.sum(jnp.where(lane8 == head, beta_ref[...], 0.0), axis=1, keepdims=True))
        g_row = jnp.sum(jnp.where(sub8 == head, gt_ref[...], 0.0), axis=0, keepdims=True)
        g_rows = jnp.broadcast_to(g_row, (t, t))
        gc_col = jnp.sum(jnp.where(incl, g_rows, 0.0), axis=1, keepdims=True)
        gl_col = jnp.sum(jnp.where(same, g_rows, 0.0), axis=1, keepdims=True)
        gc_row = jnp.sum(jnp.where(upper, jnp.broadcast_to(g_col, (t, t)), 0.0), axis=0, keepdims=True)
        decay.append(jnp.where(incl, jnp.exp(jnp.minimum(gc_col - gc_row, 0.0)), 0.0))
        eg_col.append(jnp.exp(gc_col))
        ekl_col.append(jnp.exp(gl_col - gc_col))
        egl_col.append(jnp.exp(gl_col))

    lmat, aqk, rhs, q_dec, k_dec = [], [], [], [], []
    for j in heads:
        hs = slice(j * D_K, (j + 1) * D_K)
        qf, kf, vf = q_ref[:, hs], k_ref[:, hs], v_ref[:, hs]
        kb = kf * b_col[j]
        k16 = kf.astype(BF16)
        gram = _dot_nt(jnp.concatenate([kb.astype(BF16), qf.astype(BF16)], axis=0), k16)
        lmat.append(gram[:t] * decay[j])
        aqk.append(jnp.where(incl, gram[t:] * decay[j], 0.0).astype(BF16))
        rhs.append(jnp.concatenate([(vf * b_col[j]).astype(BF16), (kb * eg_col[j]).astype(BF16)], axis=1))
        q_dec.append(qf * eg_col[j])
        k_dec.append((kf * ekl_col[j]).astype(BF16))

    tinv = [eye - jnp.where(level[0], lmat[j], 0.0) for j in heads]
    for lb in range(1, log_c):
        half = 1 << lb
        t16 = [tinv[j].astype(BF16) for j in heads]
        if half < 8:
            y16 = [_dot(jnp.where(level[lb], lmat[j], 0.0).astype(BF16), t16[j]).astype(BF16) for j in heads]
            tinv = [tinv[j] - _dot(t16[j], y16[j]) for j in heads]
            continue
        pairs = t // (2 * half)
        split = lambda a: a.reshape(pairs, 2, half, t)
        lower = lambda a: split(a)[:, 1].reshape(t // 2, t)
        i = lax.broadcasted_iota(I32, (t // 2, t), 0)
        r_low = lax.shift_left(lax.shift_right_logical(i, lb), lb + 1) + half + (i & (half - 1))
        in_b = lax.shift_right_logical(r_low ^ lax.broadcasted_iota(I32, (t // 2, t), 1), lb) == 1
        y_low = [_dot(jnp.where(in_b, lower(lmat[j]), 0.0).astype(BF16), t16[j]) for j in heads]
        zeros = jnp.zeros((pairs, half, t), F32)
        y16 = [jnp.stack([zeros, y_low[j].reshape(pairs, half, t)], axis=1).reshape(t, t).astype(BF16) for j in heads]
        t_low = [lower(tinv[j]) for j in heads]
        t_low = [t_low[j] - _dot(t_low[j].astype(BF16), y16[j]) for j in heads]
        tinv = [jnp.stack([split(tinv[j])[:, 0], t_low[j].reshape(pairs, half, t)], axis=1).reshape(t, t)
                for j in heads]

    sol16 = [_dot(tinv[j].astype(BF16), rhs[j]).astype(BF16) for j in heads]
    aux = [_dot(aqk[j], sol16[j]) for j in heads]
    q_eff = [(q_dec[j] - aux[j][:, D_VB:]).astype(BF16) for j in heads]
    kw = [[_dot_tn(k_dec[j][ci * CHUNK:(ci + 1) * CHUNK], sol16[j][ci * CHUNK:(ci + 1) * CHUNK])
           for ci in range(n_chunks)] for j in heads]

    state = [state_ref[j] for j in heads]
    outs = [[] for _ in heads]
    for ci in range(n_chunks):
        rs = slice(ci * CHUNK, (ci + 1) * CHUNK)
        for j in heads:
            s16 = state[j].astype(BF16)
            both = _dot(jnp.concatenate([q_eff[j][rs], kw[j][ci][:, D_VB:].astype(BF16)], axis=0), s16)
            outs[j].append(both[:CHUNK] + aux[j][rs, :D_VB])
            state[j] = state[j] * egl_col[j][ci * CHUNK:ci * CHUNK + 1, :] + kw[j][ci][:, :D_VB] - both[CHUNK:]

    for j in heads:
        hs = slice(j * D_K, (j + 1) * D_K)
        state_ref[j] = state[j]
        o = jnp.concatenate(outs[j], axis=0)
        zz = z_ref[:, hs]
        y_ref[:, hs] = (_rms(o, go_ref[...]) * (zz * _sigmoid(zz))).astype(BF16)


def _b_gdn(q, k, v, z, beta, g, gt, go, batch, seq):
    n = batch * seq
    ns = seq // GDN_TILE
    nh = H_B // GDN_HEADS
    grid = (batch, nh, ns)
    wide = pl.BlockSpec((GDN_TILE, GDN_HEADS * D_K), lambda b, h, s: (b * ns + s, h))
    narrow = pl.BlockSpec((GDN_TILE, H_B), lambda b, h, s: (b * ns + s, 0))
    return pl.pallas_call(
        _b_gdn_kernel,
        grid=grid,
        in_specs=[wide, wide, wide, wide, narrow, narrow,
                  pl.BlockSpec((H_B, GDN_TILE), lambda b, h, s: (0, b * ns + s)),
                  _const_spec(go.shape)],
        out_specs=wide,
        out_shape=jax.ShapeDtypeStruct((n, H_B * D_VB), BF16),
        scratch_shapes=[pltpu.VMEM((GDN_HEADS, D_K, D_VB), F32)],
        compiler_params=_params("parallel", "parallel", "arbitrary"),
        name="b_gdn",
    )(q, k, v, z, beta, g, gt, go)


def _out_ple_kernel(x_ref, y_ref, p_ref, wout_ref, pn_ref, wgate_ref, wproj_ref, o_ref):
    x1 = x_ref[...] + _dot(y_ref[...], wout_ref[...])
    hn = _rms(x1, pn_ref[...]).astype(BF16)
    gate = _sigmoid(_dot(hn, wgate_ref[...]))
    o_ref[...] = x1 + gate * _dot(p_ref[...].astype(BF16), wproj_ref[...])


def _out_ple(x2, y, p2, layer, wout, pn, wgate, wproj, tile):
    n, d = x2.shape
    tok = lambda w: pl.BlockSpec((tile, w), lambda i: (i, 0))
    return pl.pallas_call(
        _out_ple_kernel,
        grid=(n // tile,),
        in_specs=[tok(d), tok(y.shape[1]),
                  pl.BlockSpec((tile, p2.shape[1]), lambda i: (layer * (n // tile) + i, 0)),
                  _const_spec(wout.shape), _const_spec(pn.shape),
                  _const_spec(wgate.shape), _const_spec(wproj.shape)],
        out_specs=tok(d),
        out_shape=jax.ShapeDtypeStruct((n, d), F32),
        compiler_params=_params("parallel"),
        name="out_ple",
    )(x2, y, p2, wout, pn, wgate, wproj)


def _t5_bucket(rel):
    max_exact = N_BUCKETS // 2
    rel = jnp.maximum(rel, 0)
    rel_f = jnp.maximum(rel, 1).astype(F32)
    log_ratio = jnp.log(rel_f / max_exact) / math.log(MAX_DISTANCE / max_exact)
    large = max_exact + (log_ratio * (N_BUCKETS - max_exact)).astype(I32)
    large = jnp.minimum(large, N_BUCKETS - 1)
    return jnp.where(rel < max_exact, rel, large)


def _bias_tiles(rel_bias):
    span = Q_TILE + K_TILE
    m = jnp.arange(span, dtype=I32)
    key_minus_query = jnp.where(m < K_TILE, m, m - span)
    scaled = rel_bias.astype(F32) * LOG2_E
    tiles = []
    for cls in range(BIAS_CLASSES):
        w = scaled[_t5_bucket(cls * Q_TILE - key_minus_query)].T
        skew = jnp.tile(w, (1, Q_TILE))[:, :Q_TILE * (span - 1)].reshape(H_A, Q_TILE, span - 1)
        tiles.append(skew[:, :, :K_TILE])
    return jnp.stack(tiles)


def kernel(x, p, norm_w, a_w_in, a_g_cq, a_w_uq, a_w_uk, a_g_q, a_g_kv, a_w_iq, a_w_uv, a_w_out, rel_bias, b_w_in, b_conv_w, b_a_log, b_dt_bias, b_g_o, b_w_out, ple_norm, ple_w_gate, ple_w_proj):
    batch, seq, d_model = x.shape
    depth = p.shape[0]
    n = batch * seq
    assert seq % K_TILE == 0 and seq % GDN_TILE == 0 and H_B % GDN_HEADS == 0
    x2 = x.reshape(n, d_model)
    p2 = p.reshape(depth * n, p.shape[-1])
    bt = _bias_tiles(rel_bias)
    row = lambda a: a.reshape(1, -1).astype(F32)
    n_lat = D_CQ + D_C + D_I
    b_qkv = H_B * (2 * D_K + D_VB)
    for i in range(depth):
        j = i // 2
        if i % 2 == 0:
            w_in = a_w_in[j]
            wlat = w_in[:, :n_lat].astype(BF16)
            wwt = w_in[:, n_lat:n_lat + H_I].T.astype(BF16)
            wz = w_in[:, n_lat + H_I:].astype(BF16)
            qlat, qidx, wt, ckv, kidx, z = _a_proj(
                x2, row(norm_w[i]), wlat, wwt, wz, row(a_g_cq[j]), row(a_g_kv[j]),
                a_w_uq[j].reshape(D_CQ, H_A * D_NOPE).astype(BF16), a_w_uk[j].astype(BF16), row(a_g_q[j]),
                a_w_iq[j].reshape(D_CQ, H_I * D_I).astype(BF16), tile=1024)
            y = _a_attn(qidx, wt, kidx, qlat, ckv, z, bt, a_w_uv[j].astype(BF16), batch, seq)
            w_out = a_w_out[j]
        else:
            w_in = b_w_in[j]
            wqkv = w_in[:, :b_qkv].astype(BF16)
            w_ba = w_in[:, b_qkv:b_qkv + 2 * H_B]
            wba = jnp.pad(w_ba, ((0, 0), (0, LANES - 2 * H_B))).astype(BF16)
            wz = w_in[:, b_qkv + 2 * H_B:].astype(BF16)
            q, k, v, beta, g, gt, z = _b_proj(
                x2, row(norm_w[i]), wqkv, wba, w_ba.T.astype(BF16), wz, b_conv_w[j].astype(F32),
                b_a_log[j].astype(F32), b_dt_bias[j].astype(F32), batch, seq, tile=512)
            y = _b_gdn(q, k, v, z, beta, g, gt, row(b_g_o[j]), batch, seq)
            w_out = b_w_out[j]
        x2 = _out_ple(x2, y, p2, i, w_out.astype(BF16), row(ple_norm[i]), ple_w_gate[i].astype(BF16),
                      ple_w_proj[i].astype(BF16), tile=1024)
    return x2.reshape(batch, seq, d_model)
```

```python
import functools
import math

import jax
import jax.numpy as jnp
from jax import lax
from jax.experimental import pallas as pl
from jax.experimental.pallas import tpu as pltpu

F32 = jnp.float32
BF16 = jnp.bfloat16
I32 = jnp.int32
I16 = jnp.int16

EPS = 1e-6
NEG_INF = -1e30
LOG2_E = math.log2(math.e)
INT_MIN = -(2 ** 31)
HALF_BITS = 16
I16_MIN = -(1 << (HALF_BITS - 1))

LANES = 128
PACKED_ROWS = 16
VMEM_LIMIT_BYTES = 56 * 1024 * 1024

H_A = 8
D_C = 256
D_CQ = 256
D_I = 128
H_I = 8
D_NOPE = 128
D_V = 128
TOPK_MAX = 256
N_BUCKETS = 32
MAX_DISTANCE = 128
Q_TILE = 256
K_TILE = 256
BIAS_CLASSES = -(-(MAX_DISTANCE + K_TILE - 1) // Q_TILE) + 1
H_B = 8
D_K = 128
D_VB = 128
CONV_W = 4
CHUNK = 64
GDN_TILE = 256
GDN_HEADS = 8


def _dot(a, b):
    return jnp.dot(a, b, preferred_element_type=F32)


def _dot_nt(a, b):
    return lax.dot_general(a, b, (((1,), (1,)), ((), ())), preferred_element_type=F32)


def _dot_tn(a, b):
    return lax.dot_general(a, b, (((0,), (0,)), ((), ())), preferred_element_type=F32)


def _rms(x, gain=None):
    y = x * lax.rsqrt(jnp.mean(x * x, axis=-1, keepdims=True) + EPS)
    return y if gain is None else y * gain


def _sigmoid(x):
    return 1.0 / (1.0 + jnp.exp2(x * -LOG2_E))


def _softplus(x):
    return jnp.maximum(x, 0.0) + jnp.log1p(jnp.exp(-jnp.abs(x)))


def _params(*semantics):
    return pltpu.CompilerParams(dimension_semantics=semantics, vmem_limit_bytes=VMEM_LIMIT_BYTES)


def _const_spec(shape):
    nd = len(shape)
    return pl.BlockSpec(shape, lambda *_: (0,) * nd)


def _a_proj_kernel(x_ref, nw_ref, wlat_ref, wwt_ref, wz_ref, gcq_ref, gkv_ref, wuq_ref, wuk_ref, gq_ref, wiq_ref,
                   qlat_ref, qidx_ref, wt_ref, ckv_ref, kidx_ref, z_ref):
    h = _rms(x_ref[...], nw_ref[...]).astype(BF16)
    lat = _dot(h, wlat_ref[...])
    cq = _rms(lat[:, :D_CQ], gcq_ref[...]).astype(BF16)
    ckv_ref[...] = _rms(lat[:, D_CQ:D_CQ + D_C], gkv_ref[...]).astype(BF16)
    kidx_ref[...] = _rms(lat[:, D_CQ + D_C:]).astype(BF16)
    wt_ref[...] = _dot_nt(wwt_ref[...], h) * (H_I ** -0.5)
    z_ref[...] = _dot(h, wz_ref[...])
    qn = _dot(cq, wuq_ref[...]).astype(BF16)
    for hh in range(H_A):
        ql = _dot(qn[:, hh * D_NOPE:(hh + 1) * D_NOPE], wuk_ref[hh])
        qlat_ref[hh] = (_rms(ql, gq_ref[...]) * (D_C ** -0.5 * LOG2_E)).astype(BF16)
    qi = _dot(cq, wiq_ref[...]) * (D_I ** -0.5)
    for hh in range(H_I):
        qidx_ref[hh] = qi[:, hh * D_I:(hh + 1) * D_I].astype(BF16)


def _a_proj(x2, nw, wlat, wwt, wz, gcq, gkv, wuq, wuk, gq, wiq, tile):
    n, d = x2.shape
    grid = (n // tile,)
    tok = lambda w: pl.BlockSpec((tile, w), lambda i: (i, 0))
    return pl.pallas_call(
        _a_proj_kernel,
        grid=grid,
        in_specs=[tok(d), _const_spec(nw.shape), _const_spec(wlat.shape), _const_spec(wwt.shape),
                  _const_spec(wz.shape), _const_spec(gcq.shape), _const_spec(gkv.shape), _const_spec(wuq.shape),
                  _const_spec(wuk.shape), _const_spec(gq.shape), _const_spec(wiq.shape)],
        out_specs=[pl.BlockSpec((H_A, tile, D_C), lambda i: (0, i, 0)),
                   pl.BlockSpec((H_I, tile, D_I), lambda i: (0, i, 0)),
                   pl.BlockSpec((H_I, tile), lambda i: (0, i)),
                   tok(D_C), tok(D_I), tok(H_A * D_V)],
        out_shape=[jax.ShapeDtypeStruct((H_A, n, D_C), BF16),
                   jax.ShapeDtypeStruct((H_I, n, D_I), BF16),
                   jax.ShapeDtypeStruct((H_I, n), F32),
                   jax.ShapeDtypeStruct((n, D_C), BF16),
                   jax.ShapeDtypeStruct((n, D_I), BF16),
                   jax.ShapeDtypeStruct((n, H_A * D_V), F32)],
        compiler_params=_params("parallel"),
        name="a_proj",
    )(x2, nw, wlat, wwt, wz, gcq, gkv, wuq, wuk, gq, wiq)


def _a_attn_kernel(qidx_ref, wt_ref, kidx_ref, qlat_ref, ckv_ref, z_ref, bt_ref, wuv_ref,
                   y_ref, keys_ref, hi_ref, lo_ref, lg_ref, mx_ref, ss_ref, oacc_ref, *, topk, n_q):
    qb = pl.program_id(1)
    n_kt = qb + 1
    rows = H_A * Q_TILE
    half = K_TILE // 2
    qidx = qidx_ref[...].reshape(H_I * Q_TILE, D_I)
    qlat = qlat_ref[...].reshape(rows, D_C)

    def for_tile_groups(body):
        def pair(i, carry):
            body(2 * i, 2)
            return carry

        lax.fori_loop(0, lax.shift_right_logical(n_kt, 1), pair, 0)

        @pl.when((n_kt & 1) == 1)
        def _():
            body(n_kt - 1, 1)

    def idx_tiles(c0, cnt):
        k0 = pl.multiple_of(c0 * K_TILE, K_TILE)
        lg = _dot_nt(kidx_ref[pl.ds(k0, cnt * K_TILE), :], qidx)
        sc = jnp.zeros((cnt * K_TILE, Q_TILE), F32)
        for hh in range(H_I):
            sc = sc + jnp.maximum(lg[:, hh * Q_TILE:(hh + 1) * Q_TILE], 0.0) * wt_ref[hh:hh + 1, :]
        bits = lax.bitcast_convert_type(sc, I32)
        sign = lax.shift_right_arithmetic(bits, 31)
        skey = (bits ^ (sign & 0x7FFFFFFF)) - sign
        key_pos = k0 + lax.broadcasted_iota(I32, (cnt * K_TILE, Q_TILE), 0)
        q_pos = qb * Q_TILE + lax.broadcasted_iota(I32, (cnt * K_TILE, Q_TILE), 1)
        skey = jnp.where(key_pos <= q_pos, skey, INT_MIN)
        hi = lax.shift_right_arithmetic(skey, HALF_BITS).astype(I16)
        lo = ((skey & ((1 << HALF_BITS) - 1)) + I16_MIN).astype(I16)
        for u in range(cnt):
            ks = slice(u * K_TILE, (u + 1) * K_TILE)
            keys_ref[c0 + u] = skey[ks]
            hi_ref[c0 + u] = hi[ks].reshape(K_TILE // PACKED_ROWS, PACKED_ROWS, Q_TILE)
            lo_ref[c0 + u] = lo[ks].reshape(K_TILE // PACKED_ROWS, PACKED_ROWS, Q_TILE)

    for_tile_groups(idx_tiles)

    one = jnp.ones((), BF16)
    zero = jnp.zeros((), BF16)

    def threshold(n_tiles):
        def count(ref16, bound16, strict):
            acc = jnp.zeros((PACKED_ROWS, Q_TILE), BF16)
            for c in range(n_tiles):
                tile = ref16[c]
                hit = jnp.where(tile > bound16[None] if strict else tile >= bound16[None], one, zero)
                parts = [hit[u] for u in range(K_TILE // PACKED_ROWS)]
                while len(parts) > 1:
                    parts = [parts[u] + parts[u + 1] for u in range(0, len(parts), 2)]
                acc = acc + parts[0]
            return jnp.sum(acc.astype(F32), axis=0, keepdims=True)

        def kth_largest(ref16, need, n_all):
            def bit_body(i, carry):
                thr, n_ge = carry
                cand = thr + lax.shift_left(jnp.int32(1), HALF_BITS - 1 - i)
                n_cand = count(ref16, cand.astype(I16), False)
                ok = n_cand >= need
                return jnp.where(ok, cand, thr), jnp.where(ok, n_cand, n_ge)

            return lax.fori_loop(0, HALF_BITS, bit_body, (jnp.full((PACKED_ROWS, Q_TILE), I16_MIN, I32), n_all))

        n_all = jnp.full((1, Q_TILE), float(n_tiles * K_TILE), F32)
        hi_thr, n_ge_hi = kth_largest(hi_ref, float(topk), n_all)
        hi_thr16 = hi_thr.astype(I16)
        n_gt_hi = count(hi_ref, hi_thr16, True)
        for c in range(n_tiles):
            lo_ref[c] = jnp.where(hi_ref[c] == hi_thr16[None], lo_ref[c], jnp.int16(I16_MIN))
        lo_thr, n_ge_lo = kth_largest(lo_ref, float(topk) - n_gt_hi, n_ge_hi - n_gt_hi)
        thr = hi_thr[0:1] * (1 << HALF_BITS) + (lo_thr[0:1] - I16_MIN)
        thr = jnp.maximum(thr, INT_MIN + 1)
        return thr, n_gt_hi + n_ge_lo - float(topk)

    thr, excess = lax.switch(qb, [functools.partial(threshold, n) for n in range(1, n_q + 1)])

    @pl.when(jnp.max(excess) > 0.0)
    def _():
        kr = lax.broadcasted_iota(I32, (K_TILE, K_TILE), 0)
        kc = lax.broadcasted_iota(I32, (K_TILE, K_TILE), 1)
        later = jnp.where(kc >= kr, 1.0, 0.0).astype(BF16)

        def tie_body(i, seen):
            c = n_kt - 1 - i
            keys = keys_ref[c]
            tie = keys == thr
            rank = _dot(later, jnp.where(tie, 1.0, 0.0).astype(BF16)) + seen
            keys_ref[c] = jnp.where(tie & (rank <= excess), thr - 1, keys)
            return rank[0:1]

        lax.fori_loop(0, n_kt, tie_body, jnp.zeros((1, Q_TILE), F32))

    mx_ref[...] = jnp.full(mx_ref.shape, NEG_INF, F32)

    def qk_tiles(c0, cnt):
        k0 = pl.multiple_of(c0 * K_TILE, K_TILE)
        lg = _dot_nt(qlat, ckv_ref[pl.ds(k0, cnt * K_TILE), :])
        for u in range(cnt):
            c = c0 + u
            mb = jnp.where(keys_ref[c] >= thr, 0.0, NEG_INF).T
            t_idx = jnp.minimum(qb - c, BIAS_CLASSES - 1)
            for hh in range(H_A):
                l = lg[hh * Q_TILE:(hh + 1) * Q_TILE, u * K_TILE:(u + 1) * K_TILE] + (mb + bt_ref[t_idx, hh])
                lg_ref[c, hh] = l
                mx_ref[hh] = jnp.maximum(mx_ref[hh], jnp.maximum(l[:, :half], l[:, half:]))

    for_tile_groups(qk_tiles)

    m = jnp.max(mx_ref[...], axis=-1, keepdims=True)
    ss_ref[...] = jnp.zeros(ss_ref.shape, F32)
    oacc_ref[...] = jnp.zeros(oacc_ref.shape, F32)

    def pv_tiles(c0, cnt):
        k0 = pl.multiple_of(c0 * K_TILE, K_TILE)
        ps = [jnp.exp2(lg_ref[c0 + u] - m) for u in range(cnt)]
        part = ps[0][:, :, :half] + ps[0][:, :, half:]
        for p in ps[1:]:
            part = part + (p[:, :, :half] + p[:, :, half:])
        ss_ref[...] += part
        p16 = jnp.concatenate([p.reshape(rows, K_TILE).astype(BF16) for p in ps], axis=1)
        oacc_ref[...] += _dot(p16, ckv_ref[pl.ds(k0, cnt * K_TILE), :])

    for_tile_groups(pv_tiles)
    inv_denom = 1.0 / jnp.sum(ss_ref[...], axis=-1, keepdims=True)

    for hh in range(H_A):
        oh = _dot(oacc_ref[hh * Q_TILE:(hh + 1) * Q_TILE, :].astype(BF16), wuv_ref[hh])
        zz = z_ref[:, hh * D_V:(hh + 1) * D_V]
        y_ref[:, hh * D_V:(hh + 1) * D_V] = (oh * inv_denom[hh] * (zz * _sigmoid(zz))).astype(BF16)


def _a_attn(qidx, wt, kidx, qlat, ckv, z, bt, wuv, batch, seq):
    n = batch * seq
    nq = seq // Q_TILE
    n_kt = seq // K_TILE
    topk = min(TOPK_MAX, seq // 4)
    grid = (batch, nq)
    kern = functools.partial(_a_attn_kernel, topk=topk, n_q=nq)
    return pl.pallas_call(
        kern,
        grid=grid,
        in_specs=[pl.BlockSpec((H_I, Q_TILE, D_I), lambda b, q: (0, b * nq + q, 0)),
                  pl.BlockSpec((H_I, Q_TILE), lambda b, q: (0, b * nq + q)),
                  pl.BlockSpec((seq, D_I), lambda b, q: (b, 0)),
                  pl.BlockSpec((H_A, Q_TILE, D_C), lambda b, q: (0, b * nq + q, 0)),
                  pl.BlockSpec((seq, D_C), lambda b, q: (b, 0)),
                  pl.BlockSpec((Q_TILE, H_A * D_V), lambda b, q: (b * nq + q, 0)),
                  pl.BlockSpec(bt.shape, lambda b, q: (0, 0, 0, 0), pipeline_mode=pl.Buffered(1)),
                  pl.BlockSpec(wuv.shape, lambda b, q: (0, 0, 0), pipeline_mode=pl.Buffered(1))],
        out_specs=pl.BlockSpec((Q_TILE, H_A * D_V), lambda b, q: (b * nq + q, 0)),
        out_shape=jax.ShapeDtypeStruct((n, H_A * D_V), BF16),
        scratch_shapes=[pltpu.VMEM((n_kt, K_TILE, Q_TILE), I32),
                        pltpu.VMEM((n_kt, K_TILE // PACKED_ROWS, PACKED_ROWS, Q_TILE), I16),
                        pltpu.VMEM((n_kt, K_TILE // PACKED_ROWS, PACKED_ROWS, Q_TILE), I16),
                        pltpu.VMEM((n_kt, H_A, Q_TILE, K_TILE), F32),
                        pltpu.VMEM((H_A, Q_TILE, K_TILE // 2), F32),
                        pltpu.VMEM((H_A, Q_TILE, K_TILE // 2), F32),
                        pltpu.VMEM((H_A * Q_TILE, D_C), F32)],
        compiler_params=_params("parallel", "arbitrary"),
        name="a_attn",
    )(qidx, wt, kidx, qlat, ckv, z, bt, wuv)


def _b_proj_kernel(x_ref, nw_ref, wqkv_ref, wba_ref, wbat_ref, wz_ref, cw_ref, alog_ref, dtb_ref, alogc_ref, dtbc_ref,
                   q_ref, k_ref, v_ref, beta_ref, g_ref, gt_ref, z_ref, buf_ref, *, tile):
    s = pl.program_id(1)
    h = _rms(x_ref[...], nw_ref[...]).astype(BF16)
    z_ref[...] = _dot(h, wz_ref[...])
    ba = _dot(h, wba_ref[...])
    beta_ref[...] = _sigmoid(ba[:, :H_B])
    g_ref[...] = -jnp.exp(alog_ref[...]) * _softplus(ba[:, H_B:2 * H_B] + dtb_ref[...])
    bat = _dot_nt(wbat_ref[...], h)
    gt_ref[...] = -jnp.exp(alogc_ref[...]) * _softplus(bat[H_B:, :] + dtbc_ref[...])

    @pl.when(s == 0)
    def _():
        buf_ref[0:8, :] = jnp.zeros((8, buf_ref.shape[1]), F32)

    width = H_B * D_K
    for sec, out_ref in enumerate((q_ref, k_ref, v_ref)):
        cols = slice(sec * width, (sec + 1) * width)
        pre = _dot(h, wqkv_ref[:, cols])
        buf_ref[8:8 + tile, cols] = pre
        acc = pre * cw_ref[CONV_W - 1:CONV_W, cols]
        for w in range(CONV_W - 1):
            acc = acc + buf_ref[8 - (CONV_W - 1) + w:8 - (CONV_W - 1) + w + tile, cols] * cw_ref[w:w + 1, cols]
        buf_ref[0:8, cols] = buf_ref[tile:tile + 8, cols]
        y = acc * _sigmoid(acc)
        if sec == 2:
            out_ref[...] = y
        else:
            scale = (D_K ** -0.5) if sec == 0 else 1.0
            for hh in range(H_B):
                yy = y[:, hh * D_K:(hh + 1) * D_K]
                nrm = lax.rsqrt(jnp.sum(yy * yy, axis=-1, keepdims=True) + EPS)
                out_ref[:, hh * D_K:(hh + 1) * D_K] = yy * (nrm * scale)


def _b_proj(x2, nw, wqkv, wba, wbat, wz, cw, alog, dtb, batch, seq, tile):
    n, d = x2.shape
    ns = seq // tile
    grid = (batch, ns)
    tok = lambda w: pl.BlockSpec((tile, w), lambda b, s: (b * ns + s, 0))
    width = H_B * D_K
    kern = functools.partial(_b_proj_kernel, tile=tile)
    return pl.pallas_call(
        kern,
        grid=grid,
        in_specs=[tok(d), _const_spec(nw.shape), _const_spec(wqkv.shape), _const_spec(wba.shape),
                  _const_spec(wbat.shape), _const_spec(wz.shape), _const_spec(cw.shape),
                  _const_spec((1, H_B)), _const_spec((1, H_B)), _const_spec((H_B, 1)), _const_spec((H_B, 1))],
        out_specs=[tok(width), tok(width), tok(width), tok(H_B), tok(H_B),
                   pl.BlockSpec((H_B, tile), lambda b, s: (0, b * ns + s)), tok(width)],
        out_shape=[jax.ShapeDtypeStruct((n, width), F32)] * 3
        + [jax.ShapeDtypeStruct((n, H_B), F32)] * 2
        + [jax.ShapeDtypeStruct((H_B, n), F32), jax.ShapeDtypeStruct((n, width), F32)],
        scratch_shapes=[pltpu.VMEM((tile + 8, 3 * width), F32)],
        compiler_params=_params("parallel", "arbitrary"),
        name="b_proj",
    )(x2, nw, wqkv, wba, wbat, wz, cw, alog.reshape(1, H_B), dtb.reshape(1, H_B),
      alog.reshape(H_B, 1), dtb.reshape(H_B, 1))


def _b_gdn_kernel(q_ref, k_ref, v_ref, z_ref, beta_ref, g_ref, gt_ref, go_ref, y_ref, state_ref):
    hb = pl.program_id(1)
    s = pl.program_id(2)
    t = GDN_TILE
    n_chunks = t // CHUNK
    log_c = int(math.log2(CHUNK))
    heads = range(GDN_HEADS)

    @pl.when(s == 0)
    def _():
        state_ref[...] = jnp.zeros(state_ref.shape, F32)

    r = lax.broadcasted_iota(I32, (t, t), 0)
    c = lax.broadcasted_iota(I32, (t, t), 1)
    xs = r ^ c
    same = xs < CHUNK
    incl = same & (r >= c)
    upper = same & (r <= c)
    eye = (r == c).astype(F32)
    level = [(lax.shift_right_logical(xs, lb) == 1) & ((r & (1 << lb)) != 0) for lb in range(log_c)]
    lane8 = lax.broadcasted_iota(I32, (t, H_B), 1)
    sub8 = lax.broadcasted_iota(I32, (H_B, t), 0)

    b_col, decay, eg_col, ekl_col, egl_col = [], [], [], [], []
    for j in heads:
        head = hb * GDN_HEADS + j
        g_col = jnp.sum(jnp.where(lane8 == head, g_ref[...], 0.0), axis=1, keepdims=True)
        b_col.append(jnp.sum(jnp.where(lane8 == head, beta_ref[...], 0.0), axis=1, keepdims=True))
        g_row = jnp.sum(jnp.where(sub8 == head, gt_ref[...], 0.0), axis=0, keepdims=True)
        g_rows = jnp.broadcast_to(g_row, (t, t))
        gc_col = jnp.sum(jnp.where(incl, g_rows, 0.0), axis=1, keepdims=True)
        gl_col = jnp.sum(jnp.where(same, g_rows, 0.0), axis=1, keepdims=True)
        gc_row = jnp.sum(jnp.where(upper, jnp.broadcast_to(g_col, (t, t)), 0.0), axis=0, keepdims=True)
        decay.append(jnp.where(incl, jnp.exp(jnp.minimum(gc_col - gc_row, 0.0)), 0.0))
        eg_col.append(jnp.exp(gc_col))
        ekl_col.append(jnp.exp(gl_col - gc_col))
        egl_col.append(jnp.exp(gl_col))

    lmat, aqk, rhs, q_dec, k_dec = [], [], [], [], []
    for j in heads:
        hs = slice(j * D_K, (j + 1) * D_K)
        qf, kf, vf = q_ref[:, hs], k_ref[:, hs], v_ref[:, hs]
        kb = kf * b_col[j]
        k16 = kf.astype(BF16)
        gram = _dot_nt(jnp.concatenate([kb.astype(BF16), qf.astype(BF16)], axis=0), k16)
        lmat.append(gram[:t] * decay[j])
        aqk.append(jnp.where(incl, gram[t:] * decay[j], 0.0).astype(BF16))
        rhs.append(jnp.concatenate([(vf * b_col[j]).astype(BF16), (kb * eg_col[j]).astype(BF16)], axis=1))
        q_dec.append(qf * eg_col[j])
        k_dec.append((kf * ekl_col[j]).astype(BF16))

    tinv = [eye - jnp.where(level[0], lmat[j], 0.0) for j in heads]
    for lb in range(1, log_c):
        half = 1 << lb
        t16 = [tinv[j].astype(BF16) for j in heads]
        if half < 8:
            y16 = [_dot(jnp.where(level[lb], lmat[j], 0.0).astype(BF16), t16[j]).astype(BF16) for j in heads]
            tinv = [tinv[j] - _dot(t16[j], y16[j]) for j in heads]
            continue
        pairs = t // (2 * half)
        split = lambda a: a.reshape(pairs, 2, half, t)
        lower = lambda a: split(a)[:, 1].reshape(t // 2, t)
        i = lax.broadcasted_iota(I32, (t // 2, t), 0)
        r_low = lax.shift_left(lax.shift_right_logical(i, lb), lb + 1) + half + (i & (half - 1))
        in_b = lax.shift_right_logical(r_low ^ lax.broadcasted_iota(I32, (t // 2, t), 1), lb) == 1
        y_low = [_dot(jnp.where(in_b, lower(lmat[j]), 0.0).astype(BF16), t16[j]) for j in heads]
        zeros = jnp.zeros((pairs, half, t), F32)
        y16 = [jnp.stack([zeros, y_low[j].reshape(pairs, half, t)], axis=1).reshape(t, t).astype(BF16) for j in heads]
        t_low = [lower(tinv[j]) for j in heads]
        t_low = [t_low[j] - _dot(t_low[j].astype(BF16), y16[j]) for j in heads]
        tinv = [jnp.stack([split(tinv[j])[:, 0], t_low[j].reshape(pairs, half, t)], axis=1).reshape(t, t)
                for j in heads]

    sol16 = [_dot(tinv[j].astype(BF16), rhs[j]).astype(BF16) for j in heads]
    aux = [_dot(aqk[j], sol16[j]) for j in heads]
    q_eff = [(q_dec[j] - aux[j][:, D_VB:]).astype(BF16) for j in heads]
    kw = [[_dot_tn(k_dec[j][ci * CHUNK:(ci + 1) * CHUNK], sol16[j][ci * CHUNK:(ci + 1) * CHUNK])
           for ci in range(n_chunks)] for j in heads]

    state = [state_ref[j] for j in heads]
    outs = [[] for _ in heads]
    for ci in range(n_chunks):
        rs = slice(ci * CHUNK, (ci + 1) * CHUNK)
        for j in heads:
            s16 = state[j].astype(BF16)
            both = _dot(jnp.concatenate([q_eff[j][rs], kw[j][ci][:, D_VB:].astype(BF16)], axis=0), s16)
            outs[j].append(both[:CHUNK] + aux[j][rs, :D_VB])
            state[j] = state[j] * egl_col[j][ci * CHUNK:ci * CHUNK + 1, :] + kw[j][ci][:, :D_VB] - both[CHUNK:]

    for j in heads:
        hs = slice(j * D_K, (j + 1) * D_K)
        state_ref[j] = state[j]
        o = jnp.concatenate(outs[j], axis=0)
        zz = z_ref[:, hs]
        y_ref[:, hs] = (_rms(o, go_ref[...]) * (zz * _sigmoid(zz))).astype(BF16)


def _b_gdn(q, k, v, z, beta, g, gt, go, batch, seq):
    n = batch * seq
    ns = seq // GDN_TILE
    nh = H_B // GDN_HEADS
    grid = (batch, nh, ns)
    wide = pl.BlockSpec((GDN_TILE, GDN_HEADS * D_K), lambda b, h, s: (b * ns + s, h))
    narrow = pl.BlockSpec((GDN_TILE, H_B), lambda b, h, s: (b * ns + s, 0))
    return pl.pallas_call(
        _b_gdn_kernel,
        grid=grid,
        in_specs=[wide, wide, wide, wide, narrow, narrow,
                  pl.BlockSpec((H_B, GDN_TILE), lambda b, h, s: (0, b * ns + s)),
                  _const_spec(go.shape)],
        out_specs=wide,
        out_shape=jax.ShapeDtypeStruct((n, H_B * D_VB), BF16),
        scratch_shapes=[pltpu.VMEM((GDN_HEADS, D_K, D_VB), F32)],
        compiler_params=_params("parallel", "parallel", "arbitrary"),
        name="b_gdn",
    )(q, k, v, z, beta, g, gt, go)


def _out_ple_kernel(x_ref, y_ref, p_ref, wout_ref, pn_ref, wgate_ref, wproj_ref, o_ref):
    x1 = x_ref[...] + _dot(y_ref[...], wout_ref[...])
    hn = _rms(x1, pn_ref[...]).astype(BF16)
    gate = _sigmoid(_dot(hn, wgate_ref[...]))
    o_ref[...] = x1 + gate * _dot(p_ref[...].astype(BF16), wproj_ref[...])


def _out_ple(x2, y, p2, layer, wout, pn, wgate, wproj, tile):
    n, d = x2.shape
    tok = lambda w: pl.BlockSpec((tile, w), lambda i: (i, 0))
    return pl.pallas_call(
        _out_ple_kernel,
        grid=(n // tile,),
        in_specs=[tok(d), tok(y.shape[1]),
                  pl.BlockSpec((tile, p2.shape[1]), lambda i: (layer * (n // tile) + i, 0)),
                  _const_spec(wout.shape), _const_spec(pn.shape),
                  _const_spec(wgate.shape), _const_spec(wproj.shape)],
        out_specs=tok(d),
        out_shape=jax.ShapeDtypeStruct((n, d), F32),
        compiler_params=_params("parallel"),
        name="out_ple",
    )(x2, y, p2, wout, pn, wgate, wproj)


def _t5_bucket(rel):
    max_exact = N_BUCKETS // 2
    rel = jnp.maximum(rel, 0)
    rel_f = jnp.maximum(rel, 1).astype(F32)
    log_ratio = jnp.log(rel_f / max_exact) / math.log(MAX_DISTANCE / max_exact)
    large = max_exact + (log_ratio * (N_BUCKETS - max_exact)).astype(I32)
    large = jnp.minimum(large, N_BUCKETS - 1)
    return jnp.where(rel < max_exact, rel, large)


def _bias_tiles(rel_bias):
    span = Q_TILE + K_TILE
    m = jnp.arange(span, dtype=I32)
    key_minus_query = jnp.where(m < K_TILE, m, m - span)
    scaled = rel_bias.astype(F32) * LOG2_E
    tiles = []
    for cls in range(BIAS_CLASSES):
        w = scaled[_t5_bucket(cls * Q_TILE - key_minus_query)].T
        skew = jnp.tile(w, (1, Q_TILE))[:, :Q_TILE * (span - 1)].reshape(H_A, Q_TILE, span - 1)
        tiles.append(skew[:, :, :K_TILE])
    return jnp.stack(tiles)


def kernel(x, p, norm_w, a_w_in, a_g_cq, a_w_uq, a_w_uk, a_g_q, a_g_kv, a_w_iq, a_w_uv, a_w_out, rel_bias, b_w_in, b_conv_w, b_a_log, b_dt_bias, b_g_o, b_w_out, ple_norm, ple_w_gate, ple_w_proj):
    batch, seq, d_model = x.shape
    depth = p.shape[0]
    n = batch * seq
    assert seq % K_TILE == 0 and seq % GDN_TILE == 0 and H_B % GDN_HEADS == 0
    x2 = x.reshape(n, d_model)
    p2 = p.reshape(depth * n, p.shape[-1])
    bt = _bias_tiles(rel_bias)
    row = lambda a: a.reshape(1, -1).astype(F32)
    n_lat = D_CQ + D_C + D_I
    b_qkv = H_B * (2 * D_K + D_VB)
    for i in range(depth):
        j = i // 2
        if i % 2 == 0:
            w_in = a_w_in[j]
            wlat = w_in[:, :n_lat].astype(BF16)
            wwt = w_in[:, n_lat:n_lat + H_I].T.astype(BF16)
            wz = w_in[:, n_lat + H_I:].astype(BF16)
            qlat, qidx, wt, ckv, kidx, z = _a_proj(
                x2, row(norm_w[i]), wlat, wwt, wz, row(a_g_cq[j]), row(a_g_kv[j]),
                a_w_uq[j].reshape(D_CQ, H_A * D_NOPE).astype(BF16), a_w_uk[j].astype(BF16), row(a_g_q[j]),
                a_w_iq[j].reshape(D_CQ, H_I * D_I).astype(BF16), tile=1024)
            y = _a_attn(qidx, wt, kidx, qlat, ckv, z, bt, a_w_uv[j].astype(BF16), batch, seq)
            w_out = a_w_out[j]
        else:
            w_in = b_w_in[j]
            wqkv = w_in[:, :b_qkv].astype(BF16)
            w_ba = w_in[:, b_qkv:b_qkv + 2 * H_B]
            wba = jnp.pad(w_ba, ((0, 0), (0, LANES - 2 * H_B))).astype(BF16)
            wz = w_in[:, b_qkv + 2 * H_B:].astype(BF16)
            q, k, v, beta, g, gt, z = _b_proj(
                x2, row(norm_w[i]), wqkv, wba, w_ba.T.astype(BF16), wz, b_conv_w[j].astype(F32),
                b_a_log[j].astype(F32), b_dt_bias[j].astype(F32), batch, seq, tile=512)
            y = _b_gdn(q, k, v, z, beta, g, gt, row(b_g_o[j]), batch, seq)
            w_out = b_w_out[j]
        x2 = _out_ple(x2, y, p2, i, w_out.astype(BF16), row(ple_norm[i]), ple_w_gate[i].astype(BF16),
                      ple_w_proj[i].astype(BF16), tile=1024)
    return x2.reshape(batch, seq, d_model)
```

```python
import functools
import math

import jax
import jax.numpy as jnp
from jax import lax
from jax.experimental import pallas as pl
from jax.experimental.pallas import tpu as pltpu

F32 = jnp.float32
BF16 = jnp.bfloat16
I32 = jnp.int32
I16 = jnp.int16

EPS = 1e-6
NEG_INF = -1e30
LOG2_E = math.log2(math.e)
INT_MIN = -(2 ** 31)
HALF_BITS = 16
I16_MIN = -(1 << (HALF_BITS - 1))

LANES = 128
PACKED_ROWS = 16
VMEM_LIMIT_BYTES = 56 * 1024 * 1024

H_A = 8
D_C = 256
D_CQ = 256
D_I = 128
H_I = 8
D_NOPE = 128
D_V = 128
TOPK_MAX = 256
N_BUCKETS = 32
MAX_DISTANCE = 128
Q_TILE = 256
K_TILE = 256
BIAS_CLASSES = -(-(MAX_DISTANCE + K_TILE - 1) // Q_TILE) + 1
H_B = 8
D_K = 128
D_VB = 128
CONV_W = 4
CHUNK = 64
GDN_TILE = 256
GDN_HEADS = 8


def _dot(a, b):
    return jnp.dot(a, b, preferred_element_type=F32)


def _dot_nt(a, b):
    return lax.dot_general(a, b, (((1,), (1,)), ((), ())), preferred_element_type=F32)


def _dot_tn(a, b):
    return lax.dot_general(a, b, (((0,), (0,)), ((), ())), preferred_element_type=F32)


def _rms(x, gain=None):
    y = x * lax.rsqrt(jnp.mean(x * x, axis=-1, keepdims=True) + EPS)
    return y if gain is None else y * gain


def _sigmoid(x):
    return 1.0 / (1.0 + jnp.exp2(x * -LOG2_E))


def _softplus(x):
    return jnp.maximum(x, 0.0) + jnp.log1p(jnp.exp(-jnp.abs(x)))


def _params(*semantics):
    return pltpu.CompilerParams(dimension_semantics=semantics, vmem_limit_bytes=VMEM_LIMIT_BYTES)


def _const_spec(shape):
    nd = len(shape)
    return pl.BlockSpec(shape, lambda *_: (0,) * nd)


def _a_proj_kernel(x_ref, nw_ref, wlat_ref, wz_ref, gcq_ref, gkv_ref, wuq_ref, wuk_ref, gq_ref, wiq_ref,
                   qlat_ref, qidx_ref, wt_ref, ckv_ref, kidx_ref, z_ref):
    h = _rms(x_ref[...], nw_ref[...]).astype(BF16)
    lat = _dot(h, wlat_ref[...])
    cq = _rms(lat[:, :D_CQ], gcq_ref[...]).astype(BF16)
    ckv_ref[...] = _rms(lat[:, D_CQ:D_CQ + D_C], gkv_ref[...]).astype(BF16)
    n_lat = D_CQ + D_C + D_I
    kidx_ref[...] = _rms(lat[:, D_CQ + D_C:n_lat]).astype(BF16)
    wt_ref[...] = lat[:, n_lat:].T[:H_I] * (H_I ** -0.5)
    z_ref[...] = _dot(h, wz_ref[...])
    qn = _dot(cq, wuq_ref[...]).astype(BF16)
    for hh in range(H_A):
        ql = _dot(qn[:, hh * D_NOPE:(hh + 1) * D_NOPE], wuk_ref[hh])
        qlat_ref[hh] = (_rms(ql, gq_ref[...]) * (D_C ** -0.5 * LOG2_E)).astype(BF16)
    qi = _dot(cq, wiq_ref[...]) * (D_I ** -0.5)
    for hh in range(H_I):
        qidx_ref[hh] = qi[:, hh * D_I:(hh + 1) * D_I].astype(BF16)


def _a_proj(x2, nw, wlat, wz, gcq, gkv, wuq, wuk, gq, wiq, tile):
    n, d = x2.shape
    grid = (n // tile,)
    tok = lambda w: pl.BlockSpec((tile, w), lambda i: (i, 0))
    return pl.pallas_call(
        _a_proj_kernel,
        grid=grid,
        in_specs=[tok(d), _const_spec(nw.shape), _const_spec(wlat.shape),
                  _const_spec(wz.shape), _const_spec(gcq.shape), _const_spec(gkv.shape), _const_spec(wuq.shape),
                  _const_spec(wuk.shape), _const_spec(gq.shape), _const_spec(wiq.shape)],
        out_specs=[pl.BlockSpec((H_A, tile, D_C), lambda i: (0, i, 0)),
                   pl.BlockSpec((H_I, tile, D_I), lambda i: (0, i, 0)),
                   pl.BlockSpec((H_I, tile), lambda i: (0, i)),
                   tok(D_C), tok(D_I), tok(H_A * D_V)],
        out_shape=[jax.ShapeDtypeStruct((H_A, n, D_C), BF16),
                   jax.ShapeDtypeStruct((H_I, n, D_I), BF16),
                   jax.ShapeDtypeStruct((H_I, n), F32),
                   jax.ShapeDtypeStruct((n, D_C), BF16),
                   jax.ShapeDtypeStruct((n, D_I), BF16),
                   jax.ShapeDtypeStruct((n, H_A * D_V), F32)],
        compiler_params=_params("parallel"),
        name="a_proj",
    )(x2, nw, wlat, wz, gcq, gkv, wuq, wuk, gq, wiq)


def _a_attn_kernel(qidx_ref, wt_ref, kidx_ref, qlat_ref, ckv_ref, z_ref, bt_ref, wuv_ref,
                   y_ref, keys_ref, hi_ref, lo_ref, lg_ref, mx_ref, ss_ref, oacc_ref, *, topk, n_q):
    qb = pl.program_id(1)
    n_kt = qb + 1
    rows = H_A * Q_TILE
    half = K_TILE // 2
    qidx = qidx_ref[...].reshape(H_I * Q_TILE, D_I)
    qlat = qlat_ref[...].reshape(rows, D_C)

    def for_tile_groups(body):
        def pair(i, carry):
            body(2 * i, 2)
            return carry

        lax.fori_loop(0, lax.shift_right_logical(n_kt, 1), pair, 0)

        @pl.when((n_kt & 1) == 1)
        def _():
            body(n_kt - 1, 1)

    def idx_tiles(c0, cnt):
        k0 = pl.multiple_of(c0 * K_TILE, K_TILE)
        lg = _dot_nt(kidx_ref[pl.ds(k0, cnt * K_TILE), :], qidx)
        sc = jnp.zeros((cnt * K_TILE, Q_TILE), F32)
        for hh in range(H_I):
            sc = sc + jnp.maximum(lg[:, hh * Q_TILE:(hh + 1) * Q_TILE], 0.0) * wt_ref[hh:hh + 1, :]
        bits = lax.bitcast_convert_type(sc, I32)
        sign = lax.shift_right_arithmetic(bits, 31)
        skey = (bits ^ (sign & 0x7FFFFFFF)) - sign
        key_pos = k0 + lax.broadcasted_iota(I32, (cnt * K_TILE, Q_TILE), 0)
        q_pos = qb * Q_TILE + lax.broadcasted_iota(I32, (cnt * K_TILE, Q_TILE), 1)
        skey = jnp.where(key_pos <= q_pos, skey, INT_MIN)
        hi = lax.shift_right_arithmetic(skey, HALF_BITS).astype(I16)
        lo = ((skey & ((1 << HALF_BITS) - 1)) + I16_MIN).astype(I16)
        for u in range(cnt):
            ks = slice(u * K_TILE, (u + 1) * K_TILE)
            keys_ref[c0 + u] = skey[ks]
            hi_ref[c0 + u] = hi[ks].reshape(K_TILE // PACKED_ROWS, PACKED_ROWS, Q_TILE)
            lo_ref[c0 + u] = lo[ks].reshape(K_TILE // PACKED_ROWS, PACKED_ROWS, Q_TILE)

    for_tile_groups(idx_tiles)

    one = jnp.ones((), BF16)
    zero = jnp.zeros((), BF16)

    def threshold(n_tiles):
        def count(ref16, bound16, strict):
            acc = jnp.zeros((PACKED_ROWS, Q_TILE), BF16)
            for c in range(n_tiles):
                tile = ref16[c]
                hit = jnp.where(tile > bound16[None] if strict else tile >= bound16[None], one, zero)
                parts = [hit[u] for u in range(K_TILE // PACKED_ROWS)]
                while len(parts) > 1:
                    parts = [parts[u] + parts[u + 1] for u in range(0, len(parts), 2)]
                acc = acc + parts[0]
            return jnp.sum(acc.astype(F32), axis=0, keepdims=True)

        def kth_largest(ref16, need, n_all):
            def bit_body(i, carry):
                thr, n_ge = carry
                cand = thr + lax.shift_left(jnp.int32(1), HALF_BITS - 1 - i)
                n_cand = count(ref16, cand.astype(I16), False)
                ok = n_cand >= need
                return jnp.where(ok, cand, thr), jnp.where(ok, n_cand, n_ge)

            return lax.fori_loop(0, HALF_BITS, bit_body, (jnp.full((PACKED_ROWS, Q_TILE), I16_MIN, I32), n_all))

        n_all = jnp.full((1, Q_TILE), float(n_tiles * K_TILE), F32)
        hi_thr, n_ge_hi = kth_largest(hi_ref, float(topk), n_all)
        hi_thr16 = hi_thr.astype(I16)
        n_gt_hi = count(hi_ref, hi_thr16, True)
        for c in range(n_tiles):
            lo_ref[c] = jnp.where(hi_ref[c] == hi_thr16[None], lo_ref[c], jnp.int16(I16_MIN))
        lo_thr, n_ge_lo = kth_largest(lo_ref, float(topk) - n_gt_hi, n_ge_hi - n_gt_hi)
        thr = hi_thr[0:1] * (1 << HALF_BITS) + (lo_thr[0:1] - I16_MIN)
        thr = jnp.maximum(thr, INT_MIN + 1)
        return thr, n_gt_hi + n_ge_lo - float(topk)

    thr, excess = lax.switch(qb, [functools.partial(threshold, n) for n in range(1, n_q + 1)])

    @pl.when(jnp.max(excess) > 0.0)
    def _():
        kr = lax.broadcasted_iota(I32, (K_TILE, K_TILE), 0)
        kc = lax.broadcasted_iota(I32, (K_TILE, K_TILE), 1)
        later = jnp.where(kc >= kr, 1.0, 0.0).astype(BF16)

        def tie_body(i, seen):
            c = n_kt - 1 - i
            keys = keys_ref[c]
            tie = keys == thr
            rank = _dot(later, jnp.where(tie, 1.0, 0.0).astype(BF16)) + seen
            keys_ref[c] = jnp.where(tie & (rank <= excess), thr - 1, keys)
            return rank[0:1]

        lax.fori_loop(0, n_kt, tie_body, jnp.zeros((1, Q_TILE), F32))

    mx_ref[...] = jnp.full(mx_ref.shape, NEG_INF, F32)

    def qk_tiles(c0, cnt):
        k0 = pl.multiple_of(c0 * K_TILE, K_TILE)
        lg = _dot_nt(qlat, ckv_ref[pl.ds(k0, cnt * K_TILE), :])
        for u in range(cnt):
            c = c0 + u
            mb = jnp.where(keys_ref[c] >= thr, 0.0, NEG_INF).T
            t_idx = jnp.minimum(qb - c, BIAS_CLASSES - 1)
            for hh in range(H_A):
                l = lg[hh * Q_TILE:(hh + 1) * Q_TILE, u * K_TILE:(u + 1) * K_TILE] + (mb + bt_ref[t_idx, hh])
                lg_ref[c, hh] = l
                mx_ref[hh] = jnp.maximum(mx_ref[hh], jnp.maximum(l[:, :half], l[:, half:]))

    for_tile_groups(qk_tiles)

    m = jnp.max(mx_ref[...], axis=-1, keepdims=True)
    ss_ref[...] = jnp.zeros(ss_ref.shape, F32)
    oacc_ref[...] = jnp.zeros(oacc_ref.shape, F32)

    def pv_tiles(c0, cnt):
        k0 = pl.multiple_of(c0 * K_TILE, K_TILE)
        ps = [jnp.exp2(lg_ref[c0 + u] - m) for u in range(cnt)]
        part = ps[0][:, :, :half] + ps[0][:, :, half:]
        for p in ps[1:]:
            part = part + (p[:, :, :half] + p[:, :, half:])
        ss_ref[...] += part
        p16 = jnp.concatenate([p.reshape(rows, K_TILE).astype(BF16) for p in ps], axis=1)
        oacc_ref[...] += _dot(p16, ckv_ref[pl.ds(k0, cnt * K_TILE), :])

    for_tile_groups(pv_tiles)
    inv_denom = 1.0 / jnp.sum(ss_ref[...], axis=-1, keepdims=True)

    for hh in range(H_A):
        oh = _dot(oacc_ref[hh * Q_TILE:(hh + 1) * Q_TILE, :].astype(BF16), wuv_ref[hh])
        zz = z_ref[:, hh * D_V:(hh + 1) * D_V]
        y_ref[:, hh * D_V:(hh + 1) * D_V] = (oh * inv_denom[hh] * (zz * _sigmoid(zz))).astype(BF16)


def _a_attn(qidx, wt, kidx, qlat, ckv, z, bt, wuv, batch, seq):
    n = batch * seq
    nq = seq // Q_TILE
    n_kt = seq // K_TILE
    topk = min(TOPK_MAX, seq // 4)
    grid = (batch, nq)
    kern = functools.partial(_a_attn_kernel, topk=topk, n_q=nq)
    return pl.pallas_call(
        kern,
        grid=grid,
        in_specs=[pl.BlockSpec((H_I, Q_TILE, D_I), lambda b, q: (0, b * nq + q, 0)),
                  pl.BlockSpec((H_I, Q_TILE), lambda b, q: (0, b * nq + q)),
                  pl.BlockSpec((seq, D_I), lambda b, q: (b, 0)),
                  pl.BlockSpec((H_A, Q_TILE, D_C), lambda b, q: (0, b * nq + q, 0)),
                  pl.BlockSpec((seq, D_C), lambda b, q: (b, 0)),
                  pl.BlockSpec((Q_TILE, H_A * D_V), lambda b, q: (b * nq + q, 0)),
                  pl.BlockSpec(bt.shape, lambda b, q: (0, 0, 0, 0), pipeline_mode=pl.Buffered(1)),
                  pl.BlockSpec(wuv.shape, lambda b, q: (0, 0, 0), pipeline_mode=pl.Buffered(1))],
        out_specs=pl.BlockSpec((Q_TILE, H_A * D_V), lambda b, q: (b * nq + q, 0)),
        out_shape=jax.ShapeDtypeStruct((n, H_A * D_V), BF16),
        scratch_shapes=[pltpu.VMEM((n_kt, K_TILE, Q_TILE), I32),
                        pltpu.VMEM((n_kt, K_TILE // PACKED_ROWS, PACKED_ROWS, Q_TILE), I16),
                        pltpu.VMEM((n_kt, K_TILE // PACKED_ROWS, PACKED_ROWS, Q_TILE), I16),
                        pltpu.VMEM((n_kt, H_A, Q_TILE, K_TILE), F32),
                        pltpu.VMEM((H_A, Q_TILE, K_TILE // 2), F32),
                        pltpu.VMEM((H_A, Q_TILE, K_TILE // 2), F32),
                        pltpu.VMEM((H_A * Q_TILE, D_C), F32)],
        compiler_params=_params("parallel", "arbitrary"),
        name="a_attn",
    )(qidx, wt, kidx, qlat, ckv, z, bt, wuv)


def _b_proj_kernel(x_ref, nw_ref, wqkv_ref, wba_ref, wz_ref, cw_ref, alog_ref, dtb_ref, alogc_ref, dtbc_ref,
                   q_ref, k_ref, v_ref, beta_ref, g_ref, gt_ref, z_ref, buf_ref, *, tile):
    s = pl.program_id(1)
    h = _rms(x_ref[...], nw_ref[...]).astype(BF16)
    z_ref[...] = _dot(h, wz_ref[...])
    ba = _dot(h, wba_ref[...])
    beta_ref[...] = _sigmoid(ba[:, :H_B])
    g_ref[...] = -jnp.exp(alog_ref[...]) * _softplus(ba[:, H_B:2 * H_B] + dtb_ref[...])
    a_rows = ba.T[H_B:2 * H_B]
    gt_ref[...] = -jnp.exp(alogc_ref[...]) * _softplus(a_rows + dtbc_ref[...])

    @pl.when(s == 0)
    def _():
        buf_ref[0:8, :] = jnp.zeros((8, buf_ref.shape[1]), F32)

    width = H_B * D_K
    for sec, out_ref in enumerate((q_ref, k_ref, v_ref)):
        cols = slice(sec * width, (sec + 1) * width)
        pre = _dot(h, wqkv_ref[:, cols])
        buf_ref[8:8 + tile, cols] = pre
        acc = pre * cw_ref[CONV_W - 1:CONV_W, cols]
        for w in range(CONV_W - 1):
            acc = acc + buf_ref[8 - (CONV_W - 1) + w:8 - (CONV_W - 1) + w + tile, cols] * cw_ref[w:w + 1, cols]
        buf_ref[0:8, cols] = buf_ref[tile:tile + 8, cols]
        y = acc * _sigmoid(acc)
        if sec == 2:
            out_ref[...] = y
        else:
            scale = (D_K ** -0.5) if sec == 0 else 1.0
            for hh in range(H_B):
                yy = y[:, hh * D_K:(hh + 1) * D_K]
                nrm = lax.rsqrt(jnp.sum(yy * yy, axis=-1, keepdims=True) + EPS)
                out_ref[:, hh * D_K:(hh + 1) * D_K] = yy * (nrm * scale)


def _b_proj(x2, nw, wqkv, wba, wz, cw, alog, dtb, batch, seq, tile):
    n, d = x2.shape
    ns = seq // tile
    grid = (batch, ns)
    tok = lambda w: pl.BlockSpec((tile, w), lambda b, s: (b * ns + s, 0))
    width = H_B * D_K
    kern = functools.partial(_b_proj_kernel, tile=tile)
    return pl.pallas_call(
        kern,
        grid=grid,
        in_specs=[tok(d), _const_spec(nw.shape), _const_spec(wqkv.shape), _const_spec(wba.shape),
                  _const_spec(wz.shape), _const_spec(cw.shape),
                  _const_spec((1, H_B)), _const_spec((1, H_B)), _const_spec((H_B, 1)), _const_spec((H_B, 1))],
        out_specs=[tok(width), tok(width), tok(width), tok(H_B), tok(H_B),
                   pl.BlockSpec((H_B, tile), lambda b, s: (0, b * ns + s)), tok(width)],
        out_shape=[jax.ShapeDtypeStruct((n, width), F32)] * 3
        + [jax.ShapeDtypeStruct((n, H_B), F32)] * 2
        + [jax.ShapeDtypeStruct((H_B, n), F32), jax.ShapeDtypeStruct((n, width), F32)],
        scratch_shapes=[pltpu.VMEM((tile + 8, 3 * width), F32)],
        compiler_params=_params("parallel", "arbitrary"),
        name="b_proj",
    )(x2, nw, wqkv, wba, wz, cw, alog.reshape(1, H_B), dtb.reshape(1, H_B),
      alog.reshape(H_B, 1), dtb.reshape(H_B, 1))


def _b_gdn_kernel(q_ref, k_ref, v_ref, z_ref, beta_ref, g_ref, gt_ref, go_ref, y_ref, state_ref):
    hb = pl.program_id(1)
    s = pl.program_id(2)
    t = GDN_TILE
    n_chunks = t // CHUNK
    log_c = int(math.log2(CHUNK))
    heads = range(GDN_HEADS)

    @pl.when(s == 0)
    def _():
        state_ref[...] = jnp.zeros(state_ref.shape, F32)

    r = lax.broadcasted_iota(I32, (t, t), 0)
    c = lax.broadcasted_iota(I32, (t, t), 1)
    xs = r ^ c
    same = xs < CHUNK
    incl = same & (r >= c)
    upper = same & (r <= c)
    eye = (r == c).astype(F32)
    level = [(lax.shift_right_logical(xs, lb) == 1) & ((r & (1 << lb)) != 0) for lb in range(log_c)]
    lane8 = lax.broadcasted_iota(I32, (t, H_B), 1)
    sub8 = lax.broadcasted_iota(I32, (H_B, t), 0)

    b_col, decay, eg_col, ekl_col, egl_col = [], [], [], [], []
    for j in heads:
        head = hb * GDN_HEADS + j
        g_col = jnp.sum(jnp.where(lane8 == head, g_ref[...], 0.0), axis=1, keepdims=True)
        b_col.append(jnp.sum(jnp.where(lane8 == head, beta_ref[...], 0.0), axis=1, keepdims=True))
        g_row = jnp.sum(jnp.where(sub8 == head, gt_ref[...], 0.0), axis=0, keepdims=True)
        g_rows = jnp.broadcast_to(g_row, (t, t))
        gc_col = jnp.sum(jnp.where(incl, g_rows, 0.0), axis=1, keepdims=True)
        gl_col = jnp.sum(jnp.where(same, g_rows, 0.0), axis=1, keepdims=True)
        gc_row = jnp.sum(jnp.where(upper, jnp.broadcast_to(g_col, (t, t)), 0.0), axis=0, keepdims=True)
        decay.append(jnp.where(incl, jnp.exp(jnp.minimum(gc_col - gc_row, 0.0)), 0.0))
        eg_col.append(jnp.exp(gc_col))
        ekl_col.append(jnp.exp(gl_col - gc_col))
        egl_col.append(jnp.exp(gl_col))

    lmat, aqk, rhs, q_dec, k_dec = [], [], [], [], []
    for j in heads:
        hs = slice(j * D_K, (j + 1) * D_K)
        qf, kf, vf = q_ref[:, hs], k_ref[:, hs], v_ref[:, hs]
        kb = kf * b_col[j]
        k16 = kf.astype(BF16)
        gram = _dot_nt(jnp.concatenate([kb.astype(BF16), qf.astype(BF16)], axis=0), k16)
        lmat.append(gram[:t] * decay[j])
        aqk.append(jnp.where(incl, gram[t:] * decay[j], 0.0).astype(BF16))
        rhs.append(jnp.concatenate([(vf * b_col[j]).astype(BF16), (kb * eg_col[j]).astype(BF16)], axis=1))
        q_dec.append(qf * eg_col[j])
        k_dec.append((kf * ekl_col[j]).astype(BF16))

    tinv = [eye - jnp.where(level[0], lmat[j], 0.0) for j in heads]
    for lb in range(1, log_c):
        half = 1 << lb
        t16 = [tinv[j].astype(BF16) for j in heads]
        if half < 8:
            y16 = [_dot(jnp.where(level[lb], lmat[j], 0.0).astype(BF16), t16[j]).astype(BF16) for j in heads]
            tinv = [tinv[j] - _dot(t16[j], y16[j]) for j in heads]
            continue
        pairs = t // (2 * half)
        split = lambda a: a.reshape(pairs, 2, half, t)
        lower = lambda a: split(a)[:, 1].reshape(t // 2, t)
        i = lax.broadcasted_iota(I32, (t // 2, t), 0)
        r_low = lax.shift_left(lax.shift_right_logical(i, lb), lb + 1) + half + (i & (half - 1))
        in_b = lax.shift_right_logical(r_low ^ lax.broadcasted_iota(I32, (t // 2, t), 1), lb) == 1
        y_low = [_dot(jnp.where(in_b, lower(lmat[j]), 0.0).astype(BF16), t16[j]) for j in heads]
        zeros = jnp.zeros((pairs, half, t), F32)
        y16 = [jnp.stack([zeros, y_low[j].reshape(pairs, half, t)], axis=1).reshape(t, t).astype(BF16) for j in heads]
        t_low = [lower(tinv[j]) for j in heads]
        t_low = [t_low[j] - _dot(t_low[j].astype(BF16), y16[j]) for j in heads]
        tinv = [jnp.stack([split(tinv[j])[:, 0], t_low[j].reshape(pairs, half, t)], axis=1).reshape(t, t)
                for j in heads]

    sol16 = [_dot(tinv[j].astype(BF16), rhs[j]).astype(BF16) for j in heads]
    aux = [_dot(aqk[j], sol16[j]) for j in heads]
    q_eff = [(q_dec[j] - aux[j][:, D_VB:]).astype(BF16) for j in heads]
    kw = [[_dot_tn(k_dec[j][ci * CHUNK:(ci + 1) * CHUNK], sol16[j][ci * CHUNK:(ci + 1) * CHUNK])
           for ci in range(n_chunks)] for j in heads]

    state = [state_ref[j] for j in heads]
    outs = [[] for _ in heads]
    for ci in range(n_chunks):
        rs = slice(ci * CHUNK, (ci + 1) * CHUNK)
        for j in heads:
            s16 = state[j].astype(BF16)
            both = _dot(jnp.concatenate([q_eff[j][rs], kw[j][ci][:, D_VB:].astype(BF16)], axis=0), s16)
            outs[j].append(both[:CHUNK] + aux[j][rs, :D_VB])
            state[j] = state[j] * egl_col[j][ci * CHUNK:ci * CHUNK + 1, :] + kw[j][ci][:, :D_VB] - both[CHUNK:]

    for j in heads:
        hs = slice(j * D_K, (j + 1) * D_K)
        state_ref[j] = state[j]
        o = jnp.concatenate(outs[j], axis=0)
        zz = z_ref[:, hs]
        y_ref[:, hs] = (_rms(o, go_ref[...]) * (zz * _sigmoid(zz))).astype(BF16)


def _b_gdn(q, k, v, z, beta, g, gt, go, batch, seq):
    n = batch * seq
    ns = seq // GDN_TILE
    nh = H_B // GDN_HEADS
    grid = (batch, nh, ns)
    wide = pl.BlockSpec((GDN_TILE, GDN_HEADS * D_K), lambda b, h, s: (b * ns + s, h))
    narrow = pl.BlockSpec((GDN_TILE, H_B), lambda b, h, s: (b * ns + s, 0))
    return pl.pallas_call(
        _b_gdn_kernel,
        grid=grid,
        in_specs=[wide, wide, wide, wide, narrow, narrow,
                  pl.BlockSpec((H_B, GDN_TILE), lambda b, h, s: (0, b * ns + s)),
                  _const_spec(go.shape)],
        out_specs=wide,
        out_shape=jax.ShapeDtypeStruct((n, H_B * D_VB), BF16),
        scratch_shapes=[pltpu.VMEM((GDN_HEADS, D_K, D_VB), F32)],
        compiler_params=_params("parallel", "parallel", "arbitrary"),
        name="b_gdn",
    )(q, k, v, z, beta, g, gt, go)


def _out_ple_kernel(x_ref, y_ref, p_ref, wout_ref, pn_ref, wgate_ref, wproj_ref, o_ref):
    x1 = x_ref[...] + _dot(y_ref[...], wout_ref[...])
    hn = _rms(x1, pn_ref[...]).astype(BF16)
    gate = _sigmoid(_dot(hn, wgate_ref[...]))
    o_ref[...] = x1 + gate * _dot(p_ref[...].astype(BF16), wproj_ref[...])


def _out_ple(x2, y, p2, layer, wout, pn, wgate, wproj, tile):
    n, d = x2.shape
    tok = lambda w: pl.BlockSpec((tile, w), lambda i: (i, 0))
    return pl.pallas_call(
        _out_ple_kernel,
        grid=(n // tile,),
        in_specs=[tok(d), tok(y.shape[1]),
                  pl.BlockSpec((tile, p2.shape[1]), lambda i: (layer * (n // tile) + i, 0)),
                  _const_spec(wout.shape), _const_spec(pn.shape),
                  _const_spec(wgate.shape), _const_spec(wproj.shape)],
        out_specs=tok(d),
        out_shape=jax.ShapeDtypeStruct((n, d), F32),
        compiler_params=_params("parallel"),
        name="out_ple",
    )(x2, y, p2, wout, pn, wgate, wproj)


def _t5_bucket(rel):
    max_exact = N_BUCKETS // 2
    rel = jnp.maximum(rel, 0)
    rel_f = jnp.maximum(rel, 1).astype(F32)
    log_ratio = jnp.log(rel_f / max_exact) / math.log(MAX_DISTANCE / max_exact)
    large = max_exact + (log_ratio * (N_BUCKETS - max_exact)).astype(I32)
    large = jnp.minimum(large, N_BUCKETS - 1)
    return jnp.where(rel < max_exact, rel, large)


def _bias_tiles(rel_bias):
    span = Q_TILE + K_TILE
    m = jnp.arange(span, dtype=I32)
    key_minus_query = jnp.where(m < K_TILE, m, m - span)
    scaled = rel_bias.astype(F32) * LOG2_E
    tiles = []
    for cls in range(BIAS_CLASSES):
        w = scaled[_t5_bucket(cls * Q_TILE - key_minus_query)].T
        skew = jnp.tile(w, (1, Q_TILE))[:, :Q_TILE * (span - 1)].reshape(H_A, Q_TILE, span - 1)
        tiles.append(skew[:, :, :K_TILE])
    return jnp.stack(tiles)


def kernel(x, p, norm_w, a_w_in, a_g_cq, a_w_uq, a_w_uk, a_g_q, a_g_kv, a_w_iq, a_w_uv, a_w_out, rel_bias, b_w_in, b_conv_w, b_a_log, b_dt_bias, b_g_o, b_w_out, ple_norm, ple_w_gate, ple_w_proj):
    batch, seq, d_model = x.shape
    depth = p.shape[0]
    n = batch * seq
    assert seq % K_TILE == 0 and seq % GDN_TILE == 0 and H_B % GDN_HEADS == 0
    x2 = x.reshape(n, d_model)
    p2 = p.reshape(depth * n, p.shape[-1])
    bt = _bias_tiles(rel_bias)
    row = lambda a: a.reshape(1, -1).astype(F32)
    n_lat = D_CQ + D_C + D_I
    b_qkv = H_B * (2 * D_K + D_VB)
    for i in range(depth):
        j = i // 2
        if i % 2 == 0:
            w_in = a_w_in[j]
            wlat = jnp.pad(w_in[:, :n_lat + H_I], ((0, 0), (0, LANES - H_I))).astype(BF16)
            wz = w_in[:, n_lat + H_I:].astype(BF16)
            qlat, qidx, wt, ckv, kidx, z = _a_proj(
                x2, row(norm_w[i]), wlat, wz, row(a_g_cq[j]), row(a_g_kv[j]),
                a_w_uq[j].reshape(D_CQ, H_A * D_NOPE).astype(BF16), a_w_uk[j].astype(BF16), row(a_g_q[j]),
                a_w_iq[j].reshape(D_CQ, H_I * D_I).astype(BF16), tile=1024)
            y = _a_attn(qidx, wt, kidx, qlat, ckv, z, bt, a_w_uv[j].astype(BF16), batch, seq)
            w_out = a_w_out[j]
        else:
            w_in = b_w_in[j]
            wqkv = w_in[:, :b_qkv].astype(BF16)
            w_ba = w_in[:, b_qkv:b_qkv + 2 * H_B]
            wba = jnp.pad(w_ba, ((0, 0), (0, LANES - 2 * H_B))).astype(BF16)
            wz = w_in[:, b_qkv + 2 * H_B:].astype(BF16)
            q, k, v, beta, g, gt, z = _b_proj(
                x2, row(norm_w[i]), wqkv, wba, wz, b_conv_w[j].astype(F32),
                b_a_log[j].astype(F32), b_dt_bias[j].astype(F32), batch, seq, tile=512)
            y = _b_gdn(q, k, v, z, beta, g, gt, row(b_g_o[j]), batch, seq)
            w_out = b_w_out[j]
        x2 = _out_ple(x2, y, p2, i, w_out.astype(BF16), row(ple_norm[i]), ple_w_gate[i].astype(BF16),
                      ple_w_proj[i].astype(BF16), tile=1024)
    return x2.reshape(batch, seq, d_model)
```

```python
import functools
import math

import jax
import jax.numpy as jnp
from jax import lax
from jax.experimental import pallas as pl
from jax.experimental.pallas import tpu as pltpu

F32 = jnp.float32
BF16 = jnp.bfloat16
I32 = jnp.int32
I16 = jnp.int16

EPS = 1e-6
NEG_INF = -1e30
LOG2_E = math.log2(math.e)
INT_MIN = -(2 ** 31)
HALF_BITS = 16
I16_MIN = -(1 << (HALF_BITS - 1))

LANES = 128
PACKED_ROWS = 16
VMEM_LIMIT_BYTES = 56 * 1024 * 1024

H_A = 8
D_C = 256
D_CQ = 256
D_I = 128
H_I = 8
D_NOPE = 128
D_V = 128
TOPK_MAX = 256
N_BUCKETS = 32
MAX_DISTANCE = 128
Q_TILE = 256
K_TILE = 256
BIAS_CLASSES = -(-(MAX_DISTANCE + K_TILE - 1) // Q_TILE) + 1
H_B = 8
D_K = 128
D_VB = 128
CONV_W = 4
CHUNK = 64
GDN_TILE = 256
GDN_HEADS = 8


def _dot(a, b):
    return jnp.dot(a, b, preferred_element_type=F32)


def _dot_nt(a, b):
    return lax.dot_general(a, b, (((1,), (1,)), ((), ())), preferred_element_type=F32)


def _dot_tn(a, b):
    return lax.dot_general(a, b, (((0,), (0,)), ((), ())), preferred_element_type=F32)


def _rms(x, gain=None):
    y = x * lax.rsqrt(jnp.mean(x * x, axis=-1, keepdims=True) + EPS)
    return y if gain is None else y * gain


def _sigmoid(x):
    return 1.0 / (1.0 + jnp.exp2(x * -LOG2_E))


def _softplus(x):
    return jnp.maximum(x, 0.0) + jnp.log1p(jnp.exp(-jnp.abs(x)))


def _params(*semantics):
    return pltpu.CompilerParams(dimension_semantics=semantics, vmem_limit_bytes=VMEM_LIMIT_BYTES)


def _const_spec(shape):
    nd = len(shape)
    return pl.BlockSpec(shape, lambda *_: (0,) * nd)


def _a_proj_kernel(x_ref, nw_ref, wlat_ref, wz_ref, gcq_ref, gkv_ref, wuq_ref, wuk_ref, gq_ref, wiq_ref,
                   qlat_ref, qidx_ref, wt_ref, ckv_ref, kidx_ref, z_ref):
    h = _rms(x_ref[...], nw_ref[...]).astype(BF16)
    lat = _dot(h, wlat_ref[...])
    cq = _rms(lat[:, :D_CQ], gcq_ref[...]).astype(BF16)
    ckv_ref[...] = _rms(lat[:, D_CQ:D_CQ + D_C], gkv_ref[...]).astype(BF16)
    n_lat = D_CQ + D_C + D_I
    kidx_ref[...] = _rms(lat[:, D_CQ + D_C:n_lat]).astype(BF16)
    wt_ref[...] = lat[:, n_lat:].T[:H_I] * (H_I ** -0.5)
    z_ref[...] = _dot(h, wz_ref[...])
    qn = _dot(cq, wuq_ref[...]).astype(BF16)
    for hh in range(H_A):
        ql = _dot(qn[:, hh * D_NOPE:(hh + 1) * D_NOPE], wuk_ref[hh])
        qlat_ref[hh] = (_rms(ql, gq_ref[...]) * (D_C ** -0.5 * LOG2_E)).astype(BF16)
    qi = _dot(cq, wiq_ref[...]) * (D_I ** -0.5)
    for hh in range(H_I):
        qidx_ref[hh] = qi[:, hh * D_I:(hh + 1) * D_I].astype(BF16)


def _a_proj(x2, nw, wlat, wz, gcq, gkv, wuq, wuk, gq, wiq, tile):
    n, d = x2.shape
    grid = (n // tile,)
    tok = lambda w: pl.BlockSpec((tile, w), lambda i: (i, 0))
    return pl.pallas_call(
        _a_proj_kernel,
        grid=grid,
        in_specs=[tok(d), _const_spec(nw.shape), _const_spec(wlat.shape),
                  _const_spec(wz.shape), _const_spec(gcq.shape), _const_spec(gkv.shape), _const_spec(wuq.shape),
                  _const_spec(wuk.shape), _const_spec(gq.shape), _const_spec(wiq.shape)],
        out_specs=[pl.BlockSpec((H_A, tile, D_C), lambda i: (0, i, 0)),
                   pl.BlockSpec((H_I, tile, D_I), lambda i: (0, i, 0)),
                   pl.BlockSpec((H_I, tile), lambda i: (0, i)),
                   tok(D_C), tok(D_I), tok(H_A * D_V)],
        out_shape=[jax.ShapeDtypeStruct((H_A, n, D_C), BF16),
                   jax.ShapeDtypeStruct((H_I, n, D_I), BF16),
                   jax.ShapeDtypeStruct((H_I, n), F32),
                   jax.ShapeDtypeStruct((n, D_C), BF16),
                   jax.ShapeDtypeStruct((n, D_I), BF16),
                   jax.ShapeDtypeStruct((n, H_A * D_V), F32)],
        compiler_params=_params("parallel"),
        name="a_proj",
    )(x2, nw, wlat, wz, gcq, gkv, wuq, wuk, gq, wiq)


def _a_attn_kernel(qidx_ref, wt_ref, kidx_ref, qlat_ref, ckv_ref, z_ref, bt_ref, wuv_ref,
                   y_ref, keys_ref, hi_ref, lo_ref, lg_ref, mx_ref, ss_ref, oacc_ref, *, topk, n_q):
    qb = pl.program_id(1)
    n_kt = qb + 1
    rows = H_A * Q_TILE
    half = K_TILE // 2
    qidx = qidx_ref[...].reshape(H_I * Q_TILE, D_I)
    qlat = qlat_ref[...].reshape(rows, D_C)

    def for_tile_groups(body):
        def pair(i, carry):
            body(2 * i, 2)
            return carry

        lax.fori_loop(0, lax.shift_right_logical(n_kt, 1), pair, 0)

        @pl.when((n_kt & 1) == 1)
        def _():
            body(n_kt - 1, 1)

    def for_tile_groups_init(body):
        odd = (n_kt & 1) == 1

        @pl.when(odd)
        def _():
            body(0, 1, True)

        @pl.when(jnp.logical_not(odd))
        def _():
            body(0, 2, True)

        start = jnp.where(odd, 1, 2)

        def pair(i, carry):
            body(start + 2 * i, 2, False)
            return carry

        lax.fori_loop(0, lax.shift_right_logical(n_kt - start, 1), pair, 0)

    def idx_tiles(c0, cnt):
        k0 = pl.multiple_of(c0 * K_TILE, K_TILE)
        lg = _dot_nt(kidx_ref[pl.ds(k0, cnt * K_TILE), :], qidx)
        sc = jnp.zeros((cnt * K_TILE, Q_TILE), F32)
        for hh in range(H_I):
            sc = sc + jnp.maximum(lg[:, hh * Q_TILE:(hh + 1) * Q_TILE], 0.0) * wt_ref[hh:hh + 1, :]
        bits = lax.bitcast_convert_type(sc, I32)
        sign = lax.shift_right_arithmetic(bits, 31)
        skey = (bits ^ (sign & 0x7FFFFFFF)) - sign
        key_pos = k0 + lax.broadcasted_iota(I32, (cnt * K_TILE, Q_TILE), 0)
        q_pos = qb * Q_TILE + lax.broadcasted_iota(I32, (cnt * K_TILE, Q_TILE), 1)
        skey = jnp.where(key_pos <= q_pos, skey, INT_MIN)
        hi = lax.shift_right_arithmetic(skey, HALF_BITS).astype(I16)
        lo = ((skey & ((1 << HALF_BITS) - 1)) + I16_MIN).astype(I16)
        for u in range(cnt):
            ks = slice(u * K_TILE, (u + 1) * K_TILE)
            keys_ref[c0 + u] = skey[ks]
            hi_ref[c0 + u] = hi[ks].reshape(K_TILE // PACKED_ROWS, PACKED_ROWS, Q_TILE)
            lo_ref[c0 + u] = lo[ks].reshape(K_TILE // PACKED_ROWS, PACKED_ROWS, Q_TILE)

    for_tile_groups(idx_tiles)

    one = jnp.ones((), BF16)
    zero = jnp.zeros((), BF16)

    def threshold(n_tiles):
        def count(ref16, bound16, strict):
            acc = jnp.zeros((PACKED_ROWS, Q_TILE), BF16)
            for c in range(n_tiles):
                tile = ref16[c]
                hit = jnp.where(tile > bound16[None] if strict else tile >= bound16[None], one, zero)
                parts = [hit[u] for u in range(K_TILE // PACKED_ROWS)]
                while len(parts) > 1:
                    parts = [parts[u] + parts[u + 1] for u in range(0, len(parts), 2)]
                acc = acc + parts[0]
            return jnp.sum(acc.astype(F32), axis=0, keepdims=True)

        def kth_largest(ref16, need, n_all):
            def bit_body(i, carry):
                thr, n_ge = carry
                cand = thr + lax.shift_left(jnp.int32(1), HALF_BITS - 1 - i)
                n_cand = count(ref16, cand.astype(I16), False)
                ok = n_cand >= need
                return jnp.where(ok, cand, thr), jnp.where(ok, n_cand, n_ge)

            return lax.fori_loop(0, HALF_BITS, bit_body, (jnp.full((PACKED_ROWS, Q_TILE), I16_MIN, I32), n_all))

        n_all = jnp.full((1, Q_TILE), float(n_tiles * K_TILE), F32)
        hi_thr, n_ge_hi = kth_largest(hi_ref, float(topk), n_all)
        hi_thr16 = hi_thr.astype(I16)
        n_gt_hi = count(hi_ref, hi_thr16, True)
        for c in range(n_tiles):
            lo_ref[c] = jnp.where(hi_ref[c] == hi_thr16[None], lo_ref[c], jnp.int16(I16_MIN))
        lo_thr, n_ge_lo = kth_largest(lo_ref, float(topk) - n_gt_hi, n_ge_hi - n_gt_hi)
        thr = hi_thr[0:1] * (1 << HALF_BITS) + (lo_thr[0:1] - I16_MIN)
        thr = jnp.maximum(thr, INT_MIN + 1)
        return thr, n_gt_hi + n_ge_lo - float(topk)

    thr, excess = lax.switch(qb, [functools.partial(threshold, n) for n in range(1, n_q + 1)])

    @pl.when(jnp.max(excess) > 0.0)
    def _():
        kr = lax.broadcasted_iota(I32, (K_TILE, K_TILE), 0)
        kc = lax.broadcasted_iota(I32, (K_TILE, K_TILE), 1)
        later = jnp.where(kc >= kr, 1.0, 0.0).astype(BF16)

        def tie_body(i, seen):
            c = n_kt - 1 - i
            keys = keys_ref[c]
            tie = keys == thr
            rank = _dot(later, jnp.where(tie, 1.0, 0.0).astype(BF16)) + seen
            keys_ref[c] = jnp.where(tie & (rank <= excess), thr - 1, keys)
            return rank[0:1]

        lax.fori_loop(0, n_kt, tie_body, jnp.zeros((1, Q_TILE), F32))

    def qk_tiles(c0, cnt, init):
        k0 = pl.multiple_of(c0 * K_TILE, K_TILE)
        lg = _dot_nt(qlat, ckv_ref[pl.ds(k0, cnt * K_TILE), :])
        for u in range(cnt):
            c = c0 + u
            mb = jnp.where(keys_ref[c] >= thr, 0.0, NEG_INF).T
            t_idx = jnp.minimum(qb - c, BIAS_CLASSES - 1)
            for hh in range(H_A):
                l = lg[hh * Q_TILE:(hh + 1) * Q_TILE, u * K_TILE:(u + 1) * K_TILE] + (mb + bt_ref[t_idx, hh])
                lg_ref[c, hh] = l
                top = jnp.maximum(l[:, :half], l[:, half:])
                mx_ref[hh] = top if (init and u == 0) else jnp.maximum(mx_ref[hh], top)

    for_tile_groups_init(qk_tiles)

    m = jnp.max(mx_ref[...], axis=-1, keepdims=True)

    def pv_tiles(c0, cnt, init):
        k0 = pl.multiple_of(c0 * K_TILE, K_TILE)
        ps = [jnp.exp2(lg_ref[c0 + u] - m) for u in range(cnt)]
        part = ps[0][:, :, :half] + ps[0][:, :, half:]
        for p in ps[1:]:
            part = part + (p[:, :, :half] + p[:, :, half:])
        p16 = jnp.concatenate([p.reshape(rows, K_TILE).astype(BF16) for p in ps], axis=1)
        pv = _dot(p16, ckv_ref[pl.ds(k0, cnt * K_TILE), :])
        if init:
            ss_ref[...] = part
            oacc_ref[...] = pv
        else:
            ss_ref[...] += part
            oacc_ref[...] += pv

    for_tile_groups_init(pv_tiles)
    inv_denom = 1.0 / jnp.sum(ss_ref[...], axis=-1, keepdims=True)

    for hh in range(H_A):
        oh = _dot(oacc_ref[hh * Q_TILE:(hh + 1) * Q_TILE, :].astype(BF16), wuv_ref[hh])
        zz = z_ref[:, hh * D_V:(hh + 1) * D_V]
        y_ref[:, hh * D_V:(hh + 1) * D_V] = (oh * inv_denom[hh] * (zz * _sigmoid(zz))).astype(BF16)


def _a_attn(qidx, wt, kidx, qlat, ckv, z, bt, wuv, batch, seq):
    n = batch * seq
    nq = seq // Q_TILE
    n_kt = seq // K_TILE
    topk = min(TOPK_MAX, seq // 4)
    grid = (batch, nq)
    kern = functools.partial(_a_attn_kernel, topk=topk, n_q=nq)
    return pl.pallas_call(
        kern,
        grid=grid,
        in_specs=[pl.BlockSpec((H_I, Q_TILE, D_I), lambda b, q: (0, b * nq + q, 0)),
                  pl.BlockSpec((H_I, Q_TILE), lambda b, q: (0, b * nq + q)),
                  pl.BlockSpec((seq, D_I), lambda b, q: (b, 0)),
                  pl.BlockSpec((H_A, Q_TILE, D_C), lambda b, q: (0, b * nq + q, 0)),
                  pl.BlockSpec((seq, D_C), lambda b, q: (b, 0)),
                  pl.BlockSpec((Q_TILE, H_A * D_V), lambda b, q: (b * nq + q, 0)),
                  pl.BlockSpec(bt.shape, lambda b, q: (0, 0, 0, 0), pipeline_mode=pl.Buffered(1)),
                  pl.BlockSpec(wuv.shape, lambda b, q: (0, 0, 0), pipeline_mode=pl.Buffered(1))],
        out_specs=pl.BlockSpec((Q_TILE, H_A * D_V), lambda b, q: (b * nq + q, 0)),
        out_shape=jax.ShapeDtypeStruct((n, H_A * D_V), BF16),
        scratch_shapes=[pltpu.VMEM((n_kt, K_TILE, Q_TILE), I32),
                        pltpu.VMEM((n_kt, K_TILE // PACKED_ROWS, PACKED_ROWS, Q_TILE), I16),
                        pltpu.VMEM((n_kt, K_TILE // PACKED_ROWS, PACKED_ROWS, Q_TILE), I16),
                        pltpu.VMEM((n_kt, H_A, Q_TILE, K_TILE), F32),
                        pltpu.VMEM((H_A, Q_TILE, K_TILE // 2), F32),
                        pltpu.VMEM((H_A, Q_TILE, K_TILE // 2), F32),
                        pltpu.VMEM((H_A * Q_TILE, D_C), F32)],
        compiler_params=_params("parallel", "arbitrary"),
        name="a_attn",
    )(qidx, wt, kidx, qlat, ckv, z, bt, wuv)


def _b_proj_kernel(x_ref, nw_ref, wqkv_ref, wba_ref, wbat_ref, wz_ref, cw_ref, alog_ref, dtb_ref, alogc_ref, dtbc_ref,
                   q_ref, k_ref, v_ref, beta_ref, g_ref, gt_ref, z_ref, buf_ref, *, tile):
    s = pl.program_id(1)
    h = _rms(x_ref[...], nw_ref[...]).astype(BF16)
    z_ref[...] = _dot(h, wz_ref[...])
    ba = _dot(h, wba_ref[...])
    beta_ref[...] = _sigmoid(ba[:, :H_B])
    g_ref[...] = -jnp.exp(alog_ref[...]) * _softplus(ba[:, H_B:2 * H_B] + dtb_ref[...])
    bat = _dot_nt(wbat_ref[...], h)
    gt_ref[...] = -jnp.exp(alogc_ref[...]) * _softplus(bat[H_B:, :] + dtbc_ref[...])

    @pl.when(s == 0)
    def _():
        buf_ref[0:8, :] = jnp.zeros((8, buf_ref.shape[1]), F32)

    width = H_B * D_K
    for sec, out_ref in enumerate((q_ref, k_ref, v_ref)):
        cols = slice(sec * width, (sec + 1) * width)
        pre = _dot(h, wqkv_ref[:, cols])
        buf_ref[8:8 + tile, cols] = pre
        acc = pre * cw_ref[CONV_W - 1:CONV_W, cols]
        for w in range(CONV_W - 1):
            acc = acc + buf_ref[8 - (CONV_W - 1) + w:8 - (CONV_W - 1) + w + tile, cols] * cw_ref[w:w + 1, cols]
        buf_ref[0:8, cols] = buf_ref[tile:tile + 8, cols]
        y = acc * _sigmoid(acc)
        if sec == 2:
            out_ref[...] = y
        else:
            scale = (D_K ** -0.5) if sec == 0 else 1.0
            for hh in range(H_B):
                yy = y[:, hh * D_K:(hh + 1) * D_K]
                nrm = lax.rsqrt(jnp.sum(yy * yy, axis=-1, keepdims=True) + EPS)
                out_ref[:, hh * D_K:(hh + 1) * D_K] = yy * (nrm * scale)


def _b_proj(x2, nw, wqkv, wba, wbat, wz, cw, alog, dtb, batch, seq, tile):
    n, d = x2.shape
    ns = seq // tile
    grid = (batch, ns)
    tok = lambda w: pl.BlockSpec((tile, w), lambda b, s: (b * ns + s, 0))
    width = H_B * D_K
    kern = functools.partial(_b_proj_kernel, tile=tile)
    return pl.pallas_call(
        kern,
        grid=grid,
        in_specs=[tok(d), _const_spec(nw.shape), _const_spec(wqkv.shape), _const_spec(wba.shape),
                  _const_spec(wbat.shape), _const_spec(wz.shape), _const_spec(cw.shape),
                  _const_spec((1, H_B)), _const_spec((1, H_B)), _const_spec((H_B, 1)), _const_spec((H_B, 1))],
        out_specs=[tok(width), tok(width), tok(width), tok(H_B), tok(H_B),
                   pl.BlockSpec((H_B, tile), lambda b, s: (0, b * ns + s)), tok(width)],
        out_shape=[jax.ShapeDtypeStruct((n, width), F32)] * 3
        + [jax.ShapeDtypeStruct((n, H_B), F32)] * 2
        + [jax.ShapeDtypeStruct((H_B, n), F32), jax.ShapeDtypeStruct((n, width), F32)],
        scratch_shapes=[pltpu.VMEM((tile + 8, 3 * width), F32)],
        compiler_params=_params("parallel", "arbitrary"),
        name="b_proj",
    )(x2, nw, wqkv, wba, wbat, wz, cw, alog.reshape(1, H_B), dtb.reshape(1, H_B),
      alog.reshape(H_B, 1), dtb.reshape(H_B, 1))


def _b_gdn_kernel(q_ref, k_ref, v_ref, z_ref, beta_ref, g_ref, gt_ref, go_ref, y_ref, state_ref):
    hb = pl.program_id(1)
    s = pl.program_id(2)
    t = GDN_TILE
    n_chunks = t // CHUNK
    log_c = int(math.log2(CHUNK))
    heads = range(GDN_HEADS)

    @pl.when(s == 0)
    def _():
        state_ref[...] = jnp.zeros(state_ref.shape, F32)

    r = lax.broadcasted_iota(I32, (t, t), 0)
    c = lax.broadcasted_iota(I32, (t, t), 1)
    xs = r ^ c
    same = xs < CHUNK
    incl = same & (r >= c)
    upper = same & (r <= c)
    eye = (r == c).astype(F32)
    level = [(lax.shift_right_logical(xs, lb) == 1) & ((r & (1 << lb)) != 0) for lb in range(log_c)]
    lane8 = lax.broadcasted_iota(I32, (t, H_B), 1)
    sub8 = lax.broadcasted_iota(I32, (H_B, t), 0)

    b_col, decay, eg_col, ekl_col, egl_col = [], [], [], [], []
    for j in heads:
        head = hb * GDN_HEADS + j
        g_col = jnp.sum(jnp.where(lane8 == head, g_ref[...], 0.0), axis=1, keepdims=True)
        b_col.append(jnp.sum(jnp.where(lane8 == head, beta_ref[...], 0.0), axis=1, keepdims=True))
        g_row = jnp.sum(jnp.where(sub8 == head, gt_ref[...], 0.0), axis=0, keepdims=True)
        g_rows = jnp.broadcast_to(g_row, (t, t))
        gc_col = jnp.sum(jnp.where(incl, g_rows, 0.0), axis=1, keepdims=True)
        gl_col = jnp.sum(jnp.where(same, g_rows, 0.0), axis=1, keepdims=True)
        gc_row = jnp.sum(jnp.where(upper, jnp.broadcast_to(g_col, (t, t)), 0.0), axis=0, keepdims=True)
        decay.append(jnp.where(incl, jnp.exp(jnp.minimum(gc_col - gc_row, 0.0)), 0.0))
        eg_col.append(jnp.exp(gc_col))
        ekl_col.append(jnp.exp(gl_col - gc_col))
        egl_col.append(jnp.exp(gl_col))

    lmat, aqk, rhs, q_dec, k_dec = [], [], [], [], []
    for j in heads:
        hs = slice(j * D_K, (j + 1) * D_K)
        qf, kf, vf = q_ref[:, hs], k_ref[:, hs], v_ref[:, hs]
        kb = kf * b_col[j]
        k16 = kf.astype(BF16)
        gram = _dot_nt(jnp.concatenate([kb.astype(BF16), qf.astype(BF16)], axis=0), k16)
        lmat.append(gram[:t] * decay[j])
        aqk.append(jnp.where(incl, gram[t:] * decay[j], 0.0).astype(BF16))
        rhs.append(jnp.concatenate([(vf * b_col[j]).astype(BF16), (kb * eg_col[j]).astype(BF16)], axis=1))
        q_dec.append(qf * eg_col[j])
        k_dec.append((kf * ekl_col[j]).astype(BF16))

    tinv = [eye - jnp.where(level[0], lmat[j], 0.0) for j in heads]
    for lb in range(1, log_c):
        half = 1 << lb
        t16 = [tinv[j].astype(BF16) for j in heads]
        if half < 8:
            y16 = [_dot(jnp.where(level[lb], lmat[j], 0.0).astype(BF16), t16[j]).astype(BF16) for j in heads]
            tinv = [tinv[j] - _dot(t16[j], y16[j]) for j in heads]
            continue
        pairs = t // (2 * half)
        split = lambda a: a.reshape(pairs, 2, half, t)
        lower = lambda a: split(a)[:, 1].reshape(t // 2, t)
        i = lax.broadcasted_iota(I32, (t // 2, t), 0)
        r_low = lax.shift_left(lax.shift_right_logical(i, lb), lb + 1) + half + (i & (half - 1))
        in_b = lax.shift_right_logical(r_low ^ lax.broadcasted_iota(I32, (t // 2, t), 1), lb) == 1
        y_low = [_dot(jnp.where(in_b, lower(lmat[j]), 0.0).astype(BF16), t16[j]) for j in heads]
        zeros = jnp.zeros((pairs, half, t), F32)
        y16 = [jnp.stack([zeros, y_low[j].reshape(pairs, half, t)], axis=1).reshape(t, t).astype(BF16) for j in heads]
        t_low = [lower(tinv[j]) for j in heads]
        t_low = [t_low[j] - _dot(t_low[j].astype(BF16), y16[j]) for j in heads]
        tinv = [jnp.stack([split(tinv[j])[:, 0], t_low[j].reshape(pairs, half, t)], axis=1).reshape(t, t)
                for j in heads]

    sol16 = [_dot(tinv[j].astype(BF16), rhs[j]).astype(BF16) for j in heads]
    aux = [_dot(aqk[j], sol16[j]) for j in heads]
    q_eff = [(q_dec[j] - aux[j][:, D_VB:]).astype(BF16) for j in heads]
    kw = [[_dot_tn(k_dec[j][ci * CHUNK:(ci + 1) * CHUNK], sol16[j][ci * CHUNK:(ci + 1) * CHUNK])
           for ci in range(n_chunks)] for j in heads]

    state = [state_ref[j] for j in heads]
    outs = [[] for _ in heads]
    for ci in range(n_chunks):
        rs = slice(ci * CHUNK, (ci + 1) * CHUNK)
        for j in heads:
            s16 = state[j].astype(BF16)
            both = _dot(jnp.concatenate([q_eff[j][rs], kw[j][ci][:, D_VB:].astype(BF16)], axis=0), s16)
            outs[j].append(both[:CHUNK] + aux[j][rs, :D_VB])
            state[j] = state[j] * egl_col[j][ci * CHUNK:ci * CHUNK + 1, :] + kw[j][ci][:, :D_VB] - both[CHUNK:]

    for j in heads:
        hs = slice(j * D_K, (j + 1) * D_K)
        state_ref[j] = state[j]
        o = jnp.concatenate(outs[j], axis=0)
        zz = z_ref[:, hs]
        y_ref[:, hs] = (_rms(o, go_ref[...]) * (zz * _sigmoid(zz))).astype(BF16)


def _b_gdn(q, k, v, z, beta, g, gt, go, batch, seq):
    n = batch * seq
    ns = seq // GDN_TILE
    nh = H_B // GDN_HEADS
    grid = (batch, nh, ns)
    wide = pl.BlockSpec((GDN_TILE, GDN_HEADS * D_K), lambda b, h, s: (b * ns + s, h))
    narrow = pl.BlockSpec((GDN_TILE, H_B), lambda b, h, s: (b * ns + s, 0))
    return pl.pallas_call(
        _b_gdn_kernel,
        grid=grid,
        in_specs=[wide, wide, wide, wide, narrow, narrow,
                  pl.BlockSpec((H_B, GDN_TILE), lambda b, h, s: (0, b * ns + s)),
                  _const_spec(go.shape)],
        out_specs=wide,
        out_shape=jax.ShapeDtypeStruct((n, H_B * D_VB), BF16),
        scratch_shapes=[pltpu.VMEM((GDN_HEADS, D_K, D_VB), F32)],
        compiler_params=_params("parallel", "parallel", "arbitrary"),
        name="b_gdn",
    )(q, k, v, z, beta, g, gt, go)


def _out_ple_kernel(x_ref, y_ref, p_ref, wout_ref, pn_ref, wgate_ref, wproj_ref, o_ref):
    x1 = x_ref[...] + _dot(y_ref[...], wout_ref[...])
    hn = _rms(x1, pn_ref[...]).astype(BF16)
    gate = _sigmoid(_dot(hn, wgate_ref[...]))
    o_ref[...] = x1 + gate * _dot(p_ref[...].astype(BF16), wproj_ref[...])


def _out_ple(x2, y, p2, layer, wout, pn, wgate, wproj, tile):
    n, d = x2.shape
    tok = lambda w: pl.BlockSpec((tile, w), lambda i: (i, 0))
    return pl.pallas_call(
        _out_ple_kernel,
        grid=(n // tile,),
        in_specs=[tok(d), tok(y.shape[1]),
                  pl.BlockSpec((tile, p2.shape[1]), lambda i: (layer * (n // tile) + i, 0)),
                  _const_spec(wout.shape), _const_spec(pn.shape),
                  _const_spec(wgate.shape), _const_spec(wproj.shape)],
        out_specs=tok(d),
        out_shape=jax.ShapeDtypeStruct((n, d), F32),
        compiler_params=_params("parallel"),
        name="out_ple",
    )(x2, y, p2, wout, pn, wgate, wproj)


def _t5_bucket(rel):
    max_exact = N_BUCKETS // 2
    rel = jnp.maximum(rel, 0)
    rel_f = jnp.maximum(rel, 1).astype(F32)
    log_ratio = jnp.log(rel_f / max_exact) / math.log(MAX_DISTANCE / max_exact)
    large = max_exact + (log_ratio * (N_BUCKETS - max_exact)).astype(I32)
    large = jnp.minimum(large, N_BUCKETS - 1)
    return jnp.where(rel < max_exact, rel, large)


def _bias_tiles(rel_bias):
    span = Q_TILE + K_TILE
    m = jnp.arange(span, dtype=I32)
    key_minus_query = jnp.where(m < K_TILE, m, m - span)
    scaled = rel_bias.astype(F32) * LOG2_E
    tiles = []
    for cls in range(BIAS_CLASSES):
        w = scaled[_t5_bucket(cls * Q_TILE - key_minus_query)].T
        skew = jnp.tile(w, (1, Q_TILE))[:, :Q_TILE * (span - 1)].reshape(H_A, Q_TILE, span - 1)
        tiles.append(skew[:, :, :K_TILE])
    return jnp.stack(tiles)


def kernel(x, p, norm_w, a_w_in, a_g_cq, a_w_uq, a_w_uk, a_g_q, a_g_kv, a_w_iq, a_w_uv, a_w_out, rel_bias, b_w_in, b_conv_w, b_a_log, b_dt_bias, b_g_o, b_w_out, ple_norm, ple_w_gate, ple_w_proj):
    batch, seq, d_model = x.shape
    depth = p.shape[0]
    n = batch * seq
    assert seq % K_TILE == 0 and seq % GDN_TILE == 0 and H_B % GDN_HEADS == 0
    x2 = x.reshape(n, d_model)
    p2 = p.reshape(depth * n, p.shape[-1])
    bt = _bias_tiles(rel_bias)
    row = lambda a: a.reshape(1, -1).astype(F32)
    n_lat = D_CQ + D_C + D_I
    b_qkv = H_B * (2 * D_K + D_VB)
    for i in range(depth):
        j = i // 2
        if i % 2 == 0:
            w_in = a_w_in[j]
            wlat = jnp.pad(w_in[:, :n_lat + H_I], ((0, 0), (0, LANES - H_I))).astype(BF16)
            wz = w_in[:, n_lat + H_I:].astype(BF16)
            qlat, qidx, wt, ckv, kidx, z = _a_proj(
                x2, row(norm_w[i]), wlat, wz, row(a_g_cq[j]), row(a_g_kv[j]),
                a_w_uq[j].reshape(D_CQ, H_A * D_NOPE).astype(BF16), a_w_uk[j].astype(BF16), row(a_g_q[j]),
                a_w_iq[j].reshape(D_CQ, H_I * D_I).astype(BF16), tile=1024)
            y = _a_attn(qidx, wt, kidx, qlat, ckv, z, bt, a_w_uv[j].astype(BF16), batch, seq)
            w_out = a_w_out[j]
        else:
            w_in = b_w_in[j]
            wqkv = w_in[:, :b_qkv].astype(BF16)
            w_ba = w_in[:, b_qkv:b_qkv + 2 * H_B]
            wba = jnp.pad(w_ba, ((0, 0), (0, LANES - 2 * H_B))).astype(BF16)
            wz = w_in[:, b_qkv + 2 * H_B:].astype(BF16)
            q, k, v, beta, g, gt, z = _b_proj(
                x2, row(norm_w[i]), wqkv, wba, w_ba.T.astype(BF16), wz, b_conv_w[j].astype(F32),
                b_a_log[j].astype(F32), b_dt_bias[j].astype(F32), batch, seq, tile=512)
            y = _b_gdn(q, k, v, z, beta, g, gt, row(b_g_o[j]), batch, seq)
            w_out = b_w_out[j]
        x2 = _out_ple(x2, y, p2, i, w_out.astype(BF16), row(ple_norm[i]), ple_w_gate[i].astype(BF16),
                      ple_w_proj[i].astype(BF16), tile=1024)
    return x2.reshape(batch, seq, d_model)
```

```python
import functools
import math

import jax
import jax.numpy as jnp
from jax import lax
from jax.experimental import pallas as pl
from jax.experimental.pallas import tpu as pltpu

F32 = jnp.float32
BF16 = jnp.bfloat16
I32 = jnp.int32
I16 = jnp.int16

EPS = 1e-6
NEG_INF = -1e30
LOG2_E = math.log2(math.e)
INT_MIN = -(2 ** 31)
HALF_BITS = 16
I16_MIN = -(1 << (HALF_BITS - 1))

LANES = 128
PACKED_ROWS = 16
VMEM_LIMIT_BYTES = 56 * 1024 * 1024

H_A = 8
D_C = 256
D_CQ = 256
D_I = 128
H_I = 8
D_NOPE = 128
D_V = 128
TOPK_MAX = 256
N_BUCKETS = 32
MAX_DISTANCE = 128
Q_TILE = 256
K_TILE = 256
BIAS_CLASSES = -(-(MAX_DISTANCE + K_TILE - 1) // Q_TILE) + 1
H_B = 8
D_K = 128
D_VB = 128
CONV_W = 4
CHUNK = 64
GDN_TILE = 256
GDN_HEADS = 8
PROJ_TILE = 1024
CONV_TILE = 512


def _dot(a, b):
    return jnp.dot(a, b, preferred_element_type=F32)


def _dot_nt(a, b):
    return lax.dot_general(a, b, (((1,), (1,)), ((), ())), preferred_element_type=F32)


def _dot_tn(a, b):
    return lax.dot_general(a, b, (((0,), (0,)), ((), ())), preferred_element_type=F32)


def _rms(x, gain=None):
    y = x * lax.rsqrt(jnp.mean(x * x, axis=-1, keepdims=True) + EPS)
    return y if gain is None else y * gain


def _sigmoid(x):
    return 1.0 / (1.0 + jnp.exp2(x * -LOG2_E))


def _softplus(x):
    return jnp.maximum(x, 0.0) + jnp.log1p(jnp.exp(-jnp.abs(x)))


def _params(*semantics):
    return pltpu.CompilerParams(dimension_semantics=semantics, vmem_limit_bytes=VMEM_LIMIT_BYTES)


def _const_spec(shape):
    nd = len(shape)
    return pl.BlockSpec(shape, lambda *_: (0,) * nd)


def _a_proj_kernel(x_ref, nw_ref, wlat_ref, wz_ref, gcq_ref, gkv_ref, wuq_ref, wuk_ref, gq_ref, wiq_ref,
                   qlat_ref, qidx_ref, wt_ref, ckv_ref, kidx_ref, z_ref):
    h = _rms(x_ref[...], nw_ref[...]).astype(BF16)
    lat = _dot(h, wlat_ref[...])
    cq = _rms(lat[:, :D_CQ], gcq_ref[...]).astype(BF16)
    ckv_ref[...] = _rms(lat[:, D_CQ:D_CQ + D_C], gkv_ref[...]).astype(BF16)
    n_lat = D_CQ + D_C + D_I
    kidx_ref[...] = _rms(lat[:, D_CQ + D_C:n_lat]).astype(BF16)
    wt_ref[...] = lat[:, n_lat:].T[:H_I] * (H_I ** -0.5)
    z_ref[...] = _dot(h, wz_ref[...])
    qn = _dot(cq, wuq_ref[...]).astype(BF16)
    for hh in range(H_A):
        ql = _dot(qn[:, hh * D_NOPE:(hh + 1) * D_NOPE], wuk_ref[hh])
        qlat_ref[hh] = (_rms(ql, gq_ref[...]) * (D_C ** -0.5 * LOG2_E)).astype(BF16)
    qi = _dot(cq, wiq_ref[...]) * (D_I ** -0.5)
    for hh in range(H_I):
        qidx_ref[hh] = qi[:, hh * D_I:(hh + 1) * D_I].astype(BF16)


def _a_proj(x2, nw, wlat, wz, gcq, gkv, wuq, wuk, gq, wiq, tile):
    n, d = x2.shape
    grid = (n // tile,)
    tok = lambda w: pl.BlockSpec((tile, w), lambda i: (i, 0))
    return pl.pallas_call(
        _a_proj_kernel,
        grid=grid,
        in_specs=[tok(d), _const_spec(nw.shape), _const_spec(wlat.shape),
                  _const_spec(wz.shape), _const_spec(gcq.shape), _const_spec(gkv.shape), _const_spec(wuq.shape),
                  _const_spec(wuk.shape), _const_spec(gq.shape), _const_spec(wiq.shape)],
        out_specs=[pl.BlockSpec((H_A, tile, D_C), lambda i: (0, i, 0)),
                   pl.BlockSpec((H_I, tile, D_I), lambda i: (0, i, 0)),
                   pl.BlockSpec((H_I, tile), lambda i: (0, i)),
                   tok(D_C), tok(D_I), tok(H_A * D_V)],
        out_shape=[jax.ShapeDtypeStruct((H_A, n, D_C), BF16),
                   jax.ShapeDtypeStruct((H_I, n, D_I), BF16),
                   jax.ShapeDtypeStruct((H_I, n), F32),
                   jax.ShapeDtypeStruct((n, D_C), BF16),
                   jax.ShapeDtypeStruct((n, D_I), BF16),
                   jax.ShapeDtypeStruct((n, H_A * D_V), F32)],
        compiler_params=_params("parallel"),
        name="a_proj",
    )(x2, nw, wlat, wz, gcq, gkv, wuq, wuk, gq, wiq)


def _a_attn_kernel(qidx_ref, wt_ref, kidx_ref, qlat_ref, ckv_ref, z_ref, bt_ref, wuv_ref,
                   y_ref, keys_ref, hi_ref, lo_ref, lg_ref, mx_ref, ss_ref, oacc_ref, *, topk, n_q):
    qb = pl.program_id(1)
    n_kt = qb + 1
    rows = H_A * Q_TILE
    half = K_TILE // 2
    qidx = qidx_ref[...].reshape(H_I * Q_TILE, D_I)
    qlat = qlat_ref[...].reshape(rows, D_C)

    def for_tile_groups(body):
        def pair(i, carry):
            body(2 * i, 2)
            return carry

        lax.fori_loop(0, lax.shift_right_logical(n_kt, 1), pair, 0)

        @pl.when((n_kt & 1) == 1)
        def _():
            body(n_kt - 1, 1)

    def for_tile_groups_init(body):
        odd = (n_kt & 1) == 1

        @pl.when(odd)
        def _():
            body(0, 1, True)

        @pl.when(jnp.logical_not(odd))
        def _():
            body(0, 2, True)

        start = jnp.where(odd, 1, 2)

        def pair(i, carry):
            body(start + 2 * i, 2, False)
            return carry

        lax.fori_loop(0, lax.shift_right_logical(n_kt - start, 1), pair, 0)

    def idx_tiles(c0, cnt):
        k0 = pl.multiple_of(c0 * K_TILE, K_TILE)
        lg = _dot_nt(kidx_ref[pl.ds(k0, cnt * K_TILE), :], qidx)
        sc = jnp.zeros((cnt * K_TILE, Q_TILE), F32)
        for hh in range(H_I):
            sc = sc + jnp.maximum(lg[:, hh * Q_TILE:(hh + 1) * Q_TILE], 0.0) * wt_ref[hh:hh + 1, :]
        bits = lax.bitcast_convert_type(sc, I32)
        sign = lax.shift_right_arithmetic(bits, 31)
        skey = (bits ^ (sign & 0x7FFFFFFF)) - sign
        key_pos = k0 + lax.broadcasted_iota(I32, (cnt * K_TILE, Q_TILE), 0)
        q_pos = qb * Q_TILE + lax.broadcasted_iota(I32, (cnt * K_TILE, Q_TILE), 1)
        skey = jnp.where(key_pos <= q_pos, skey, INT_MIN)
        hi = lax.shift_right_arithmetic(skey, HALF_BITS).astype(I16)
        lo = ((skey & ((1 << HALF_BITS) - 1)) + I16_MIN).astype(I16)
        for u in range(cnt):
            ks = slice(u * K_TILE, (u + 1) * K_TILE)
            keys_ref[c0 + u] = skey[ks]
            hi_ref[c0 + u] = hi[ks].reshape(K_TILE // PACKED_ROWS, PACKED_ROWS, Q_TILE)
            lo_ref[c0 + u] = lo[ks].reshape(K_TILE // PACKED_ROWS, PACKED_ROWS, Q_TILE)

    for_tile_groups(idx_tiles)

    one = jnp.ones((), BF16)
    zero = jnp.zeros((), BF16)

    def threshold(n_tiles):
        def count(ref16, bound16, strict):
            acc = jnp.zeros((PACKED_ROWS, Q_TILE), BF16)
            for c in range(n_tiles):
                tile = ref16[c]
                hit = jnp.where(tile > bound16[None] if strict else tile >= bound16[None], one, zero)
                parts = [hit[u] for u in range(K_TILE // PACKED_ROWS)]
                while len(parts) > 1:
                    parts = [parts[u] + parts[u + 1] for u in range(0, len(parts), 2)]
                acc = acc + parts[0]
            return jnp.sum(acc.astype(F32), axis=0, keepdims=True)

        def kth_largest(ref16, need, n_all):
            def bit_body(i, carry):
                thr, n_ge = carry
                cand = thr + lax.shift_left(jnp.int32(1), HALF_BITS - 1 - i)
                n_cand = count(ref16, cand.astype(I16), False)
                ok = n_cand >= need
                return jnp.where(ok, cand, thr), jnp.where(ok, n_cand, n_ge)

            return lax.fori_loop(0, HALF_BITS, bit_body, (jnp.full((PACKED_ROWS, Q_TILE), I16_MIN, I32), n_all))

        n_all = jnp.full((1, Q_TILE), float(n_tiles * K_TILE), F32)
        hi_thr, n_ge_hi = kth_largest(hi_ref, float(topk), n_all)
        hi_thr16 = hi_thr.astype(I16)
        n_gt_hi = count(hi_ref, hi_thr16, True)
        for c in range(n_tiles):
            lo_ref[c] = jnp.where(hi_ref[c] == hi_thr16[None], lo_ref[c], jnp.int16(I16_MIN))
        lo_thr, n_ge_lo = kth_largest(lo_ref, float(topk) - n_gt_hi, n_ge_hi - n_gt_hi)
        thr = hi_thr[0:1] * (1 << HALF_BITS) + (lo_thr[0:1] - I16_MIN)
        thr = jnp.maximum(thr, INT_MIN + 1)
        return thr, n_gt_hi + n_ge_lo - float(topk)

    thr, excess = lax.switch(qb, [functools.partial(threshold, n) for n in range(1, n_q + 1)])

    @pl.when(jnp.max(excess) > 0.0)
    def _():
        kr = lax.broadcasted_iota(I32, (K_TILE, K_TILE), 0)
        kc = lax.broadcasted_iota(I32, (K_TILE, K_TILE), 1)
        later = jnp.where(kc >= kr, 1.0, 0.0).astype(BF16)

        def tie_body(i, seen):
            c = n_kt - 1 - i
            keys = keys_ref[c]
            tie = keys == thr
            rank = _dot(later, jnp.where(tie, 1.0, 0.0).astype(BF16)) + seen
            keys_ref[c] = jnp.where(tie & (rank <= excess), thr - 1, keys)
            return rank[0:1]

        lax.fori_loop(0, n_kt, tie_body, jnp.zeros((1, Q_TILE), F32))

    def qk_tiles(c0, cnt, init):
        k0 = pl.multiple_of(c0 * K_TILE, K_TILE)
        lg = _dot_nt(qlat, ckv_ref[pl.ds(k0, cnt * K_TILE), :])
        for u in range(cnt):
            c = c0 + u
            mb = jnp.where(keys_ref[c] >= thr, 0.0, NEG_INF).T
            t_idx = jnp.minimum(qb - c, BIAS_CLASSES - 1)
            for hh in range(H_A):
                l = lg[hh * Q_TILE:(hh + 1) * Q_TILE, u * K_TILE:(u + 1) * K_TILE] + (mb + bt_ref[t_idx, hh])
                lg_ref[c, hh] = l
                top = jnp.maximum(l[:, :half], l[:, half:])
                mx_ref[hh] = top if (init and u == 0) else jnp.maximum(mx_ref[hh], top)

    for_tile_groups_init(qk_tiles)

    m = jnp.max(mx_ref[...], axis=-1, keepdims=True)

    def pv_tiles(c0, cnt, init):
        k0 = pl.multiple_of(c0 * K_TILE, K_TILE)
        ps = [jnp.exp2(lg_ref[c0 + u] - m) for u in range(cnt)]
        part = ps[0][:, :, :half] + ps[0][:, :, half:]
        for p in ps[1:]:
            part = part + (p[:, :, :half] + p[:, :, half:])
        p16 = jnp.concatenate([p.reshape(rows, K_TILE).astype(BF16) for p in ps], axis=1)
        pv = _dot(p16, ckv_ref[pl.ds(k0, cnt * K_TILE), :])
        if init:
            ss_ref[...] = part
            oacc_ref[...] = pv
        else:
            ss_ref[...] += part
            oacc_ref[...] += pv

    for_tile_groups_init(pv_tiles)
    inv_denom = 1.0 / jnp.sum(ss_ref[...], axis=-1, keepdims=True)

    for hh in range(H_A):
        oh = _dot(oacc_ref[hh * Q_TILE:(hh + 1) * Q_TILE, :].astype(BF16), wuv_ref[hh])
        zz = z_ref[:, hh * D_V:(hh + 1) * D_V]
        y_ref[:, hh * D_V:(hh + 1) * D_V] = (oh * inv_denom[hh] * (zz * _sigmoid(zz))).astype(BF16)


def _a_attn(qidx, wt, kidx, qlat, ckv, z, bt, wuv, batch, seq):
    n = batch * seq
    nq = seq // Q_TILE
    n_kt = seq // K_TILE
    topk = min(TOPK_MAX, seq // 4)
    grid = (batch, nq)
    kern = functools.partial(_a_attn_kernel, topk=topk, n_q=nq)
    return pl.pallas_call(
        kern,
        grid=grid,
        in_specs=[pl.BlockSpec((H_I, Q_TILE, D_I), lambda b, q: (0, b * nq + q, 0)),
                  pl.BlockSpec((H_I, Q_TILE), lambda b, q: (0, b * nq + q)),
                  pl.BlockSpec((seq, D_I), lambda b, q: (b, 0)),
                  pl.BlockSpec((H_A, Q_TILE, D_C), lambda b, q: (0, b * nq + q, 0)),
                  pl.BlockSpec((seq, D_C), lambda b, q: (b, 0)),
                  pl.BlockSpec((Q_TILE, H_A * D_V), lambda b, q: (b * nq + q, 0)),
                  pl.BlockSpec(bt.shape, lambda b, q: (0, 0, 0, 0), pipeline_mode=pl.Buffered(1)),
                  pl.BlockSpec(wuv.shape, lambda b, q: (0, 0, 0), pipeline_mode=pl.Buffered(1))],
        out_specs=pl.BlockSpec((Q_TILE, H_A * D_V), lambda b, q: (b * nq + q, 0)),
        out_shape=jax.ShapeDtypeStruct((n, H_A * D_V), BF16),
        scratch_shapes=[pltpu.VMEM((n_kt, K_TILE, Q_TILE), I32),
                        pltpu.VMEM((n_kt, K_TILE // PACKED_ROWS, PACKED_ROWS, Q_TILE), I16),
                        pltpu.VMEM((n_kt, K_TILE // PACKED_ROWS, PACKED_ROWS, Q_TILE), I16),
                        pltpu.VMEM((n_kt, H_A, Q_TILE, K_TILE), F32),
                        pltpu.VMEM((H_A, Q_TILE, K_TILE // 2), F32),
                        pltpu.VMEM((H_A, Q_TILE, K_TILE // 2), F32),
                        pltpu.VMEM((H_A * Q_TILE, D_C), F32)],
        compiler_params=_params("parallel", "arbitrary"),
        name="a_attn",
    )(qidx, wt, kidx, qlat, ckv, z, bt, wuv)


def _b_proj_kernel(x_ref, nw_ref, wqkv_ref, wba_ref, wbat_ref, wz_ref, cw_ref, alog_ref, dtb_ref, alogc_ref, dtbc_ref,
                   q_ref, k_ref, v_ref, beta_ref, g_ref, gt_ref, z_ref, buf_ref, *, tile):
    s = pl.program_id(1)
    h = _rms(x_ref[...], nw_ref[...]).astype(BF16)
    z_ref[...] = _dot(h, wz_ref[...])
    ba = _dot(h, wba_ref[...])
    beta_ref[...] = _sigmoid(ba[:, :H_B])
    g_ref[...] = -jnp.exp(alog_ref[...]) * _softplus(ba[:, H_B:2 * H_B] + dtb_ref[...])
    bat = _dot_nt(wbat_ref[...], h)
    gt_ref[...] = -jnp.exp(alogc_ref[...]) * _softplus(bat[H_B:, :] + dtbc_ref[...])

    @pl.when(s == 0)
    def _():
        buf_ref[0:8, :] = jnp.zeros((8, buf_ref.shape[1]), F32)

    width = H_B * D_K
    for sec, out_ref in enumerate((q_ref, k_ref, v_ref)):
        cols = slice(sec * width, (sec + 1) * width)
        pre = _dot(h, wqkv_ref[:, cols])
        buf_ref[8:8 + tile, cols] = pre
        acc = pre * cw_ref[CONV_W - 1:CONV_W, cols]
        for w in range(CONV_W - 1):
            acc = acc + buf_ref[8 - (CONV_W - 1) + w:8 - (CONV_W - 1) + w + tile, cols] * cw_ref[w:w + 1, cols]
        buf_ref[0:8, cols] = buf_ref[tile:tile + 8, cols]
        y = acc * _sigmoid(acc)
        if sec == 2:
            out_ref[...] = y
        else:
            scale = (D_K ** -0.5) if sec == 0 else 1.0
            for hh in range(H_B):
                yy = y[:, hh * D_K:(hh + 1) * D_K]
                nrm = lax.rsqrt(jnp.sum(yy * yy, axis=-1, keepdims=True) + EPS)
                out_ref[:, hh * D_K:(hh + 1) * D_K] = yy * (nrm * scale)


def _b_proj(x2, nw, wqkv, wba, wbat, wz, cw, alog, dtb, batch, seq, tile):
    n, d = x2.shape
    ns = seq // tile
    grid = (batch, ns)
    tok = lambda w: pl.BlockSpec((tile, w), lambda b, s: (b * ns + s, 0))
    width = H_B * D_K
    kern = functools.partial(_b_proj_kernel, tile=tile)
    return pl.pallas_call(
        kern,
        grid=grid,
        in_specs=[tok(d), _const_spec(nw.shape), _const_spec(wqkv.shape), _const_spec(wba.shape),
                  _const_spec(wbat.shape), _const_spec(wz.shape), _const_spec(cw.shape),
                  _const_spec((1, H_B)), _const_spec((1, H_B)), _const_spec((H_B, 1)), _const_spec((H_B, 1))],
        out_specs=[tok(width), tok(width), tok(width), tok(H_B), tok(H_B),
                   pl.BlockSpec((H_B, tile), lambda b, s: (0, b * ns + s)), tok(width)],
        out_shape=[jax.ShapeDtypeStruct((n, width), F32)] * 3
        + [jax.ShapeDtypeStruct((n, H_B), F32)] * 2
        + [jax.ShapeDtypeStruct((H_B, n), F32), jax.ShapeDtypeStruct((n, width), F32)],
        scratch_shapes=[pltpu.VMEM((tile + 8, 3 * width), F32)],
        compiler_params=_params("parallel", "arbitrary"),
        name="b_proj",
    )(x2, nw, wqkv, wba, wbat, wz, cw, alog.reshape(1, H_B), dtb.reshape(1, H_B),
      alog.reshape(H_B, 1), dtb.reshape(H_B, 1))


def _b_gdn_kernel(q_ref, k_ref, v_ref, z_ref, beta_ref, g_ref, gt_ref, go_ref, y_ref, state_ref):
    hb = pl.program_id(1)
    s = pl.program_id(2)
    t = GDN_TILE
    n_chunks = t // CHUNK
    log_c = int(math.log2(CHUNK))
    heads = range(GDN_HEADS)

    @pl.when(s == 0)
    def _():
        state_ref[...] = jnp.zeros(state_ref.shape, F32)

    r = lax.broadcasted_iota(I32, (t, t), 0)
    c = lax.broadcasted_iota(I32, (t, t), 1)
    xs = r ^ c
    same = xs < CHUNK
    incl = same & (r >= c)
    upper = same & (r <= c)
    eye = (r == c).astype(F32)
    level = [(lax.shift_right_logical(xs, lb) == 1) & ((r & (1 << lb)) != 0) for lb in range(log_c)]
    lane8 = lax.broadcasted_iota(I32, (t, H_B), 1)
    sub8 = lax.broadcasted_iota(I32, (H_B, t), 0)

    b_col, decay, eg_col, ekl_col, egl_col = [], [], [], [], []
    for j in heads:
        head = hb * GDN_HEADS + j
        g_col = jnp.sum(jnp.where(lane8 == head, g_ref[...], 0.0), axis=1, keepdims=True)
        b_col.append(jnp.sum(jnp.where(lane8 == head, beta_ref[...], 0.0), axis=1, keepdims=True))
        g_row = jnp.sum(jnp.where(sub8 == head, gt_ref[...], 0.0), axis=0, keepdims=True)
        g_rows = jnp.broadcast_to(g_row, (t, t))
        gc_col = jnp.sum(jnp.where(incl, g_rows, 0.0), axis=1, keepdims=True)
        gl_col = jnp.sum(jnp.where(same, g_rows, 0.0), axis=1, keepdims=True)
        gc_row = jnp.sum(jnp.where(upper, jnp.broadcast_to(g_col, (t, t)), 0.0), axis=0, keepdims=True)
        decay.append(jnp.where(incl, jnp.exp(jnp.minimum(gc_col - gc_row, 0.0)), 0.0))
        eg_col.append(jnp.exp(gc_col))
        ekl_col.append(jnp.exp(gl_col - gc_col))
        egl_col.append(jnp.exp(gl_col))

    lmat, aqk, rhs, q_dec, k_dec = [], [], [], [], []
    for j in heads:
        hs = slice(j * D_K, (j + 1) * D_K)
        qf, kf, vf = q_ref[:, hs], k_ref[:, hs], v_ref[:, hs]
        kb = kf * b_col[j]
        k16 = kf.astype(BF16)
        gram = _dot_nt(jnp.concatenate([kb.astype(BF16), qf.astype(BF16)], axis=0), k16)
        lmat.append(gram[:t] * decay[j])
        aqk.append(jnp.where(incl, gram[t:] * decay[j], 0.0).astype(BF16))
        rhs.append(jnp.concatenate([(vf * b_col[j]).astype(BF16), (kb * eg_col[j]).astype(BF16)], axis=1))
        q_dec.append(qf * eg_col[j])
        k_dec.append((kf * ekl_col[j]).astype(BF16))

    tinv = [eye - jnp.where(level[0], lmat[j], 0.0) for j in heads]
    for lb in range(1, log_c):
        half = 1 << lb
        t16 = [tinv[j].astype(BF16) for j in heads]
        if half < 8:
            y16 = [_dot(jnp.where(level[lb], lmat[j], 0.0).astype(BF16), t16[j]).astype(BF16) for j in heads]
            tinv = [tinv[j] - _dot(t16[j], y16[j]) for j in heads]
            continue
        pairs = t // (2 * half)
        split = lambda a: a.reshape(pairs, 2, half, t)
        lower = lambda a: split(a)[:, 1].reshape(t // 2, t)
        i = lax.broadcasted_iota(I32, (t // 2, t), 0)
        r_low = lax.shift_left(lax.shift_right_logical(i, lb), lb + 1) + half + (i & (half - 1))
        in_b = lax.shift_right_logical(r_low ^ lax.broadcasted_iota(I32, (t // 2, t), 1), lb) == 1
        y_low = [_dot(jnp.where(in_b, lower(lmat[j]), 0.0).astype(BF16), t16[j]) for j in heads]
        zeros = jnp.zeros((pairs, half, t), F32)
        y16 = [jnp.stack([zeros, y_low[j].reshape(pairs, half, t)], axis=1).reshape(t, t).astype(BF16) for j in heads]
        t_low = [lower(tinv[j]) for j in heads]
        t_low = [t_low[j] - _dot(t_low[j].astype(BF16), y16[j]) for j in heads]
        tinv = [jnp.stack([split(tinv[j])[:, 0], t_low[j].reshape(pairs, half, t)], axis=1).reshape(t, t)
                for j in heads]

    sol16 = [_dot(tinv[j].astype(BF16), rhs[j]).astype(BF16) for j in heads]
    aux = [_dot(aqk[j], sol16[j]) for j in heads]
    q_eff = [(q_dec[j] - aux[j][:, D_VB:]).astype(BF16) for j in heads]
    kw = [[_dot_tn(k_dec[j][ci * CHUNK:(ci + 1) * CHUNK], sol16[j][ci * CHUNK:(ci + 1) * CHUNK])
           for ci in range(n_chunks)] for j in heads]

    state = [state_ref[j] for j in heads]
    outs = [[] for _ in heads]
    for ci in range(n_chunks):
        rs = slice(ci * CHUNK, (ci + 1) * CHUNK)
        for j in heads:
            s16 = state[j].astype(BF16)
            both = _dot(jnp.concatenate([q_eff[j][rs], kw[j][ci][:, D_VB:].astype(BF16)], axis=0), s16)
            outs[j].append(both[:CHUNK] + aux[j][rs, :D_VB])
            state[j] = state[j] * egl_col[j][ci * CHUNK:ci * CHUNK + 1, :] + kw[j][ci][:, :D_VB] - both[CHUNK:]

    for j in heads:
        hs = slice(j * D_K, (j + 1) * D_K)
        state_ref[j] = state[j]
        o = jnp.concatenate(outs[j], axis=0)
        zz = z_ref[:, hs]
        y_ref[:, hs] = (_rms(o, go_ref[...]) * (zz * _sigmoid(zz))).astype(BF16)


def _b_gdn(q, k, v, z, beta, g, gt, go, batch, seq):
    n = batch * seq
    ns = seq // GDN_TILE
    nh = H_B // GDN_HEADS
    grid = (batch, nh, ns)
    wide = pl.BlockSpec((GDN_TILE, GDN_HEADS * D_K), lambda b, h, s: (b * ns + s, h))
    narrow = pl.BlockSpec((GDN_TILE, H_B), lambda b, h, s: (b * ns + s, 0))
    return pl.pallas_call(
        _b_gdn_kernel,
        grid=grid,
        in_specs=[wide, wide, wide, wide, narrow, narrow,
                  pl.BlockSpec((H_B, GDN_TILE), lambda b, h, s: (0, b * ns + s)),
                  _const_spec(go.shape)],
        out_specs=wide,
        out_shape=jax.ShapeDtypeStruct((n, H_B * D_VB), BF16),
        scratch_shapes=[pltpu.VMEM((GDN_HEADS, D_K, D_VB), F32)],
        compiler_params=_params("parallel", "parallel", "arbitrary"),
        name="b_gdn",
    )(q, k, v, z, beta, g, gt, go)


def _out_ple_kernel(x_ref, y_ref, p_ref, wout_ref, pn_ref, wgate_ref, wproj_ref, o_ref):
    x1 = x_ref[...] + _dot(y_ref[...], wout_ref[...])
    hn = _rms(x1, pn_ref[...]).astype(BF16)
    gate = _sigmoid(_dot(hn, wgate_ref[...]))
    o_ref[...] = x1 + gate * _dot(p_ref[...].astype(BF16), wproj_ref[...])


def _out_ple(x2, y, p2, layer, wout, pn, wgate, wproj, tile):
    n, d = x2.shape
    tok = lambda w: pl.BlockSpec((tile, w), lambda i: (i, 0))
    return pl.pallas_call(
        _out_ple_kernel,
        grid=(n // tile,),
        in_specs=[tok(d), tok(y.shape[1]),
                  pl.BlockSpec((tile, p2.shape[1]), lambda i: (layer * (n // tile) + i, 0)),
                  _const_spec(wout.shape), _const_spec(pn.shape),
                  _const_spec(wgate.shape), _const_spec(wproj.shape)],
        out_specs=tok(d),
        out_shape=jax.ShapeDtypeStruct((n, d), F32),
        compiler_params=_params("parallel"),
        name="out_ple",
    )(x2, y, p2, wout, pn, wgate, wproj)


def _t5_bucket(rel):
    max_exact = N_BUCKETS // 2
    rel = jnp.maximum(rel, 0)
    rel_f = jnp.maximum(rel, 1).astype(F32)
    log_ratio = jnp.log(rel_f / max_exact) / math.log(MAX_DISTANCE / max_exact)
    large = max_exact + (log_ratio * (N_BUCKETS - max_exact)).astype(I32)
    large = jnp.minimum(large, N_BUCKETS - 1)
    return jnp.where(rel < max_exact, rel, large)


def _bias_tiles(rel_bias):
    span = Q_TILE + K_TILE
    m = jnp.arange(span, dtype=I32)
    key_minus_query = jnp.where(m < K_TILE, m, m - span)
    scaled = rel_bias.astype(F32) * LOG2_E
    tiles = []
    for cls in range(BIAS_CLASSES):
        w = scaled[_t5_bucket(cls * Q_TILE - key_minus_query)].T
        skew = jnp.tile(w, (1, Q_TILE))[:, :Q_TILE * (span - 1)].reshape(H_A, Q_TILE, span - 1)
        tiles.append(skew[:, :, :K_TILE])
    return jnp.stack(tiles)


def kernel(x, p, norm_w, a_w_in, a_g_cq, a_w_uq, a_w_uk, a_g_q, a_g_kv, a_w_iq, a_w_uv, a_w_out, rel_bias, b_w_in, b_conv_w, b_a_log, b_dt_bias, b_g_o, b_w_out, ple_norm, ple_w_gate, ple_w_proj):
    batch, seq, d_model = x.shape
    depth = p.shape[0]
    n = batch * seq
    assert seq % Q_TILE == 0 and seq % K_TILE == 0 and seq % GDN_TILE == 0 and H_B % GDN_HEADS == 0
    assert n % PROJ_TILE == 0 and seq % CONV_TILE == 0
    x2 = x.reshape(n, d_model)
    p2 = p.reshape(depth * n, p.shape[-1])
    bt = _bias_tiles(rel_bias)
    row = lambda a: a.reshape(1, -1).astype(F32)
    n_lat = D_CQ + D_C + D_I
    b_qkv = H_B * (2 * D_K + D_VB)
    for i in range(depth):
        j = i // 2
        if i % 2 == 0:
            w_in = a_w_in[j]
            wlat = jnp.pad(w_in[:, :n_lat + H_I], ((0, 0), (0, LANES - H_I))).astype(BF16)
            wz = w_in[:, n_lat + H_I:].astype(BF16)
            qlat, qidx, wt, ckv, kidx, z = _a_proj(
                x2, row(norm_w[i]), wlat, wz, row(a_g_cq[j]), row(a_g_kv[j]),
                a_w_uq[j].reshape(D_CQ, H_A * D_NOPE).astype(BF16), a_w_uk[j].astype(BF16), row(a_g_q[j]),
                a_w_iq[j].reshape(D_CQ, H_I * D_I).astype(BF16), tile=PROJ_TILE)
            y = _a_attn(qidx, wt, kidx, qlat, ckv, z, bt, a_w_uv[j].astype(BF16), batch, seq)
            w_out = a_w_out[j]
        else:
            w_in = b_w_in[j]
            wqkv = w_in[:, :b_qkv].astype(BF16)
            w_ba = w_in[:, b_qkv:b_qkv + 2 * H_B]
            wba = jnp.pad(w_ba, ((0, 0), (0, LANES - 2 * H_B))).astype(BF16)
            wz = w_in[:, b_qkv + 2 * H_B:].astype(BF16)
            q, k, v, beta, g, gt, z = _b_proj(
                x2, row(norm_w[i]), wqkv, wba, w_ba.T.astype(BF16), wz, b_conv_w[j].astype(F32),
                b_a_log[j].astype(F32), b_dt_bias[j].astype(F32), batch, seq, tile=CONV_TILE)
            y = _b_gdn(q, k, v, z, beta, g, gt, row(b_g_o[j]), batch, seq)
            w_out = b_w_out[j]
        x2 = _out_ple(x2, y, p2, i, w_out.astype(BF16), row(ple_norm[i]), ple_w_gate[i].astype(BF16),
                      ple_w_proj[i].astype(BF16), tile=PROJ_TILE)
    return x2.reshape(batch, seq, d_model)
```

```python
import functools
import math

import jax
import jax.numpy as jnp
from jax import lax
from jax.experimental import pallas as pl
from jax.experimental.pallas import tpu as pltpu

F32 = jnp.float32
BF16 = jnp.bfloat16
I32 = jnp.int32
I16 = jnp.int16

EPS = 1e-6
NEG_INF = -1e30
LOG2_E = math.log2(math.e)
INT_MIN = -(2 ** 31)
HALF_BITS = 16
I16_MIN = -(1 << (HALF_BITS - 1))

LANES = 128
PACKED_ROWS = 16
VMEM_LIMIT_BYTES = 56 * 1024 * 1024

H_A = 8
D_C = 256
D_CQ = 256
D_I = 128
H_I = 8
D_NOPE = 128
D_V = 128
TOPK_MAX = 256
N_BUCKETS = 32
MAX_DISTANCE = 128
Q_TILE = 256
K_TILE = 256
BIAS_CLASSES = -(-(MAX_DISTANCE + K_TILE - 1) // Q_TILE) + 1
H_B = 8
D_K = 128
D_VB = 128
CONV_W = 4
CHUNK = 64
GDN_TILE = 256
GDN_HEADS = 8
PROJ_TILE = 1024
CONV_TILE = 512


def _dot(a, b):
    return jnp.dot(a, b, preferred_element_type=F32)


def _dot_nt(a, b):
    return lax.dot_general(a, b, (((1,), (1,)), ((), ())), preferred_element_type=F32)


def _dot_tn(a, b):
    return lax.dot_general(a, b, (((0,), (0,)), ((), ())), preferred_element_type=F32)


def _rms(x, gain=None):
    y = x * lax.rsqrt(jnp.mean(x * x, axis=-1, keepdims=True) + EPS)
    return y if gain is None else y * gain


def _sigmoid(x):
    return 1.0 / (1.0 + jnp.exp2(x * -LOG2_E))


def _softplus(x):
    return jnp.maximum(x, 0.0) + jnp.log1p(jnp.exp(-jnp.abs(x)))


def _params(*semantics):
    return pltpu.CompilerParams(dimension_semantics=semantics, vmem_limit_bytes=VMEM_LIMIT_BYTES)


def _const_spec(shape):
    nd = len(shape)
    return pl.BlockSpec(shape, lambda *_: (0,) * nd)


def _a_proj_kernel(x_ref, nw_ref, wlat_ref, wz_ref, gcq_ref, gkv_ref, wuq_ref, wuk_ref, gq_ref, wiq_ref,
                   qlat_ref, qidx_ref, wt_ref, ckv_ref, kidx_ref, z_ref):
    h = _rms(x_ref[...], nw_ref[...]).astype(BF16)
    lat = _dot(h, wlat_ref[...])
    cq = _rms(lat[:, :D_CQ], gcq_ref[...]).astype(BF16)
    ckv_ref[...] = _rms(lat[:, D_CQ:D_CQ + D_C], gkv_ref[...]).astype(BF16)
    n_lat = D_CQ + D_C + D_I
    kidx_ref[...] = _rms(lat[:, D_CQ + D_C:n_lat]).astype(BF16)
    wt_ref[...] = lat[:, n_lat:].T[:H_I] * (H_I ** -0.5)
    z_ref[...] = _dot(h, wz_ref[...])
    qn = _dot(cq, wuq_ref[...]).astype(BF16)
    for hh in range(H_A):
        ql = _dot(qn[:, hh * D_NOPE:(hh + 1) * D_NOPE], wuk_ref[hh])
        qlat_ref[hh] = (_rms(ql, gq_ref[...]) * (D_C ** -0.5 * LOG2_E)).astype(BF16)
    qi = _dot(cq, wiq_ref[...]) * (D_I ** -0.5)
    for hh in range(H_I):
        qidx_ref[hh] = qi[:, hh * D_I:(hh + 1) * D_I].astype(BF16)


def _a_proj(x2, nw, wlat, wz, gcq, gkv, wuq, wuk, gq, wiq, tile):
    n, d = x2.shape
    grid = (n // tile,)
    tok = lambda w: pl.BlockSpec((tile, w), lambda i: (i, 0))
    return pl.pallas_call(
        _a_proj_kernel,
        grid=grid,
        in_specs=[tok(d), _const_spec(nw.shape), _const_spec(wlat.shape),
                  _const_spec(wz.shape), _const_spec(gcq.shape), _const_spec(gkv.shape), _const_spec(wuq.shape),
                  _const_spec(wuk.shape), _const_spec(gq.shape), _const_spec(wiq.shape)],
        out_specs=[pl.BlockSpec((H_A, tile, D_C), lambda i: (0, i, 0)),
                   pl.BlockSpec((H_I, tile, D_I), lambda i: (0, i, 0)),
                   pl.BlockSpec((H_I, tile), lambda i: (0, i)),
                   tok(D_C), tok(D_I), tok(H_A * D_V)],
        out_shape=[jax.ShapeDtypeStruct((H_A, n, D_C), BF16),
                   jax.ShapeDtypeStruct((H_I, n, D_I), BF16),
                   jax.ShapeDtypeStruct((H_I, n), F32),
                   jax.ShapeDtypeStruct((n, D_C), BF16),
                   jax.ShapeDtypeStruct((n, D_I), BF16),
                   jax.ShapeDtypeStruct((n, H_A * D_V), F32)],
        compiler_params=_params("parallel"),
        name="a_proj",
    )(x2, nw, wlat, wz, gcq, gkv, wuq, wuk, gq, wiq)


def _a_attn_kernel(qidx_ref, wt_ref, kidx_ref, qlat_ref, ckv_ref, z_ref, bt_ref, wuv_ref,
                   y_ref, keys_ref, hi_ref, lo_ref, lg_ref, mx_ref, ss_ref, oacc_ref, *, topk, n_q):
    qb = pl.program_id(1)
    n_kt = qb + 1
    rows = H_A * Q_TILE
    half = K_TILE // 2

    def for_tile_groups(body):
        def quad(i, carry):
            body(4 * i, 4)
            return carry

        n_quads = lax.shift_right_logical(n_kt, 2)
        lax.fori_loop(0, n_quads, quad, 0)

        @pl.when((n_kt & 2) == 2)
        def _():
            body(4 * n_quads, 2)

        @pl.when((n_kt & 1) == 1)
        def _():
            body(n_kt - 1, 1)

    def for_tile_groups_init(body):
        odd = (n_kt & 1) == 1

        @pl.when(odd)
        def _():
            body(0, 1, True)

        @pl.when(jnp.logical_not(odd))
        def _():
            body(0, 2, True)

        start = jnp.where(odd, 1, 2)

        def pair(i, carry):
            body(start + 2 * i, 2, False)
            return carry

        lax.fori_loop(0, lax.shift_right_logical(n_kt - start, 1), pair, 0)

    def idx_tiles(c0, cnt):
        k0 = pl.multiple_of(c0 * K_TILE, K_TILE)
        qidx = qidx_ref[...].reshape(H_I * Q_TILE, D_I)
        lg = _dot_nt(kidx_ref[pl.ds(k0, cnt * K_TILE), :], qidx)
        sc = jnp.maximum(lg[:, :Q_TILE], 0.0) * wt_ref[0:1, :]
        for hh in range(1, H_I):
            sc = sc + jnp.maximum(lg[:, hh * Q_TILE:(hh + 1) * Q_TILE], 0.0) * wt_ref[hh:hh + 1, :]
        bits = lax.bitcast_convert_type(sc, I32)
        sign = lax.shift_right_arithmetic(bits, 31)
        skey = (bits ^ (sign & 0x7FFFFFFF)) - sign
        key_pos = k0 + lax.broadcasted_iota(I32, (cnt * K_TILE, Q_TILE), 0)
        q_pos = qb * Q_TILE + lax.broadcasted_iota(I32, (cnt * K_TILE, Q_TILE), 1)
        skey = jnp.where(key_pos <= q_pos, skey, INT_MIN)
        hi = lax.shift_right_arithmetic(skey, HALF_BITS).astype(I16)
        lo = ((skey & ((1 << HALF_BITS) - 1)) + I16_MIN).astype(I16)
        for u in range(cnt):
            ks = slice(u * K_TILE, (u + 1) * K_TILE)
            keys_ref[c0 + u] = skey[ks]
            hi_ref[c0 + u] = hi[ks].reshape(K_TILE // PACKED_ROWS, PACKED_ROWS, Q_TILE)
            lo_ref[c0 + u] = lo[ks].reshape(K_TILE // PACKED_ROWS, PACKED_ROWS, Q_TILE)

    for_tile_groups(idx_tiles)

    one = jnp.ones((), BF16)
    zero = jnp.zeros((), BF16)

    def threshold(n_tiles):
        def count(ref16, bound16, strict):
            acc = jnp.zeros((PACKED_ROWS, Q_TILE), BF16)
            for c in range(n_tiles):
                tile = ref16[c]
                hit = jnp.where(tile > bound16[None] if strict else tile >= bound16[None], one, zero)
                parts = [hit[u] for u in range(K_TILE // PACKED_ROWS)]
                while len(parts) > 1:
                    parts = [parts[u] + parts[u + 1] for u in range(0, len(parts), 2)]
                acc = acc + parts[0]
            return jnp.sum(acc.astype(F32), axis=0, keepdims=True)

        def kth_largest(ref16, need, n_all):
            def bit_body(i, carry):
                thr, n_ge = carry
                cand = thr + lax.shift_left(jnp.int32(1), HALF_BITS - 1 - i)
                n_cand = count(ref16, cand.astype(I16), False)
                ok = n_cand >= need
                return jnp.where(ok, cand, thr), jnp.where(ok, n_cand, n_ge)

            return lax.fori_loop(0, HALF_BITS, bit_body, (jnp.full((PACKED_ROWS, Q_TILE), I16_MIN, I32), n_all))

        n_all = jnp.full((1, Q_TILE), float(n_tiles * K_TILE), F32)
        hi_thr, n_ge_hi = kth_largest(hi_ref, float(topk), n_all)
        hi_thr16 = hi_thr.astype(I16)
        n_gt_hi = count(hi_ref, hi_thr16, True)
        for c in range(n_tiles):
            lo_ref[c] = jnp.where(hi_ref[c] == hi_thr16[None], lo_ref[c], jnp.int16(I16_MIN))
        lo_thr, n_ge_lo = kth_largest(lo_ref, float(topk) - n_gt_hi, n_ge_hi - n_gt_hi)
        thr = hi_thr[0:1] * (1 << HALF_BITS) + (lo_thr[0:1] - I16_MIN)
        thr = jnp.maximum(thr, INT_MIN + 1)
        return thr, n_gt_hi + n_ge_lo - float(topk)

    thr, excess = lax.switch(qb, [functools.partial(threshold, n) for n in range(1, n_q + 1)])

    @pl.when(jnp.max(excess) > 0.0)
    def _():
        kr = lax.broadcasted_iota(I32, (K_TILE, K_TILE), 0)
        kc = lax.broadcasted_iota(I32, (K_TILE, K_TILE), 1)
        later = jnp.where(kc >= kr, 1.0, 0.0).astype(BF16)

        def tie_body(i, seen):
            c = n_kt - 1 - i
            keys = keys_ref[c]
            tie = keys == thr
            rank = _dot(later, jnp.where(tie, 1.0, 0.0).astype(BF16)) + seen
            keys_ref[c] = jnp.where(tie & (rank <= excess), thr - 1, keys)
            return rank[0:1]

        lax.fori_loop(0, n_kt, tie_body, jnp.zeros((1, Q_TILE), F32))

    def qk_tiles(c0, cnt, init):
        k0 = pl.multiple_of(c0 * K_TILE, K_TILE)
        lg = _dot_nt(qlat_ref[...].reshape(rows, D_C), ckv_ref[pl.ds(k0, cnt * K_TILE), :])
        for u in range(cnt):
            c = c0 + u
            mb = jnp.where(keys_ref[c] >= thr, 0.0, NEG_INF).T
            t_idx = jnp.minimum(qb - c, BIAS_CLASSES - 1)
            for hh in range(H_A):
                l = lg[hh * Q_TILE:(hh + 1) * Q_TILE, u * K_TILE:(u + 1) * K_TILE] + (mb + bt_ref[t_idx, hh])
                lg_ref[c, hh] = l
                top = jnp.maximum(l[:, :half], l[:, half:])
                mx_ref[hh] = top if (init and u == 0) else jnp.maximum(mx_ref[hh], top)

    for_tile_groups_init(qk_tiles)

    m = jnp.max(mx_ref[...], axis=-1, keepdims=True)

    def pv_tiles(c0, cnt, init):
        k0 = pl.multiple_of(c0 * K_TILE, K_TILE)
        ps = [jnp.exp2(lg_ref[c0 + u] - m) for u in range(cnt)]
        part = ps[0][:, :, :half] + ps[0][:, :, half:]
        for p in ps[1:]:
            part = part + (p[:, :, :half] + p[:, :, half:])
        p16 = jnp.concatenate([p.reshape(rows, K_TILE).astype(BF16) for p in ps], axis=1)
        pv = _dot(p16, ckv_ref[pl.ds(k0, cnt * K_TILE), :])
        if init:
            ss_ref[...] = part
            oacc_ref[...] = pv
        else:
            ss_ref[...] += part
            oacc_ref[...] += pv

    for_tile_groups_init(pv_tiles)
    inv_denom = 1.0 / jnp.sum(ss_ref[...], axis=-1, keepdims=True)

    for hh in range(H_A):
        oh = _dot(oacc_ref[hh * Q_TILE:(hh + 1) * Q_TILE, :].astype(BF16), wuv_ref[hh])
        zz = z_ref[:, hh * D_V:(hh + 1) * D_V]
        y_ref[:, hh * D_V:(hh + 1) * D_V] = (oh * inv_denom[hh] * (zz * _sigmoid(zz))).astype(BF16)


def _a_attn(qidx, wt, kidx, qlat, ckv, z, bt, wuv, batch, seq):
    n = batch * seq
    nq = seq // Q_TILE
    n_kt = seq // K_TILE
    topk = min(TOPK_MAX, seq // 4)
    grid = (batch, nq)
    kern = functools.partial(_a_attn_kernel, topk=topk, n_q=nq)
    return pl.pallas_call(
        kern,
        grid=grid,
        in_specs=[pl.BlockSpec((H_I, Q_TILE, D_I), lambda b, q: (0, b * nq + q, 0)),
                  pl.BlockSpec((H_I, Q_TILE), lambda b, q: (0, b * nq + q)),
                  pl.BlockSpec((seq, D_I), lambda b, q: (b, 0)),
                  pl.BlockSpec((H_A, Q_TILE, D_C), lambda b, q: (0, b * nq + q, 0)),
                  pl.BlockSpec((seq, D_C), lambda b, q: (b, 0)),
                  pl.BlockSpec((Q_TILE, H_A * D_V), lambda b, q: (b * nq + q, 0)),
                  pl.BlockSpec(bt.shape, lambda b, q: (0, 0, 0, 0), pipeline_mode=pl.Buffered(1)),
                  pl.BlockSpec(wuv.shape, lambda b, q: (0, 0, 0), pipeline_mode=pl.Buffered(1))],
        out_specs=pl.BlockSpec((Q_TILE, H_A * D_V), lambda b, q: (b * nq + q, 0)),
        out_shape=jax.ShapeDtypeStruct((n, H_A * D_V), BF16),
        scratch_shapes=[pltpu.VMEM((n_kt, K_TILE, Q_TILE), I32),
                        pltpu.VMEM((n_kt, K_TILE // PACKED_ROWS, PACKED_ROWS, Q_TILE), I16),
                        pltpu.VMEM((n_kt, K_TILE // PACKED_ROWS, PACKED_ROWS, Q_TILE), I16),
                        pltpu.VMEM((n_kt, H_A, Q_TILE, K_TILE), F32),
                        pltpu.VMEM((H_A, Q_TILE, K_TILE // 2), F32),
                        pltpu.VMEM((H_A, Q_TILE, K_TILE // 2), F32),
                        pltpu.VMEM((H_A * Q_TILE, D_C), F32)],
        compiler_params=_params("parallel", "arbitrary"),
        name="a_attn",
    )(qidx, wt, kidx, qlat, ckv, z, bt, wuv)


def _b_proj_kernel(x_ref, nw_ref, wqkv_ref, wba_ref, wbat_ref, wz_ref, cw_ref, alog_ref, dtb_ref, alogc_ref, dtbc_ref,
                   q_ref, k_ref, v_ref, beta_ref, g_ref, gt_ref, z_ref, buf_ref, *, tile):
    s = pl.program_id(1)
    h = _rms(x_ref[...], nw_ref[...]).astype(BF16)
    z_ref[...] = _dot(h, wz_ref[...])
    ba = _dot(h, wba_ref[...])
    beta_ref[...] = _sigmoid(ba[:, :H_B])
    g_ref[...] = -jnp.exp(alog_ref[...]) * _softplus(ba[:, H_B:2 * H_B] + dtb_ref[...])
    bat = _dot_nt(wbat_ref[...], h)
    gt_ref[...] = -jnp.exp(alogc_ref[...]) * _softplus(bat[H_B:, :] + dtbc_ref[...])

    @pl.when(s == 0)
    def _():
        buf_ref[0:8, :] = jnp.zeros((8, buf_ref.shape[1]), F32)

    width = H_B * D_K
    for sec, out_ref in enumerate((q_ref, k_ref, v_ref)):
        cols = slice(sec * width, (sec + 1) * width)
        pre = _dot(h, wqkv_ref[:, cols])
        buf_ref[8:8 + tile, cols] = pre
        acc = pre * cw_ref[CONV_W - 1:CONV_W, cols]
        for w in range(CONV_W - 1):
            acc = acc + buf_ref[8 - (CONV_W - 1) + w:8 - (CONV_W - 1) + w + tile, cols] * cw_ref[w:w + 1, cols]
        buf_ref[0:8, cols] = buf_ref[tile:tile + 8, cols]
        y = acc * _sigmoid(acc)
        if sec == 2:
            out_ref[...] = y
        else:
            scale = (D_K ** -0.5) if sec == 0 else 1.0
            for hh in range(H_B):
                yy = y[:, hh * D_K:(hh + 1) * D_K]
                nrm = lax.rsqrt(jnp.sum(yy * yy, axis=-1, keepdims=True) + EPS)
                out_ref[:, hh * D_K:(hh + 1) * D_K] = yy * (nrm * scale)


def _b_proj(x2, nw, wqkv, wba, wbat, wz, cw, alog, dtb, batch, seq, tile):
    n, d = x2.shape
    ns = seq // tile
    grid = (batch, ns)
    tok = lambda w: pl.BlockSpec((tile, w), lambda b, s: (b * ns + s, 0))
    width = H_B * D_K
    kern = functools.partial(_b_proj_kernel, tile=tile)
    return pl.pallas_call(
        kern,
        grid=grid,
        in_specs=[tok(d), _const_spec(nw.shape), _const_spec(wqkv.shape), _const_spec(wba.shape),
                  _const_spec(wbat.shape), _const_spec(wz.shape), _const_spec(cw.shape),
                  _const_spec((1, H_B)), _const_spec((1, H_B)), _const_spec((H_B, 1)), _const_spec((H_B, 1))],
        out_specs=[tok(width), tok(width), tok(width), tok(H_B), tok(H_B),
                   pl.BlockSpec((H_B, tile), lambda b, s: (0, b * ns + s)), tok(width)],
        out_shape=[jax.ShapeDtypeStruct((n, width), F32)] * 3
        + [jax.ShapeDtypeStruct((n, H_B), F32)] * 2
        + [jax.ShapeDtypeStruct((H_B, n), F32), jax.ShapeDtypeStruct((n, width), F32)],
        scratch_shapes=[pltpu.VMEM((tile + 8, 3 * width), F32)],
        compiler_params=_params("parallel", "arbitrary"),
        name="b_proj",
    )(x2, nw, wqkv, wba, wbat, wz, cw, alog.reshape(1, H_B), dtb.reshape(1, H_B),
      alog.reshape(H_B, 1), dtb.reshape(H_B, 1))


def _b_gdn_kernel(q_ref, k_ref, v_ref, z_ref, beta_ref, g_ref, gt_ref, go_ref, y_ref, state_ref):
    hb = pl.program_id(1)
    s = pl.program_id(2)
    t = GDN_TILE
    n_chunks = t // CHUNK
    log_c = int(math.log2(CHUNK))
    heads = range(GDN_HEADS)

    @pl.when(s == 0)
    def _():
        state_ref[...] = jnp.zeros(state_ref.shape, F32)

    r = lax.broadcasted_iota(I32, (t, t), 0)
    c = lax.broadcasted_iota(I32, (t, t), 1)
    xs = r ^ c
    same = xs < CHUNK
    incl = same & (r >= c)
    upper = same & (r <= c)
    eye = (r == c).astype(F32)
    level = [(lax.shift_right_logical(xs, lb) == 1) & ((r & (1 << lb)) != 0) for lb in range(log_c)]
    lane8 = lax.broadcasted_iota(I32, (t, H_B), 1)
    sub8 = lax.broadcasted_iota(I32, (H_B, t), 0)

    b_col, decay, eg_col, ekl_col, egl_col = [], [], [], [], []
    for j in heads:
        head = hb * GDN_HEADS + j
        g_col = jnp.sum(jnp.where(lane8 == head, g_ref[...], 0.0), axis=1, keepdims=True)
        b_col.append(jnp.sum(jnp.where(lane8 == head, beta_ref[...], 0.0), axis=1, keepdims=True))
        g_row = jnp.sum(jnp.where(sub8 == head, gt_ref[...], 0.0), axis=0, keepdims=True)
        g_rows = jnp.broadcast_to(g_row, (t, t))
        gc_col = jnp.sum(jnp.where(incl, g_rows, 0.0), axis=1, keepdims=True)
        gl_col = jnp.sum(jnp.where(same, g_rows, 0.0), axis=1, keepdims=True)
        gc_row = jnp.sum(jnp.where(upper, jnp.broadcast_to(g_col, (t, t)), 0.0), axis=0, keepdims=True)
        decay.append(jnp.where(incl, jnp.exp(jnp.minimum(gc_col - gc_row, 0.0)), 0.0))
        eg_col.append(jnp.exp(gc_col))
        ekl_col.append(jnp.exp(gl_col - gc_col))
        egl_col.append(jnp.exp(gl_col))

    lmat, aqk, rhs, q_dec, k_dec = [], [], [], [], []
    for j in heads:
        hs = slice(j * D_K, (j + 1) * D_K)
        qf, kf, vf = q_ref[:, hs], k_ref[:, hs], v_ref[:, hs]
        kb = kf * b_col[j]
        k16 = kf.astype(BF16)
        gram = _dot_nt(jnp.concatenate([kb.astype(BF16), qf.astype(BF16)], axis=0), k16)
        lmat.append(gram[:t] * decay[j])
        aqk.append(jnp.where(incl, gram[t:] * decay[j], 0.0).astype(BF16))
        rhs.append(jnp.concatenate([(vf * b_col[j]).astype(BF16), (kb * eg_col[j]).astype(BF16)], axis=1))
        q_dec.append(qf * eg_col[j])
        k_dec.append((kf * ekl_col[j]).astype(BF16))

    tinv = [eye - jnp.where(level[0], lmat[j], 0.0) for j in heads]
    for lb in range(1, log_c):
        half = 1 << lb
        t16 = [tinv[j].astype(BF16) for j in heads]
        if half < 8:
            y16 = [_dot(jnp.where(level[lb], lmat[j], 0.0).astype(BF16), t16[j]).astype(BF16) for j in heads]
            tinv = [tinv[j] - _dot(t16[j], y16[j]) for j in heads]
            continue
        pairs = t // (2 * half)
        split = lambda a: a.reshape(pairs, 2, half, t)
        lower = lambda a: split(a)[:, 1].reshape(t // 2, t)
        i = lax.broadcasted_iota(I32, (t // 2, t), 0)
        r_low = lax.shift_left(lax.shift_right_logical(i, lb), lb + 1) + half + (i & (half - 1))
        in_b = lax.shift_right_logical(r_low ^ lax.broadcasted_iota(I32, (t // 2, t), 1), lb) == 1
        y_low = [_dot(jnp.where(in_b, lower(lmat[j]), 0.0).astype(BF16), t16[j]) for j in heads]
        zeros = jnp.zeros((pairs, half, t), F32)
        y16 = [jnp.stack([zeros, y_low[j].reshape(pairs, half, t)], axis=1).reshape(t, t).astype(BF16) for j in heads]
        t_low = [lower(tinv[j]) for j in heads]
        t_low = [t_low[j] - _dot(t_low[j].astype(BF16), y16[j]) for j in heads]
        tinv = [jnp.stack([split(tinv[j])[:, 0], t_low[j].reshape(pairs, half, t)], axis=1).reshape(t, t)
                for j in heads]

    sol16 = [_dot(tinv[j].astype(BF16), rhs[j]).astype(BF16) for j in heads]
    aux = [_dot(aqk[j], sol16[j]) for j in heads]
    q_eff = [(q_dec[j] - aux[j][:, D_VB:]).astype(BF16) for j in heads]
    kw = [[_dot_tn(k_dec[j][ci * CHUNK:(ci + 1) * CHUNK], sol16[j][ci * CHUNK:(ci + 1) * CHUNK])
           for ci in range(n_chunks)] for j in heads]

    state = [state_ref[j] for j in heads]
    outs = [[] for _ in heads]
    for ci in range(n_chunks):
        rs = slice(ci * CHUNK, (ci + 1) * CHUNK)
        for j in heads:
            s16 = state[j].astype(BF16)
            both = _dot(jnp.concatenate([q_eff[j][rs], kw[j][ci][:, D_VB:].astype(BF16)], axis=0), s16)
            outs[j].append(both[:CHUNK] + aux[j][rs, :D_VB])
            state[j] = state[j] * egl_col[j][ci * CHUNK:ci * CHUNK + 1, :] + kw[j][ci][:, :D_VB] - both[CHUNK:]

    for j in heads:
        hs = slice(j * D_K, (j + 1) * D_K)
        state_ref[j] = state[j]
        o = jnp.concatenate(outs[j], axis=0)
        zz = z_ref[:, hs]
        y_ref[:, hs] = (_rms(o, go_ref[...]) * (zz * _sigmoid(zz))).astype(BF16)


def _b_gdn(q, k, v, z, beta, g, gt, go, batch, seq):
    n = batch * seq
    ns = seq // GDN_TILE
    nh = H_B // GDN_HEADS
    grid = (batch, nh, ns)
    wide = pl.BlockSpec((GDN_TILE, GDN_HEADS * D_K), lambda b, h, s: (b * ns + s, h))
    narrow = pl.BlockSpec((GDN_TILE, H_B), lambda b, h, s: (b * ns + s, 0))
    return pl.pallas_call(
        _b_gdn_kernel,
        grid=grid,
        in_specs=[wide, wide, wide, wide, narrow, narrow,
                  pl.BlockSpec((H_B, GDN_TILE), lambda b, h, s: (0, b * ns + s)),
                  _const_spec(go.shape)],
        out_specs=wide,
        out_shape=jax.ShapeDtypeStruct((n, H_B * D_VB), BF16),
        scratch_shapes=[pltpu.VMEM((GDN_HEADS, D_K, D_VB), F32)],
        compiler_params=_params("parallel", "parallel", "arbitrary"),
        name="b_gdn",
    )(q, k, v, z, beta, g, gt, go)


def _out_ple_kernel(x_ref, y_ref, p_ref, wout_ref, pn_ref, wgate_ref, wproj_ref, o_ref):
    x1 = x_ref[...] + _dot(y_ref[...], wout_ref[...])
    hn = _rms(x1, pn_ref[...]).astype(BF16)
    gate = _sigmoid(_dot(hn, wgate_ref[...]))
    o_ref[...] = x1 + gate * _dot(p_ref[...].astype(BF16), wproj_ref[...])


def _out_ple(x2, y, p2, layer, wout, pn, wgate, wproj, tile):
    n, d = x2.shape
    tok = lambda w: pl.BlockSpec((tile, w), lambda i: (i, 0))
    return pl.pallas_call(
        _out_ple_kernel,
        grid=(n // tile,),
        in_specs=[tok(d), tok(y.shape[1]),
                  pl.BlockSpec((tile, p2.shape[1]), lambda i: (layer * (n // tile) + i, 0)),
                  _const_spec(wout.shape), _const_spec(pn.shape),
                  _const_spec(wgate.shape), _const_spec(wproj.shape)],
        out_specs=tok(d),
        out_shape=jax.ShapeDtypeStruct((n, d), F32),
        compiler_params=_params("parallel"),
        name="out_ple",
    )(x2, y, p2, wout, pn, wgate, wproj)


def _t5_bucket(rel):
    max_exact = N_BUCKETS // 2
    rel = jnp.maximum(rel, 0)
    rel_f = jnp.maximum(rel, 1).astype(F32)
    log_ratio = jnp.log(rel_f / max_exact) / math.log(MAX_DISTANCE / max_exact)
    large = max_exact + (log_ratio * (N_BUCKETS - max_exact)).astype(I32)
    large = jnp.minimum(large, N_BUCKETS - 1)
    return jnp.where(rel < max_exact, rel, large)


def _bias_tiles(rel_bias):
    span = Q_TILE + K_TILE
    m = jnp.arange(span, dtype=I32)
    key_minus_query = jnp.where(m < K_TILE, m, m - span)
    scaled = rel_bias.astype(F32) * LOG2_E
    tiles = []
    for cls in range(BIAS_CLASSES):
        w = scaled[_t5_bucket(cls * Q_TILE - key_minus_query)].T
        skew = jnp.tile(w, (1, Q_TILE))[:, :Q_TILE * (span - 1)].reshape(H_A, Q_TILE, span - 1)
        tiles.append(skew[:, :, :K_TILE])
    return jnp.stack(tiles)


def kernel(x, p, norm_w, a_w_in, a_g_cq, a_w_uq, a_w_uk, a_g_q, a_g_kv, a_w_iq, a_w_uv, a_w_out, rel_bias, b_w_in, b_conv_w, b_a_log, b_dt_bias, b_g_o, b_w_out, ple_norm, ple_w_gate, ple_w_proj):
    batch, seq, d_model = x.shape
    depth = p.shape[0]
    n = batch * seq
    assert seq % Q_TILE == 0 and seq % K_TILE == 0 and seq % GDN_TILE == 0 and H_B % GDN_HEADS == 0
    assert n % PROJ_TILE == 0 and seq % CONV_TILE == 0
    x2 = x.reshape(n, d_model)
    p2 = p.reshape(depth * n, p.shape[-1])
    bt = _bias_tiles(rel_bias)
    row = lambda a: a.reshape(1, -1).astype(F32)
    n_lat = D_CQ + D_C + D_I
    b_qkv = H_B * (2 * D_K + D_VB)
    for i in range(depth):
        j = i // 2
        if i % 2 == 0:
            w_in = a_w_in[j]
            wlat = jnp.pad(w_in[:, :n_lat + H_I], ((0, 0), (0, LANES - H_I))).astype(BF16)
            wz = w_in[:, n_lat + H_I:].astype(BF16)
            qlat, qidx, wt, ckv, kidx, z = _a_proj(
                x2, row(norm_w[i]), wlat, wz, row(a_g_cq[j]), row(a_g_kv[j]),
                a_w_uq[j].reshape(D_CQ, H_A * D_NOPE).astype(BF16), a_w_uk[j].astype(BF16), row(a_g_q[j]),
                a_w_iq[j].reshape(D_CQ, H_I * D_I).astype(BF16), tile=PROJ_TILE)
            y = _a_attn(qidx, wt, kidx, qlat, ckv, z, bt, a_w_uv[j].astype(BF16), batch, seq)
            w_out = a_w_out[j]
        else:
            w_in = b_w_in[j]
            wqkv = w_in[:, :b_qkv].astype(BF16)
            w_ba = w_in[:, b_qkv:b_qkv + 2 * H_B]
            wba = jnp.pad(w_ba, ((0, 0), (0, LANES - 2 * H_B))).astype(BF16)
            wz = w_in[:, b_qkv + 2 * H_B:].astype(BF16)
            q, k, v, beta, g, gt, z = _b_proj(
                x2, row(norm_w[i]), wqkv, wba, w_ba.T.astype(BF16), wz, b_conv_w[j].astype(F32),
                b_a_log[j].astype(F32), b_dt_bias[j].astype(F32), batch, seq, tile=CONV_TILE)
            y = _b_gdn(q, k, v, z, beta, g, gt, row(b_g_o[j]), batch, seq)
            w_out = b_w_out[j]
        x2 = _out_ple(x2, y, p2, i, w_out.astype(BF16), row(ple_norm[i]), ple_w_gate[i].astype(BF16),
                      ple_w_proj[i].astype(BF16), tile=PROJ_TILE)
    return x2.reshape(batch, seq, d_model)
```

```python
import functools
import math

import jax
import jax.numpy as jnp
from jax import lax
from jax.experimental import pallas as pl
from jax.experimental.pallas import tpu as pltpu

F32 = jnp.float32
BF16 = jnp.bfloat16
I32 = jnp.int32
I16 = jnp.int16

EPS = 1e-6
NEG_INF = -1e30
LOG2_E = math.log2(math.e)
INT_MIN = -(2 ** 31)
HALF_BITS = 16
I16_MIN = -(1 << (HALF_BITS - 1))

LANES = 128
PACKED_ROWS = 16
VMEM_LIMIT_BYTES = 56 * 1024 * 1024

H_A = 8
D_C = 256
D_CQ = 256
D_I = 128
H_I = 8
D_NOPE = 128
D_V = 128
TOPK_MAX = 256
N_BUCKETS = 32
MAX_DISTANCE = 128
Q_TILE = 256
K_TILE = 256
BIAS_CLASSES = -(-(MAX_DISTANCE + K_TILE - 1) // Q_TILE) + 1
H_B = 8
D_K = 128
D_VB = 128
CONV_W = 4
CHUNK = 64
GDN_TILE = 256
GDN_HEADS = 8
PROJ_TILE = 1024
CONV_TILE = 512


def _dot(a, b):
    return jnp.dot(a, b, preferred_element_type=F32)


def _dot_nt(a, b):
    return lax.dot_general(a, b, (((1,), (1,)), ((), ())), preferred_element_type=F32)


def _dot_tn(a, b):
    return lax.dot_general(a, b, (((0,), (0,)), ((), ())), preferred_element_type=F32)


def _rms(x, gain=None):
    y = x * lax.rsqrt(jnp.mean(x * x, axis=-1, keepdims=True) + EPS)
    return y if gain is None else y * gain


def _sigmoid(x):
    return 1.0 / (1.0 + jnp.exp2(x * -LOG2_E))


def _softplus(x):
    return jnp.maximum(x, 0.0) + jnp.log1p(jnp.exp(-jnp.abs(x)))


def _params(*semantics):
    return pltpu.CompilerParams(dimension_semantics=semantics, vmem_limit_bytes=VMEM_LIMIT_BYTES)


def _const_spec(shape):
    nd = len(shape)
    return pl.BlockSpec(shape, lambda *_: (0,) * nd)


def _a_proj_kernel(x_ref, nw_ref, wlat_ref, wz_ref, gcq_ref, gkv_ref, wuq_ref, wuk_ref, gq_ref, wiq_ref,
                   qlat_ref, qidx_ref, wt_ref, ckv_ref, kidx_ref, z_ref):
    h = _rms(x_ref[...], nw_ref[...]).astype(BF16)
    lat = _dot(h, wlat_ref[...])
    cq = _rms(lat[:, :D_CQ], gcq_ref[...]).astype(BF16)
    ckv_ref[...] = _rms(lat[:, D_CQ:D_CQ + D_C], gkv_ref[...]).astype(BF16)
    n_lat = D_CQ + D_C + D_I
    kidx_ref[...] = _rms(lat[:, D_CQ + D_C:n_lat]).astype(BF16)
    wt_ref[...] = lat[:, n_lat:].T[:H_I] * (H_I ** -0.5)
    z_ref[...] = _dot(h, wz_ref[...])
    qn = _dot(cq, wuq_ref[...]).astype(BF16)
    for hh in range(H_A):
        ql = _dot(qn[:, hh * D_NOPE:(hh + 1) * D_NOPE], wuk_ref[hh])
        qlat_ref[hh] = (_rms(ql, gq_ref[...]) * (D_C ** -0.5 * LOG2_E)).astype(BF16)
    qi = _dot(cq, wiq_ref[...]) * (D_I ** -0.5)
    for hh in range(H_I):
        qidx_ref[hh] = qi[:, hh * D_I:(hh + 1) * D_I].astype(BF16)


def _a_proj(x2, nw, wlat, wz, gcq, gkv, wuq, wuk, gq, wiq, tile):
    n, d = x2.shape
    grid = (n // tile,)
    tok = lambda w: pl.BlockSpec((tile, w), lambda i: (i, 0))
    return pl.pallas_call(
        _a_proj_kernel,
        grid=grid,
        in_specs=[tok(d), _const_spec(nw.shape), _const_spec(wlat.shape),
                  _const_spec(wz.shape), _const_spec(gcq.shape), _const_spec(gkv.shape), _const_spec(wuq.shape),
                  _const_spec(wuk.shape), _const_spec(gq.shape), _const_spec(wiq.shape)],
        out_specs=[pl.BlockSpec((H_A, tile, D_C), lambda i: (0, i, 0)),
                   pl.BlockSpec((H_I, tile, D_I), lambda i: (0, i, 0)),
                   pl.BlockSpec((H_I, tile), lambda i: (0, i)),
                   tok(D_C), tok(D_I), tok(H_A * D_V)],
        out_shape=[jax.ShapeDtypeStruct((H_A, n, D_C), BF16),
                   jax.ShapeDtypeStruct((H_I, n, D_I), BF16),
                   jax.ShapeDtypeStruct((H_I, n), F32),
                   jax.ShapeDtypeStruct((n, D_C), BF16),
                   jax.ShapeDtypeStruct((n, D_I), BF16),
                   jax.ShapeDtypeStruct((n, H_A * D_V), F32)],
        compiler_params=_params("parallel"),
        name="a_proj",
    )(x2, nw, wlat, wz, gcq, gkv, wuq, wuk, gq, wiq)


def _a_attn_kernel(qidx_ref, wt_ref, kidx_ref, qlat_ref, ckv_ref, z_ref, bt_ref, wuv_ref,
                   y_ref, keys_ref, hi_ref, lo_ref, lg_ref, mx_ref, ss_ref, oacc_ref, *, topk, n_q):
    qb = pl.program_id(1)
    n_kt = qb + 1
    rows = H_A * Q_TILE
    half = K_TILE // 2

    def for_tile_groups(body):
        def quad(i, carry):
            body(4 * i, 4)
            return carry

        n_quads = lax.shift_right_logical(n_kt, 2)
        lax.fori_loop(0, n_quads, quad, 0)

        @pl.when((n_kt & 2) == 2)
        def _():
            body(4 * n_quads, 2)

        @pl.when((n_kt & 1) == 1)
        def _():
            body(n_kt - 1, 1)

    def for_tile_groups_init(body):
        odd = (n_kt & 1) == 1

        @pl.when(odd)
        def _():
            body(0, 1, True)

        @pl.when(jnp.logical_not(odd))
        def _():
            body(0, 2, True)

        start = jnp.where(odd, 1, 2)

        def pair(i, carry):
            body(start + 2 * i, 2, False)
            return carry

        lax.fori_loop(0, lax.shift_right_logical(n_kt - start, 1), pair, 0)

    def idx_tiles(c0, cnt):
        k0 = pl.multiple_of(c0 * K_TILE, K_TILE)
        qidx = qidx_ref[...].reshape(H_I * Q_TILE, D_I)
        lg = _dot_nt(kidx_ref[pl.ds(k0, cnt * K_TILE), :], qidx)
        sc = jnp.maximum(lg[:, :Q_TILE], 0.0) * wt_ref[0:1, :]
        for hh in range(1, H_I):
            sc = sc + jnp.maximum(lg[:, hh * Q_TILE:(hh + 1) * Q_TILE], 0.0) * wt_ref[hh:hh + 1, :]
        bits = lax.bitcast_convert_type(sc, I32)
        sign = lax.shift_right_arithmetic(bits, 31)
        skey = (bits ^ (sign & 0x7FFFFFFF)) - sign
        key_pos = k0 + lax.broadcasted_iota(I32, (cnt * K_TILE, Q_TILE), 0)
        q_pos = qb * Q_TILE + lax.broadcasted_iota(I32, (cnt * K_TILE, Q_TILE), 1)
        skey = jnp.where(key_pos <= q_pos, skey, INT_MIN)
        hi = lax.shift_right_arithmetic(skey, HALF_BITS).astype(I16)
        lo = ((skey & ((1 << HALF_BITS) - 1)) + I16_MIN).astype(I16)
        for u in range(cnt):
            ks = slice(u * K_TILE, (u + 1) * K_TILE)
            keys_ref[c0 + u] = skey[ks]
            hi_ref[c0 + u] = hi[ks].reshape(K_TILE // PACKED_ROWS, PACKED_ROWS, Q_TILE)
            lo_ref[c0 + u] = lo[ks].reshape(K_TILE // PACKED_ROWS, PACKED_ROWS, Q_TILE)

    for_tile_groups(idx_tiles)

    one = jnp.ones((), BF16)
    zero = jnp.zeros((), BF16)

    def threshold(n_tiles):
        if n_tiles * Q_TILE <= topk:
            return jnp.full((1, Q_TILE), INT_MIN + 1, I32), jnp.zeros((1, Q_TILE), F32)

        def count(ref16, bound16, strict):
            acc = jnp.zeros((PACKED_ROWS, Q_TILE), BF16)
            for c in range(n_tiles):
                tile = ref16[c]
                hit = jnp.where(tile > bound16[None] if strict else tile >= bound16[None], one, zero)
                parts = [hit[u] for u in range(K_TILE // PACKED_ROWS)]
                while len(parts) > 1:
                    parts = [parts[u] + parts[u + 1] for u in range(0, len(parts), 2)]
                acc = acc + parts[0]
            return jnp.sum(acc.astype(F32), axis=0, keepdims=True)

        def kth_largest(ref16, need, n_all):
            def bit_body(i, carry):
                thr, n_ge = carry
                cand = thr + lax.shift_left(jnp.int32(1), HALF_BITS - 1 - i)
                n_cand = count(ref16, cand.astype(I16), False)
                ok = n_cand >= need
                return jnp.where(ok, cand, thr), jnp.where(ok, n_cand, n_ge)

            return lax.fori_loop(0, HALF_BITS, bit_body, (jnp.full((PACKED_ROWS, Q_TILE), I16_MIN, I32), n_all))

        n_all = jnp.full((1, Q_TILE), float(n_tiles * K_TILE), F32)
        hi_thr, n_ge_hi = kth_largest(hi_ref, float(topk), n_all)
        hi_thr16 = hi_thr.astype(I16)
        n_gt_hi = count(hi_ref, hi_thr16, True)
        for c in range(n_tiles):
            lo_ref[c] = jnp.where(hi_ref[c] == hi_thr16[None], lo_ref[c], jnp.int16(I16_MIN))
        lo_thr, n_ge_lo = kth_largest(lo_ref, float(topk) - n_gt_hi, n_ge_hi - n_gt_hi)
        thr = hi_thr[0:1] * (1 << HALF_BITS) + (lo_thr[0:1] - I16_MIN)
        thr = jnp.maximum(thr, INT_MIN + 1)
        return thr, n_gt_hi + n_ge_lo - float(topk)

    thr, excess = lax.switch(qb, [functools.partial(threshold, n) for n in range(1, n_q + 1)])

    @pl.when(jnp.max(excess) > 0.0)
    def _():
        kr = lax.broadcasted_iota(I32, (K_TILE, K_TILE), 0)
        kc = lax.broadcasted_iota(I32, (K_TILE, K_TILE), 1)
        later = jnp.where(kc >= kr, 1.0, 0.0).astype(BF16)

        def tie_body(i, seen):
            c = n_kt - 1 - i
            keys = keys_ref[c]
            tie = keys == thr
            rank = _dot(later, jnp.where(tie, 1.0, 0.0).astype(BF16)) + seen
            keys_ref[c] = jnp.where(tie & (rank <= excess), thr - 1, keys)
            return rank[0:1]

        lax.fori_loop(0, n_kt, tie_body, jnp.zeros((1, Q_TILE), F32))

    def qk_tiles(c0, cnt, init):
        k0 = pl.multiple_of(c0 * K_TILE, K_TILE)
        lg = _dot_nt(qlat_ref[...].reshape(rows, D_C), ckv_ref[pl.ds(k0, cnt * K_TILE), :])
        for u in range(cnt):
            c = c0 + u
            mb = jnp.where(keys_ref[c] >= thr, 0.0, NEG_INF).T
            t_idx = jnp.minimum(qb - c, BIAS_CLASSES - 1)
            for hh in range(H_A):
                l = lg[hh * Q_TILE:(hh + 1) * Q_TILE, u * K_TILE:(u + 1) * K_TILE] + (mb + bt_ref[t_idx, hh])
                lg_ref[c, hh] = l
                top = jnp.maximum(l[:, :half], l[:, half:])
                mx_ref[hh] = top if (init and u == 0) else jnp.maximum(mx_ref[hh], top)

    for_tile_groups_init(qk_tiles)

    m = jnp.max(mx_ref[...], axis=-1, keepdims=True)

    def pv_tiles(c0, cnt, init):
        k0 = pl.multiple_of(c0 * K_TILE, K_TILE)
        ps = [jnp.exp2(lg_ref[c0 + u] - m) for u in range(cnt)]
        part = ps[0][:, :, :half] + ps[0][:, :, half:]
        for p in ps[1:]:
            part = part + (p[:, :, :half] + p[:, :, half:])
        p16 = jnp.concatenate([p.reshape(rows, K_TILE).astype(BF16) for p in ps], axis=1)
        pv = _dot(p16, ckv_ref[pl.ds(k0, cnt * K_TILE), :])
        if init:
            ss_ref[...] = part
            oacc_ref[...] = pv
        else:
            ss_ref[...] += part
            oacc_ref[...] += pv

    for_tile_groups_init(pv_tiles)
    inv_denom = 1.0 / jnp.sum(ss_ref[...], axis=-1, keepdims=True)

    for hh in range(H_A):
        oh = _dot(oacc_ref[hh * Q_TILE:(hh + 1) * Q_TILE, :].astype(BF16), wuv_ref[hh])
        zz = z_ref[:, hh * D_V:(hh + 1) * D_V]
        y_ref[:, hh * D_V:(hh + 1) * D_V] = (oh * inv_denom[hh] * (zz * _sigmoid(zz))).astype(BF16)


def _a_attn(qidx, wt, kidx, qlat, ckv, z, bt, wuv, batch, seq):
    n = batch * seq
    nq = seq // Q_TILE
    n_kt = seq // K_TILE
    topk = min(TOPK_MAX, seq // 4)
    grid = (batch, nq)
    kern = functools.partial(_a_attn_kernel, topk=topk, n_q=nq)
    return pl.pallas_call(
        kern,
        grid=grid,
        in_specs=[pl.BlockSpec((H_I, Q_TILE, D_I), lambda b, q: (0, b * nq + q, 0)),
                  pl.BlockSpec((H_I, Q_TILE), lambda b, q: (0, b * nq + q)),
                  pl.BlockSpec((seq, D_I), lambda b, q: (b, 0)),
                  pl.BlockSpec((H_A, Q_TILE, D_C), lambda b, q: (0, b * nq + q, 0)),
                  pl.BlockSpec((seq, D_C), lambda b, q: (b, 0)),
                  pl.BlockSpec((Q_TILE, H_A * D_V), lambda b, q: (b * nq + q, 0)),
                  pl.BlockSpec(bt.shape, lambda b, q: (0, 0, 0, 0), pipeline_mode=pl.Buffered(1)),
                  pl.BlockSpec(wuv.shape, lambda b, q: (0, 0, 0), pipeline_mode=pl.Buffered(1))],
        out_specs=pl.BlockSpec((Q_TILE, H_A * D_V), lambda b, q: (b * nq + q, 0)),
        out_shape=jax.ShapeDtypeStruct((n, H_A * D_V), BF16),
        scratch_shapes=[pltpu.VMEM((n_kt, K_TILE, Q_TILE), I32),
                        pltpu.VMEM((n_kt, K_TILE // PACKED_ROWS, PACKED_ROWS, Q_TILE), I16),
                        pltpu.VMEM((n_kt, K_TILE // PACKED_ROWS, PACKED_ROWS, Q_TILE), I16),
                        pltpu.VMEM((n_kt, H_A, Q_TILE, K_TILE), F32),
                        pltpu.VMEM((H_A, Q_TILE, K_TILE // 2), F32),
                        pltpu.VMEM((H_A, Q_TILE, K_TILE // 2), F32),
                        pltpu.VMEM((H_A * Q_TILE, D_C), F32)],
        compiler_params=_params("parallel", "arbitrary"),
        name="a_attn",
    )(qidx, wt, kidx, qlat, ckv, z, bt, wuv)


def _b_proj_kernel(x_ref, nw_ref, wqkv_ref, wba_ref, wbat_ref, wz_ref, cw_ref, alog_ref, dtb_ref, alogc_ref, dtbc_ref,
                   q_ref, k_ref, v_ref, beta_ref, g_ref, gt_ref, z_ref, buf_ref, *, tile):
    s = pl.program_id(1)
    h = _rms(x_ref[...], nw_ref[...]).astype(BF16)
    z_ref[...] = _dot(h, wz_ref[...])
    ba = _dot(h, wba_ref[...])
    beta_ref[...] = _sigmoid(ba[:, :H_B])
    g_ref[...] = -jnp.exp(alog_ref[...]) * _softplus(ba[:, H_B:2 * H_B] + dtb_ref[...])
    bat = _dot_nt(wbat_ref[...], h)
    gt_ref[...] = -jnp.exp(alogc_ref[...]) * _softplus(bat[H_B:, :] + dtbc_ref[...])

    @pl.when(s == 0)
    def _():
        buf_ref[0:8, :] = jnp.zeros((8, buf_ref.shape[1]), F32)

    width = H_B * D_K
    for sec, out_ref in enumerate((q_ref, k_ref, v_ref)):
        cols = slice(sec * width, (sec + 1) * width)
        pre = _dot(h, wqkv_ref[:, cols])
        buf_ref[8:8 + tile, cols] = pre
        acc = pre * cw_ref[CONV_W - 1:CONV_W, cols]
        for w in range(CONV_W - 1):
            acc = acc + buf_ref[8 - (CONV_W - 1) + w:8 - (CONV_W - 1) + w + tile, cols] * cw_ref[w:w + 1, cols]
        buf_ref[0:8, cols] = buf_ref[tile:tile + 8, cols]
        y = acc * _sigmoid(acc)
        if sec == 2:
            out_ref[...] = y
        else:
            scale = (D_K ** -0.5) if sec == 0 else 1.0
            for hh in range(H_B):
                yy = y[:, hh * D_K:(hh + 1) * D_K]
                nrm = lax.rsqrt(jnp.sum(yy * yy, axis=-1, keepdims=True) + EPS)
                out_ref[:, hh * D_K:(hh + 1) * D_K] = yy * (nrm * scale)


def _b_proj(x2, nw, wqkv, wba, wbat, wz, cw, alog, dtb, batch, seq, tile):
    n, d = x2.shape
    ns = seq // tile
    grid = (batch, ns)
    tok = lambda w: pl.BlockSpec((tile, w), lambda b, s: (b * ns + s, 0))
    width = H_B * D_K
    kern = functools.partial(_b_proj_kernel, tile=tile)
    return pl.pallas_call(
        kern,
        grid=grid,
        in_specs=[tok(d), _const_spec(nw.shape), _const_spec(wqkv.shape), _const_spec(wba.shape),
                  _const_spec(wbat.shape), _const_spec(wz.shape), _const_spec(cw.shape),
                  _const_spec((1, H_B)), _const_spec((1, H_B)), _const_spec((H_B, 1)), _const_spec((H_B, 1))],
        out_specs=[tok(width), tok(width), tok(width), tok(H_B), tok(H_B),
                   pl.BlockSpec((H_B, tile), lambda b, s: (0, b * ns + s)), tok(width)],
        out_shape=[jax.ShapeDtypeStruct((n, width), F32)] * 3
        + [jax.ShapeDtypeStruct((n, H_B), F32)] * 2
        + [jax.ShapeDtypeStruct((H_B, n), F32), jax.ShapeDtypeStruct((n, width), F32)],
        scratch_shapes=[pltpu.VMEM((tile + 8, 3 * width), F32)],
        compiler_params=_params("parallel", "arbitrary"),
        name="b_proj",
    )(x2, nw, wqkv, wba, wbat, wz, cw, alog.reshape(1, H_B), dtb.reshape(1, H_B),
      alog.reshape(H_B, 1), dtb.reshape(H_B, 1))


def _b_gdn_kernel(q_ref, k_ref, v_ref, z_ref, beta_ref, g_ref, gt_ref, go_ref, y_ref, state_ref):
    hb = pl.program_id(1)
    s = pl.program_id(2)
    t = GDN_TILE
    n_chunks = t // CHUNK
    log_c = int(math.log2(CHUNK))
    heads = range(GDN_HEADS)

    @pl.when(s == 0)
    def _():
        state_ref[...] = jnp.zeros(state_ref.shape, F32)

    r = lax.broadcasted_iota(I32, (t, t), 0)
    c = lax.broadcasted_iota(I32, (t, t), 1)
    xs = r ^ c
    same = xs < CHUNK
    incl = same & (r >= c)
    upper = same & (r <= c)
    eye = (r == c).astype(F32)
    level = [(lax.shift_right_logical(xs, lb) == 1) & ((r & (1 << lb)) != 0) for lb in range(log_c)]
    lane8 = lax.broadcasted_iota(I32, (t, H_B), 1)
    sub8 = lax.broadcasted_iota(I32, (H_B, t), 0)

    b_col, decay, eg_col, ekl_col, egl_col = [], [], [], [], []
    for j in heads:
        head = hb * GDN_HEADS + j
        g_col = jnp.sum(jnp.where(lane8 == head, g_ref[...], 0.0), axis=1, keepdims=True)
        b_col.append(jnp.sum(jnp.where(lane8 == head, beta_ref[...], 0.0), axis=1, keepdims=True))
        g_row = jnp.sum(jnp.where(sub8 == head, gt_ref[...], 0.0), axis=0, keepdims=True)
        g_rows = jnp.broadcast_to(g_row, (t, t))
        gc_col = jnp.sum(jnp.where(incl, g_rows, 0.0), axis=1, keepdims=True)
        gl_col = jnp.sum(jnp.where(same, g_rows, 0.0), axis=1, keepdims=True)
        gc_row = jnp.sum(jnp.where(upper, jnp.broadcast_to(g_col, (t, t)), 0.0), axis=0, keepdims=True)
        decay.append(jnp.where(incl, jnp.exp(jnp.minimum(gc_col - gc_row, 0.0)), 0.0))
        eg_col.append(jnp.exp(gc_col))
        ekl_col.append(jnp.exp(gl_col - gc_col))
        egl_col.append(jnp.exp(gl_col))

    lmat, aqk, rhs, q_dec, k_dec = [], [], [], [], []
    for j in heads:
        hs = slice(j * D_K, (j + 1) * D_K)
        qf, kf, vf = q_ref[:, hs], k_ref[:, hs], v_ref[:, hs]
        kb = kf * b_col[j]
        k16 = kf.astype(BF16)
        gram = _dot_nt(jnp.concatenate([kb.astype(BF16), qf.astype(BF16)], axis=0), k16)
        lmat.append(gram[:t] * decay[j])
        aqk.append(jnp.where(incl, gram[t:] * decay[j], 0.0).astype(BF16))
        rhs.append(jnp.concatenate([(vf * b_col[j]).astype(BF16), (kb * eg_col[j]).astype(BF16)], axis=1))
        q_dec.append(qf * eg_col[j])
        k_dec.append((kf * ekl_col[j]).astype(BF16))

    tinv = [eye - jnp.where(level[0], lmat[j], 0.0) for j in heads]
    for lb in range(1, log_c):
        half = 1 << lb
        t16 = [tinv[j].astype(BF16) for j in heads]
        if half < 8:
            y16 = [_dot(jnp.where(level[lb], lmat[j], 0.0).astype(BF16), t16[j]).astype(BF16) for j in heads]
            tinv = [tinv[j] - _dot(t16[j], y16[j]) for j in heads]
            continue
        pairs = t // (2 * half)
        split = lambda a: a.reshape(pairs, 2, half, t)
        lower = lambda a: split(a)[:, 1].reshape(t // 2, t)
        i = lax.broadcasted_iota(I32, (t // 2, t), 0)
        r_low = lax.shift_left(lax.shift_right_logical(i, lb), lb + 1) + half + (i & (half - 1))
        in_b = lax.shift_right_logical(r_low ^ lax.broadcasted_iota(I32, (t // 2, t), 1), lb) == 1
        y_low = [_dot(jnp.where(in_b, lower(lmat[j]), 0.0).astype(BF16), t16[j]) for j in heads]
        zeros = jnp.zeros((pairs, half, t), F32)
        y16 = [jnp.stack([zeros, y_low[j].reshape(pairs, half, t)], axis=1).reshape(t, t).astype(BF16) for j in heads]
        t_low = [lower(tinv[j]) for j in heads]
        t_low = [t_low[j] - _dot(t_low[j].astype(BF16), y16[j]) for j in heads]
        tinv = [jnp.stack([split(tinv[j])[:, 0], t_low[j].reshape(pairs, half, t)], axis=1).reshape(t, t)
                for j in heads]

    sol16 = [_dot(tinv[j].astype(BF16), rhs[j]).astype(BF16) for j in heads]
    aux = [_dot(aqk[j], sol16[j]) for j in heads]
    q_eff = [(q_dec[j] - aux[j][:, D_VB:]).astype(BF16) for j in heads]
    kw = [[_dot_tn(k_dec[j][ci * CHUNK:(ci + 1) * CHUNK], sol16[j][ci * CHUNK:(ci + 1) * CHUNK])
           for ci in range(n_chunks)] for j in heads]

    state = [state_ref[j] for j in heads]
    outs = [[] for _ in heads]
    for ci in range(n_chunks):
        rs = slice(ci * CHUNK, (ci + 1) * CHUNK)
        for j in heads:
            s16 = state[j].astype(BF16)
            both = _dot(jnp.concatenate([q_eff[j][rs], kw[j][ci][:, D_VB:].astype(BF16)], axis=0), s16)
            outs[j].append(both[:CHUNK] + aux[j][rs, :D_VB])
            state[j] = state[j] * egl_col[j][ci * CHUNK:ci * CHUNK + 1, :] + kw[j][ci][:, :D_VB] - both[CHUNK:]

    for j in heads:
        hs = slice(j * D_K, (j + 1) * D_K)
        state_ref[j] = state[j]
        o = jnp.concatenate(outs[j], axis=0)
        zz = z_ref[:, hs]
        y_ref[:, hs] = (_rms(o, go_ref[...]) * (zz * _sigmoid(zz))).astype(BF16)


def _b_gdn(q, k, v, z, beta, g, gt, go, batch, seq):
    n = batch * seq
    ns = seq // GDN_TILE
    nh = H_B // GDN_HEADS
    grid = (batch, nh, ns)
    wide = pl.BlockSpec((GDN_TILE, GDN_HEADS * D_K), lambda b, h, s: (b * ns + s, h))
    narrow = pl.BlockSpec((GDN_TILE, H_B), lambda b, h, s: (b * ns + s, 0))
    return pl.pallas_call(
        _b_gdn_kernel,
        grid=grid,
        in_specs=[wide, wide, wide, wide, narrow, narrow,
                  pl.BlockSpec((H_B, GDN_TILE), lambda b, h, s: (0, b * ns + s)),
                  _const_spec(go.shape)],
        out_specs=wide,
        out_shape=jax.ShapeDtypeStruct((n, H_B * D_VB), BF16),
        scratch_shapes=[pltpu.VMEM((GDN_HEADS, D_K, D_VB), F32)],
        compiler_params=_params("parallel", "parallel", "arbitrary"),
        name="b_gdn",
    )(q, k, v, z, beta, g, gt, go)


def _out_ple_kernel(x_ref, y_ref, p_ref, wout_ref, pn_ref, wgate_ref, wproj_ref, o_ref, wout16, wgate16, wproj16):
    @pl.when(pl.program_id(0) == 0)
    def _():
        wout16[...] = wout_ref[...].astype(BF16)
        wgate16[...] = wgate_ref[...].astype(BF16)
        wproj16[...] = wproj_ref[...].astype(BF16)

    x1 = x_ref[...] + _dot(y_ref[...], wout16[...])
    hn = _rms(x1, pn_ref[...]).astype(BF16)
    gate = _sigmoid(_dot(hn, wgate16[...]))
    o_ref[...] = x1 + gate * _dot(p_ref[...].astype(BF16), wproj16[...])


def _out_ple(x2, y, p2, layer, wout_all, mixer_layer, pn_all, wgate_all, wproj_all, tile):
    n, d = x2.shape
    tok = lambda w: pl.BlockSpec((tile, w), lambda i: (i, 0))
    once = lambda arr, idx: pl.BlockSpec((None,) + arr.shape[1:], lambda i: (idx,) + (0,) * (arr.ndim - 1),
                                         pipeline_mode=pl.Buffered(1))
    return pl.pallas_call(
        _out_ple_kernel,
        grid=(n // tile,),
        in_specs=[tok(d), tok(y.shape[1]),
                  pl.BlockSpec((tile, p2.shape[1]), lambda i: (layer * (n // tile) + i, 0)),
                  once(wout_all, mixer_layer), once(pn_all, layer), once(wgate_all, layer), once(wproj_all, layer)],
        out_specs=tok(d),
        out_shape=jax.ShapeDtypeStruct((n, d), F32),
        scratch_shapes=[pltpu.VMEM(wout_all.shape[1:], BF16), pltpu.VMEM(wgate_all.shape[1:], BF16),
                        pltpu.VMEM(wproj_all.shape[1:], BF16)],
        compiler_params=_params("arbitrary"),
        name="out_ple",
    )(x2, y, p2, wout_all, pn_all, wgate_all, wproj_all)


def _t5_bucket(rel):
    max_exact = N_BUCKETS // 2
    rel = jnp.maximum(rel, 0)
    rel_f = jnp.maximum(rel, 1).astype(F32)
    log_ratio = jnp.log(rel_f / max_exact) / math.log(MAX_DISTANCE / max_exact)
    large = max_exact + (log_ratio * (N_BUCKETS - max_exact)).astype(I32)
    large = jnp.minimum(large, N_BUCKETS - 1)
    return jnp.where(rel < max_exact, rel, large)


def _bias_tiles(rel_bias):
    span = Q_TILE + K_TILE
    m = jnp.arange(span, dtype=I32)
    key_minus_query = jnp.where(m < K_TILE, m, m - span)
    scaled = rel_bias.astype(F32) * LOG2_E
    tiles = []
    for cls in range(BIAS_CLASSES):
        w = scaled[_t5_bucket(cls * Q_TILE - key_minus_query)].T
        skew = jnp.tile(w, (1, Q_TILE))[:, :Q_TILE * (span - 1)].reshape(H_A, Q_TILE, span - 1)
        tiles.append(skew[:, :, :K_TILE])
    return jnp.stack(tiles)


def kernel(x, p, norm_w, a_w_in, a_g_cq, a_w_uq, a_w_uk, a_g_q, a_g_kv, a_w_iq, a_w_uv, a_w_out, rel_bias, b_w_in, b_conv_w, b_a_log, b_dt_bias, b_g_o, b_w_out, ple_norm, ple_w_gate, ple_w_proj):
    batch, seq, d_model = x.shape
    depth = p.shape[0]
    n = batch * seq
    assert seq % Q_TILE == 0 and seq % K_TILE == 0 and seq % GDN_TILE == 0 and H_B % GDN_HEADS == 0
    assert n % PROJ_TILE == 0 and seq % CONV_TILE == 0
    x2 = x.reshape(n, d_model)
    p2 = p.reshape(depth * n, p.shape[-1])
    bt = _bias_tiles(rel_bias)
    row = lambda a: a.reshape(1, -1).astype(F32)
    n_lat = D_CQ + D_C + D_I
    b_qkv = H_B * (2 * D_K + D_VB)
    for i in range(depth):
        j = i // 2
        if i % 2 == 0:
            w_in = a_w_in[j]
            wlat = jnp.pad(w_in[:, :n_lat + H_I], ((0, 0), (0, LANES - H_I))).astype(BF16)
            wz = w_in[:, n_lat + H_I:].astype(BF16)
            qlat, qidx, wt, ckv, kidx, z = _a_proj(
                x2, row(norm_w[i]), wlat, wz, row(a_g_cq[j]), row(a_g_kv[j]),
                a_w_uq[j].reshape(D_CQ, H_A * D_NOPE).astype(BF16), a_w_uk[j].astype(BF16), row(a_g_q[j]),
                a_w_iq[j].reshape(D_CQ, H_I * D_I).astype(BF16), tile=PROJ_TILE)
            y = _a_attn(qidx, wt, kidx, qlat, ckv, z, bt, a_w_uv[j].astype(BF16), batch, seq)
            w_out_all = a_w_out
        else:
            w_in = b_w_in[j]
            wqkv = w_in[:, :b_qkv].astype(BF16)
            w_ba = w_in[:, b_qkv:b_qkv + 2 * H_B]
            wba = jnp.pad(w_ba, ((0, 0), (0, LANES - 2 * H_B))).astype(BF16)
            wz = w_in[:, b_qkv + 2 * H_B:].astype(BF16)
            q, k, v, beta, g, gt, z = _b_proj(
                x2, row(norm_w[i]), wqkv, wba, w_ba.T.astype(BF16), wz, b_conv_w[j].astype(F32),
                b_a_log[j].astype(F32), b_dt_bias[j].astype(F32), batch, seq, tile=CONV_TILE)
            y = _b_gdn(q, k, v, z, beta, g, gt, row(b_g_o[j]), batch, seq)
            w_out_all = b_w_out
        x2 = _out_ple(x2, y, p2, i, w_out_all, j, ple_norm.reshape(depth, 1, d_model).astype(F32), ple_w_gate,
                      ple_w_proj, tile=PROJ_TILE)
    return x2.reshape(batch, seq, d_model)
```

```python
import functools
import math

import jax
import jax.numpy as jnp
from jax import lax
from jax.experimental import pallas as pl
from jax.experimental.pallas import tpu as pltpu

F32 = jnp.float32
BF16 = jnp.bfloat16
I32 = jnp.int32
I16 = jnp.int16

EPS = 1e-6
NEG_INF = -1e30
LOG2_E = math.log2(math.e)
INT_MIN = -(2 ** 31)
HALF_BITS = 16
I16_MIN = -(1 << (HALF_BITS - 1))

LANES = 128
PACKED_ROWS = 16
VMEM_LIMIT_BYTES = 56 * 1024 * 1024

H_A = 8
D_C = 256
D_CQ = 256
D_I = 128
H_I = 8
D_NOPE = 128
D_V = 128
TOPK_MAX = 256
N_BUCKETS = 32
MAX_DISTANCE = 128
Q_TILE = 256
K_TILE = 256
BIAS_CLASSES = -(-(MAX_DISTANCE + K_TILE - 1) // Q_TILE) + 1
H_B = 8
D_K = 128
D_VB = 128
CONV_W = 4
CHUNK = 64
GDN_TILE = 256
GDN_HEADS = 8
PROJ_TILE = 1024
CONV_TILE = 512


def _dot(a, b):
    return jnp.dot(a, b, preferred_element_type=F32)


def _dot_nt(a, b):
    return lax.dot_general(a, b, (((1,), (1,)), ((), ())), preferred_element_type=F32)


def _dot_tn(a, b):
    return lax.dot_general(a, b, (((0,), (0,)), ((), ())), preferred_element_type=F32)


def _rms(x, gain=None):
    y = x * lax.rsqrt(jnp.mean(x * x, axis=-1, keepdims=True) + EPS)
    return y if gain is None else y * gain


def _sigmoid(x):
    return 1.0 / (1.0 + jnp.exp2(x * -LOG2_E))


def _softplus(x):
    return jnp.maximum(x, 0.0) + jnp.log1p(jnp.exp(-jnp.abs(x)))


def _params(*semantics):
    return pltpu.CompilerParams(dimension_semantics=semantics, vmem_limit_bytes=VMEM_LIMIT_BYTES)


def _const_spec(shape):
    nd = len(shape)
    return pl.BlockSpec(shape, lambda *_: (0,) * nd)


def _a_proj_kernel(x_ref, nw_ref, win_ref, gcq_ref, gkv_ref, wuq32_ref, wuk32_ref, gq_ref, wiq32_ref,
                   qlat_ref, qidx_ref, wt_ref, ckv_ref, kidx_ref, z_ref,
                   wlat_ref, wz_ref, wuq_ref, wuk_ref, wiq_ref):
    n_lat = D_CQ + D_C + D_I

    @pl.when(pl.program_id(0) == 0)
    def _():
        wlat_ref[...] = win_ref[:, :n_lat + LANES].astype(BF16)
        wz_ref[...] = win_ref[:, n_lat + H_I:].astype(BF16)
        wuq_ref[...] = wuq32_ref[...].astype(BF16)
        wuk_ref[...] = wuk32_ref[...].astype(BF16)
        wiq_ref[...] = wiq32_ref[...].astype(BF16)

    h = _rms(x_ref[...], nw_ref[...]).astype(BF16)
    lat = _dot(h, wlat_ref[...])
    cq = _rms(lat[:, :D_CQ], gcq_ref[...]).astype(BF16)
    ckv_ref[...] = _rms(lat[:, D_CQ:D_CQ + D_C], gkv_ref[...]).astype(BF16)
    kidx_ref[...] = _rms(lat[:, D_CQ + D_C:n_lat]).astype(BF16)
    wt_ref[...] = lat[:, n_lat:].T[:H_I] * (H_I ** -0.5)
    z_ref[...] = _dot(h, wz_ref[...])
    qn = _dot(cq, wuq_ref[...]).astype(BF16)
    for hh in range(H_A):
        ql = _dot(qn[:, hh * D_NOPE:(hh + 1) * D_NOPE], wuk_ref[hh])
        qlat_ref[hh] = (_rms(ql, gq_ref[...]) * (D_C ** -0.5 * LOG2_E)).astype(BF16)
    qi = _dot(cq, wiq_ref[...]) * (D_I ** -0.5)
    for hh in range(H_I):
        qidx_ref[hh] = qi[:, hh * D_I:(hh + 1) * D_I].astype(BF16)


def _a_proj(x2, nw, win_all, mixer_layer, gcq, gkv, wuq_all, wuk_all, gq, wiq_all, tile):
    n, d = x2.shape
    once = lambda arr: pl.BlockSpec((None,) + arr.shape[1:], lambda i: (mixer_layer,) + (0,) * (arr.ndim - 1),
                                    pipeline_mode=pl.Buffered(1))
    grid = (n // tile,)
    tok = lambda w: pl.BlockSpec((tile, w), lambda i: (i, 0))
    return pl.pallas_call(
        _a_proj_kernel,
        grid=grid,
        in_specs=[tok(d), _const_spec(nw.shape), once(win_all), _const_spec(gcq.shape), _const_spec(gkv.shape),
                  once(wuq_all), once(wuk_all), _const_spec(gq.shape), once(wiq_all)],
        out_specs=[pl.BlockSpec((H_A, tile, D_C), lambda i: (0, i, 0)),
                   pl.BlockSpec((H_I, tile, D_I), lambda i: (0, i, 0)),
                   pl.BlockSpec((H_I, tile), lambda i: (0, i)),
                   tok(D_C), tok(D_I), tok(H_A * D_V)],
        out_shape=[jax.ShapeDtypeStruct((H_A, n, D_C), BF16),
                   jax.ShapeDtypeStruct((H_I, n, D_I), BF16),
                   jax.ShapeDtypeStruct((H_I, n), F32),
                   jax.ShapeDtypeStruct((n, D_C), BF16),
                   jax.ShapeDtypeStruct((n, D_I), BF16),
                   jax.ShapeDtypeStruct((n, H_A * D_V), F32)],
        scratch_shapes=[pltpu.VMEM((d, D_CQ + D_C + D_I + LANES), BF16), pltpu.VMEM((d, H_A * D_V), BF16),
                        pltpu.VMEM(wuq_all.shape[1:], BF16), pltpu.VMEM(wuk_all.shape[1:], BF16),
                        pltpu.VMEM(wiq_all.shape[1:], BF16)],
        compiler_params=_params("arbitrary"),
        name="a_proj",
    )(x2, nw, win_all, gcq, gkv, wuq_all, wuk_all, gq, wiq_all)


def _a_attn_kernel(qidx_ref, wt_ref, kidx_ref, qlat_ref, ckv_ref, z_ref, bt_ref, wuv_ref,
                   y_ref, keys_ref, hi_ref, lo_ref, lg_ref, mx_ref, ss_ref, oacc_ref, *, topk, n_q):
    qb = pl.program_id(1)
    n_kt = qb + 1
    rows = H_A * Q_TILE
    half = K_TILE // 2

    def for_tile_groups(body):
        def quad(i, carry):
            body(4 * i, 4)
            return carry

        n_quads = lax.shift_right_logical(n_kt, 2)
        lax.fori_loop(0, n_quads, quad, 0)

        @pl.when((n_kt & 2) == 2)
        def _():
            body(4 * n_quads, 2)

        @pl.when((n_kt & 1) == 1)
        def _():
            body(n_kt - 1, 1)

    def for_tile_groups_init(body):
        odd = (n_kt & 1) == 1

        @pl.when(odd)
        def _():
            body(0, 1, True)

        @pl.when(jnp.logical_not(odd))
        def _():
            body(0, 2, True)

        start = jnp.where(odd, 1, 2)

        def pair(i, carry):
            body(start + 2 * i, 2, False)
            return carry

        lax.fori_loop(0, lax.shift_right_logical(n_kt - start, 1), pair, 0)

    def idx_tiles(c0, cnt):
        k0 = pl.multiple_of(c0 * K_TILE, K_TILE)
        qidx = qidx_ref[...].reshape(H_I * Q_TILE, D_I)
        lg = _dot_nt(kidx_ref[pl.ds(k0, cnt * K_TILE), :], qidx)
        sc = jnp.maximum(lg[:, :Q_TILE], 0.0) * wt_ref[0:1, :]
        for hh in range(1, H_I):
            sc = sc + jnp.maximum(lg[:, hh * Q_TILE:(hh + 1) * Q_TILE], 0.0) * wt_ref[hh:hh + 1, :]
        bits = lax.bitcast_convert_type(sc, I32)
        sign = lax.shift_right_arithmetic(bits, 31)
        skey = (bits ^ (sign & 0x7FFFFFFF)) - sign
        key_pos = k0 + lax.broadcasted_iota(I32, (cnt * K_TILE, Q_TILE), 0)
        q_pos = qb * Q_TILE + lax.broadcasted_iota(I32, (cnt * K_TILE, Q_TILE), 1)
        skey = jnp.where(key_pos <= q_pos, skey, INT_MIN)
        hi = lax.shift_right_arithmetic(skey, HALF_BITS).astype(I16)
        lo = ((skey & ((1 << HALF_BITS) - 1)) + I16_MIN).astype(I16)
        for u in range(cnt):
            ks = slice(u * K_TILE, (u + 1) * K_TILE)
            keys_ref[c0 + u] = skey[ks]
            hi_ref[c0 + u] = hi[ks].reshape(K_TILE // PACKED_ROWS, PACKED_ROWS, Q_TILE)
            lo_ref[c0 + u] = lo[ks].reshape(K_TILE // PACKED_ROWS, PACKED_ROWS, Q_TILE)

    for_tile_groups(idx_tiles)

    one = jnp.ones((), BF16)
    zero = jnp.zeros((), BF16)

    def threshold(n_tiles):
        if n_tiles * Q_TILE <= topk:
            return jnp.full((1, Q_TILE), INT_MIN + 1, I32), jnp.zeros((1, Q_TILE), F32)

        def count(ref16, bound16, strict):
            acc = jnp.zeros((PACKED_ROWS, Q_TILE), BF16)
            for c in range(n_tiles):
                tile = ref16[c]
                hit = jnp.where(tile > bound16[None] if strict else tile >= bound16[None], one, zero)
                parts = [hit[u] for u in range(K_TILE // PACKED_ROWS)]
                while len(parts) > 1:
                    parts = [parts[u] + parts[u + 1] for u in range(0, len(parts), 2)]
                acc = acc + parts[0]
            return jnp.sum(acc.astype(F32), axis=0, keepdims=True)

        def kth_largest(ref16, need, n_all):
            def bit_body(i, carry):
                thr, n_ge = carry
                cand = thr + lax.shift_left(jnp.int32(1), HALF_BITS - 1 - i)
                n_cand = count(ref16, cand.astype(I16), False)
                ok = n_cand >= need
                return jnp.where(ok, cand, thr), jnp.where(ok, n_cand, n_ge)

            return lax.fori_loop(0, HALF_BITS, bit_body, (jnp.full((PACKED_ROWS, Q_TILE), I16_MIN, I32), n_all))

        n_all = jnp.full((1, Q_TILE), float(n_tiles * K_TILE), F32)
        hi_thr, n_ge_hi = kth_largest(hi_ref, float(topk), n_all)
        hi_thr16 = hi_thr.astype(I16)
        n_gt_hi = count(hi_ref, hi_thr16, True)
        for c in range(n_tiles):
            lo_ref[c] = jnp.where(hi_ref[c] == hi_thr16[None], lo_ref[c], jnp.int16(I16_MIN))
        lo_thr, n_ge_lo = kth_largest(lo_ref, float(topk) - n_gt_hi, n_ge_hi - n_gt_hi)
        thr = hi_thr[0:1] * (1 << HALF_BITS) + (lo_thr[0:1] - I16_MIN)
        thr = jnp.maximum(thr, INT_MIN + 1)
        return thr, n_gt_hi + n_ge_lo - float(topk)

    thr, excess = lax.switch(qb, [functools.partial(threshold, n) for n in range(1, n_q + 1)])

    @pl.when(jnp.max(excess) > 0.0)
    def _():
        kr = lax.broadcasted_iota(I32, (K_TILE, K_TILE), 0)
        kc = lax.broadcasted_iota(I32, (K_TILE, K_TILE), 1)
        later = jnp.where(kc >= kr, 1.0, 0.0).astype(BF16)

        def tie_body(i, seen):
            c = n_kt - 1 - i
            keys = keys_ref[c]
            tie = keys == thr
            rank = _dot(later, jnp.where(tie, 1.0, 0.0).astype(BF16)) + seen
            keys_ref[c] = jnp.where(tie & (rank <= excess), thr - 1, keys)
            return rank[0:1]

        lax.fori_loop(0, n_kt, tie_body, jnp.zeros((1, Q_TILE), F32))

    def qk_tiles(c0, cnt, init):
        k0 = pl.multiple_of(c0 * K_TILE, K_TILE)
        lg = _dot_nt(qlat_ref[...].reshape(rows, D_C), ckv_ref[pl.ds(k0, cnt * K_TILE), :])
        for u in range(cnt):
            c = c0 + u
            mb = jnp.where(keys_ref[c] >= thr, 0.0, NEG_INF).T
            t_idx = jnp.minimum(qb - c, BIAS_CLASSES - 1)
            for hh in range(H_A):
                l = lg[hh * Q_TILE:(hh + 1) * Q_TILE, u * K_TILE:(u + 1) * K_TILE] + (mb + bt_ref[t_idx, hh])
                lg_ref[c, hh] = l
                top = jnp.maximum(l[:, :half], l[:, half:])
                mx_ref[hh] = top if (init and u == 0) else jnp.maximum(mx_ref[hh], top)

    for_tile_groups_init(qk_tiles)

    m = jnp.max(mx_ref[...], axis=-1, keepdims=True)

    def pv_tiles(c0, cnt, init):
        k0 = pl.multiple_of(c0 * K_TILE, K_TILE)
        ps = [jnp.exp2(lg_ref[c0 + u] - m) for u in range(cnt)]
        part = ps[0][:, :, :half] + ps[0][:, :, half:]
        for p in ps[1:]:
            part = part + (p[:, :, :half] + p[:, :, half:])
        p16 = jnp.concatenate([p.reshape(rows, K_TILE).astype(BF16) for p in ps], axis=1)
        pv = _dot(p16, ckv_ref[pl.ds(k0, cnt * K_TILE), :])
        if init:
            ss_ref[...] = part
            oacc_ref[...] = pv
        else:
            ss_ref[...] += part
            oacc_ref[...] += pv

    for_tile_groups_init(pv_tiles)
    inv_denom = 1.0 / jnp.sum(ss_ref[...], axis=-1, keepdims=True)

    for hh in range(H_A):
        oh = _dot(oacc_ref[hh * Q_TILE:(hh + 1) * Q_TILE, :].astype(BF16), wuv_ref[hh])
        zz = z_ref[:, hh * D_V:(hh + 1) * D_V]
        y_ref[:, hh * D_V:(hh + 1) * D_V] = (oh * inv_denom[hh] * (zz * _sigmoid(zz))).astype(BF16)


def _a_attn(qidx, wt, kidx, qlat, ckv, z, bt, wuv, batch, seq):
    n = batch * seq
    nq = seq // Q_TILE
    n_kt = seq // K_TILE
    topk = min(TOPK_MAX, seq // 4)
    grid = (batch, nq)
    kern = functools.partial(_a_attn_kernel, topk=topk, n_q=nq)
    return pl.pallas_call(
        kern,
        grid=grid,
        in_specs=[pl.BlockSpec((H_I, Q_TILE, D_I), lambda b, q: (0, b * nq + q, 0)),
                  pl.BlockSpec((H_I, Q_TILE), lambda b, q: (0, b * nq + q)),
                  pl.BlockSpec((seq, D_I), lambda b, q: (b, 0)),
                  pl.BlockSpec((H_A, Q_TILE, D_C), lambda b, q: (0, b * nq + q, 0)),
                  pl.BlockSpec((seq, D_C), lambda b, q: (b, 0)),
                  pl.BlockSpec((Q_TILE, H_A * D_V), lambda b, q: (b * nq + q, 0)),
                  pl.BlockSpec(bt.shape, lambda b, q: (0, 0, 0, 0), pipeline_mode=pl.Buffered(1)),
                  pl.BlockSpec(wuv.shape, lambda b, q: (0, 0, 0), pipeline_mode=pl.Buffered(1))],
        out_specs=pl.BlockSpec((Q_TILE, H_A * D_V), lambda b, q: (b * nq + q, 0)),
        out_shape=jax.ShapeDtypeStruct((n, H_A * D_V), BF16),
        scratch_shapes=[pltpu.VMEM((n_kt, K_TILE, Q_TILE), I32),
                        pltpu.VMEM((n_kt, K_TILE // PACKED_ROWS, PACKED_ROWS, Q_TILE), I16),
                        pltpu.VMEM((n_kt, K_TILE // PACKED_ROWS, PACKED_ROWS, Q_TILE), I16),
                        pltpu.VMEM((n_kt, H_A, Q_TILE, K_TILE), F32),
                        pltpu.VMEM((H_A, Q_TILE, K_TILE // 2), F32),
                        pltpu.VMEM((H_A, Q_TILE, K_TILE // 2), F32),
                        pltpu.VMEM((H_A * Q_TILE, D_C), F32)],
        compiler_params=_params("parallel", "arbitrary"),
        name="a_attn",
    )(qidx, wt, kidx, qlat, ckv, z, bt, wuv)


def _b_proj_kernel(x_ref, nw_ref, win_ref, cw_ref, alog_ref, dtb_ref, alogc_ref, dtbc_ref,
                   q_ref, k_ref, v_ref, beta_ref, g_ref, gt_ref, z_ref,
                   buf_ref, wqkv_ref, wba_ref, wbat_ref, wz_ref, *, tile):
    s = pl.program_id(1)
    b_qkv = H_B * (2 * D_K + D_VB)

    @pl.when((pl.program_id(0) == 0) & (s == 0))
    def _():
        wqkv_ref[...] = win_ref[:, :b_qkv].astype(BF16)
        slab = win_ref[:, b_qkv:b_qkv + LANES]
        wba_ref[...] = slab.astype(BF16)
        wbat_ref[...] = slab.T.astype(BF16)
        wz_ref[...] = win_ref[:, b_qkv + 2 * H_B:].astype(BF16)

    h = _rms(x_ref[...], nw_ref[...]).astype(BF16)
    z_ref[...] = _dot(h, wz_ref[...])
    ba = _dot(h, wba_ref[...])
    beta_ref[...] = _sigmoid(ba[:, :H_B])
    g_ref[...] = -jnp.exp(alog_ref[...]) * _softplus(ba[:, H_B:2 * H_B] + dtb_ref[...])
    bat = _dot_nt(wbat_ref[0:2 * H_B, :], h)
    gt_ref[...] = -jnp.exp(alogc_ref[...]) * _softplus(bat[H_B:, :] + dtbc_ref[...])

    @pl.when(s == 0)
    def _():
        buf_ref[0:8, :] = jnp.zeros((8, buf_ref.shape[1]), F32)

    width = H_B * D_K
    for sec, out_ref in enumerate((q_ref, k_ref, v_ref)):
        cols = slice(sec * width, (sec + 1) * width)
        pre = _dot(h, wqkv_ref[:, cols])
        buf_ref[8:8 + tile, cols] = pre
        acc = pre * cw_ref[CONV_W - 1:CONV_W, cols]
        for w in range(CONV_W - 1):
            acc = acc + buf_ref[8 - (CONV_W - 1) + w:8 - (CONV_W - 1) + w + tile, cols] * cw_ref[w:w + 1, cols]
        buf_ref[0:8, cols] = buf_ref[tile:tile + 8, cols]
        y = acc * _sigmoid(acc)
        if sec == 2:
            out_ref[...] = y
        else:
            scale = (D_K ** -0.5) if sec == 0 else 1.0
            for hh in range(H_B):
                yy = y[:, hh * D_K:(hh + 1) * D_K]
                nrm = lax.rsqrt(jnp.sum(yy * yy, axis=-1, keepdims=True) + EPS)
                out_ref[:, hh * D_K:(hh + 1) * D_K] = yy * (nrm * scale)


def _b_proj(x2, nw, win_all, mixer_layer, cw, alog, dtb, batch, seq, tile):
    n, d = x2.shape
    ns = seq // tile
    grid = (batch, ns)
    tok = lambda w: pl.BlockSpec((tile, w), lambda b, s: (b * ns + s, 0))
    width = H_B * D_K
    kern = functools.partial(_b_proj_kernel, tile=tile)
    return pl.pallas_call(
        kern,
        grid=grid,
        in_specs=[tok(d), _const_spec(nw.shape),
                  pl.BlockSpec((None,) + win_all.shape[1:], lambda b, s: (mixer_layer, 0, 0),
                               pipeline_mode=pl.Buffered(1)),
                  _const_spec(cw.shape),
                  _const_spec((1, H_B)), _const_spec((1, H_B)), _const_spec((H_B, 1)), _const_spec((H_B, 1))],
        out_specs=[tok(width), tok(width), tok(width), tok(H_B), tok(H_B),
                   pl.BlockSpec((H_B, tile), lambda b, s: (0, b * ns + s)), tok(width)],
        out_shape=[jax.ShapeDtypeStruct((n, width), F32)] * 3
        + [jax.ShapeDtypeStruct((n, H_B), F32)] * 2
        + [jax.ShapeDtypeStruct((H_B, n), F32), jax.ShapeDtypeStruct((n, width), F32)],
        scratch_shapes=[pltpu.VMEM((tile + 8, 3 * width), F32),
                        pltpu.VMEM((d, 3 * width), BF16), pltpu.VMEM((d, LANES), BF16),
                        pltpu.VMEM((LANES, d), BF16), pltpu.VMEM((d, width), BF16)],
        compiler_params=_params("arbitrary", "arbitrary"),
        name="b_proj",
    )(x2, nw, win_all, cw, alog.reshape(1, H_B), dtb.reshape(1, H_B),
      alog.reshape(H_B, 1), dtb.reshape(H_B, 1))


def _b_gdn_kernel(q_ref, k_ref, v_ref, z_ref, beta_ref, g_ref, gt_ref, go_ref, y_ref, state_ref):
    hb = pl.program_id(1)
    s = pl.program_id(2)
    t = GDN_TILE
    n_chunks = t // CHUNK
    log_c = int(math.log2(CHUNK))
    heads = range(GDN_HEADS)

    @pl.when(s == 0)
    def _():
        state_ref[...] = jnp.zeros(state_ref.shape, F32)

    r = lax.broadcasted_iota(I32, (t, t), 0)
    c = lax.broadcasted_iota(I32, (t, t), 1)
    xs = r ^ c
    same = xs < CHUNK
    incl = same & (r >= c)
    upper = same & (r <= c)
    eye = (r == c).astype(F32)
    level = [(lax.shift_right_logical(xs, lb) == 1) & ((r & (1 << lb)) != 0) for lb in range(log_c)]
    lane8 = lax.broadcasted_iota(I32, (t, H_B), 1)
    sub8 = lax.broadcasted_iota(I32, (H_B, t), 0)

    b_col, decay, eg_col, ekl_col, egl_col = [], [], [], [], []
    for j in heads:
        head = hb * GDN_HEADS + j
        g_col = jnp.sum(jnp.where(lane8 == head, g_ref[...], 0.0), axis=1, keepdims=True)
        b_col.append(jnp.sum(jnp.where(lane8 == head, beta_ref[...], 0.0), axis=1, keepdims=True))
        g_row = jnp.sum(jnp.where(sub8 == head, gt_ref[...], 0.0), axis=0, keepdims=True)
        g_rows = jnp.broadcast_to(g_row, (t, t))
        gc_col = jnp.sum(jnp.where(incl, g_rows, 0.0), axis=1, keepdims=True)
        gl_col = jnp.sum(jnp.where(same, g_rows, 0.0), axis=1, keepdims=True)
        gc_row = jnp.sum(jnp.where(upper, jnp.broadcast_to(g_col, (t, t)), 0.0), axis=0, keepdims=True)
        decay.append(jnp.where(incl, jnp.exp(jnp.minimum(gc_col - gc_row, 0.0)), 0.0))
        eg_col.append(jnp.exp(gc_col))
        ekl_col.append(jnp.exp(gl_col - gc_col))
        egl_col.append(jnp.exp(gl_col))

    lmat, aqk, rhs, q_dec, k_dec = [], [], [], [], []
    for j in heads:
        hs = slice(j * D_K, (j + 1) * D_K)
        qf, kf, vf = q_ref[:, hs], k_ref[:, hs], v_ref[:, hs]
        kb = kf * b_col[j]
        k16 = kf.astype(BF16)
        gram = _dot_nt(jnp.concatenate([kb.astype(BF16), qf.astype(BF16)], axis=0), k16)
        lmat.append(gram[:t] * decay[j])
        aqk.append(jnp.where(incl, gram[t:] * decay[j], 0.0).astype(BF16))
        rhs.append(jnp.concatenate([(vf * b_col[j]).astype(BF16), (kb * eg_col[j]).astype(BF16)], axis=1))
        q_dec.append(qf * eg_col[j])
        k_dec.append((kf * ekl_col[j]).astype(BF16))

    tinv = [eye - jnp.where(level[0], lmat[j], 0.0) for j in heads]
    for lb in range(1, log_c):
        half = 1 << lb
        t16 = [tinv[j].astype(BF16) for j in heads]
        if half < 8:
            y16 = [_dot(jnp.where(level[lb], lmat[j], 0.0).astype(BF16), t16[j]).astype(BF16) for j in heads]
            tinv = [tinv[j] - _dot(t16[j], y16[j]) for j in heads]
            continue
        pairs = t // (2 * half)
        split = lambda a: a.reshape(pairs, 2, half, t)
        lower = lambda a: split(a)[:, 1].reshape(t // 2, t)
        i = lax.broadcasted_iota(I32, (t // 2, t), 0)
        r_low = lax.shift_left(lax.shift_right_logical(i, lb), lb + 1) + half + (i & (half - 1))
        in_b = lax.shift_right_logical(r_low ^ lax.broadcasted_iota(I32, (t // 2, t), 1), lb) == 1
        y_low = [_dot(jnp.where(in_b, lower(lmat[j]), 0.0).astype(BF16), t16[j]) for j in heads]
        zeros = jnp.zeros((pairs, half, t), F32)
        y16 = [jnp.stack([zeros, y_low[j].reshape(pairs, half, t)], axis=1).reshape(t, t).astype(BF16) for j in heads]
        t_low = [lower(tinv[j]) for j in heads]
        t_low = [t_low[j] - _dot(t_low[j].astype(BF16), y16[j]) for j in heads]
        tinv = [jnp.stack([split(tinv[j])[:, 0], t_low[j].reshape(pairs, half, t)], axis=1).reshape(t, t)
                for j in heads]

    sol16 = [_dot(tinv[j].astype(BF16), rhs[j]).astype(BF16) for j in heads]
    aux = [_dot(aqk[j], sol16[j]) for j in heads]
    q_eff = [(q_dec[j] - aux[j][:, D_VB:]).astype(BF16) for j in heads]
    kw = [[_dot_tn(k_dec[j][ci * CHUNK:(ci + 1) * CHUNK], sol16[j][ci * CHUNK:(ci + 1) * CHUNK])
           for ci in range(n_chunks)] for j in heads]

    state = [state_ref[j] for j in heads]
    outs = [[] for _ in heads]
    for ci in range(n_chunks):
        rs = slice(ci * CHUNK, (ci + 1) * CHUNK)
        for j in heads:
            s16 = state[j].astype(BF16)
            both = _dot(jnp.concatenate([q_eff[j][rs], kw[j][ci][:, D_VB:].astype(BF16)], axis=0), s16)
            outs[j].append(both[:CHUNK] + aux[j][rs, :D_VB])
            state[j] = state[j] * egl_col[j][ci * CHUNK:ci * CHUNK + 1, :] + kw[j][ci][:, :D_VB] - both[CHUNK:]

    for j in heads:
        hs = slice(j * D_K, (j + 1) * D_K)
        state_ref[j] = state[j]
        o = jnp.concatenate(outs[j], axis=0)
        zz = z_ref[:, hs]
        y_ref[:, hs] = (_rms(o, go_ref[...]) * (zz * _sigmoid(zz))).astype(BF16)


def _b_gdn(q, k, v, z, beta, g, gt, go, batch, seq):
    n = batch * seq
    ns = seq // GDN_TILE
    nh = H_B // GDN_HEADS
    grid = (batch, nh, ns)
    wide = pl.BlockSpec((GDN_TILE, GDN_HEADS * D_K), lambda b, h, s: (b * ns + s, h))
    narrow = pl.BlockSpec((GDN_TILE, H_B), lambda b, h, s: (b * ns + s, 0))
    return pl.pallas_call(
        _b_gdn_kernel,
        grid=grid,
        in_specs=[wide, wide, wide, wide, narrow, narrow,
                  pl.BlockSpec((H_B, GDN_TILE), lambda b, h, s: (0, b * ns + s)),
                  _const_spec(go.shape)],
        out_specs=wide,
        out_shape=jax.ShapeDtypeStruct((n, H_B * D_VB), BF16),
        scratch_shapes=[pltpu.VMEM((GDN_HEADS, D_K, D_VB), F32)],
        compiler_params=_params("parallel", "parallel", "arbitrary"),
        name="b_gdn",
    )(q, k, v, z, beta, g, gt, go)


def _out_ple_kernel(x_ref, y_ref, p_ref, wout_ref, pn_ref, wgate_ref, wproj_ref, o_ref, wout16, wgate16, wproj16):
    @pl.when(pl.program_id(0) == 0)
    def _():
        wout16[...] = wout_ref[...].astype(BF16)
        wgate16[...] = wgate_ref[...].astype(BF16)
        wproj16[...] = wproj_ref[...].astype(BF16)

    x1 = x_ref[...] + _dot(y_ref[...], wout16[...])
    hn = _rms(x1, pn_ref[...]).astype(BF16)
    gate = _sigmoid(_dot(hn, wgate16[...]))
    o_ref[...] = x1 + gate * _dot(p_ref[...].astype(BF16), wproj16[...])


def _out_ple(x2, y, p2, layer, wout_all, mixer_layer, pn_all, wgate_all, wproj_all, tile):
    n, d = x2.shape
    tok = lambda w: pl.BlockSpec((tile, w), lambda i: (i, 0))
    once = lambda arr, idx: pl.BlockSpec((None,) + arr.shape[1:], lambda i: (idx,) + (0,) * (arr.ndim - 1),
                                         pipeline_mode=pl.Buffered(1))
    return pl.pallas_call(
        _out_ple_kernel,
        grid=(n // tile,),
        in_specs=[tok(d), tok(y.shape[1]),
                  pl.BlockSpec((tile, p2.shape[1]), lambda i: (layer * (n // tile) + i, 0)),
                  once(wout_all, mixer_layer), once(pn_all, layer), once(wgate_all, layer), once(wproj_all, layer)],
        out_specs=tok(d),
        out_shape=jax.ShapeDtypeStruct((n, d), F32),
        scratch_shapes=[pltpu.VMEM(wout_all.shape[1:], BF16), pltpu.VMEM(wgate_all.shape[1:], BF16),
                        pltpu.VMEM(wproj_all.shape[1:], BF16)],
        compiler_params=_params("arbitrary"),
        name="out_ple",
    )(x2, y, p2, wout_all, pn_all, wgate_all, wproj_all)


def _t5_bucket(rel):
    max_exact = N_BUCKETS // 2
    rel = jnp.maximum(rel, 0)
    rel_f = jnp.maximum(rel, 1).astype(F32)
    log_ratio = jnp.log(rel_f / max_exact) / math.log(MAX_DISTANCE / max_exact)
    large = max_exact + (log_ratio * (N_BUCKETS - max_exact)).astype(I32)
    large = jnp.minimum(large, N_BUCKETS - 1)
    return jnp.where(rel < max_exact, rel, large)


def _bias_tiles(rel_bias):
    span = Q_TILE + K_TILE
    m = jnp.arange(span, dtype=I32)
    key_minus_query = jnp.where(m < K_TILE, m, m - span)
    scaled = rel_bias.astype(F32) * LOG2_E
    tiles = []
    for cls in range(BIAS_CLASSES):
        w = scaled[_t5_bucket(cls * Q_TILE - key_minus_query)].T
        skew = jnp.tile(w, (1, Q_TILE))[:, :Q_TILE * (span - 1)].reshape(H_A, Q_TILE, span - 1)
        tiles.append(skew[:, :, :K_TILE])
    return jnp.stack(tiles)


def kernel(x, p, norm_w, a_w_in, a_g_cq, a_w_uq, a_w_uk, a_g_q, a_g_kv, a_w_iq, a_w_uv, a_w_out, rel_bias, b_w_in, b_conv_w, b_a_log, b_dt_bias, b_g_o, b_w_out, ple_norm, ple_w_gate, ple_w_proj):
    batch, seq, d_model = x.shape
    depth = p.shape[0]
    n = batch * seq
    assert seq % Q_TILE == 0 and seq % K_TILE == 0 and seq % GDN_TILE == 0 and H_B % GDN_HEADS == 0
    assert n % PROJ_TILE == 0 and seq % CONV_TILE == 0
    x2 = x.reshape(n, d_model)
    p2 = p.reshape(depth * n, p.shape[-1])
    bt = _bias_tiles(rel_bias)
    row = lambda a: a.reshape(1, -1).astype(F32)
    for i in range(depth):
        j = i // 2
        if i % 2 == 0:
            qlat, qidx, wt, ckv, kidx, z = _a_proj(
                x2, row(norm_w[i]), a_w_in, j, row(a_g_cq[j]), row(a_g_kv[j]),
                a_w_uq.reshape(-1, D_CQ, H_A * D_NOPE), a_w_uk, row(a_g_q[j]),
                a_w_iq.reshape(-1, D_CQ, H_I * D_I), tile=PROJ_TILE)
            y = _a_attn(qidx, wt, kidx, qlat, ckv, z, bt, a_w_uv[j].astype(BF16), batch, seq)
            w_out_all = a_w_out
        else:
            q, k, v, beta, g, gt, z = _b_proj(
                x2, row(norm_w[i]), b_w_in, j, b_conv_w[j].astype(F32),
                b_a_log[j].astype(F32), b_dt_bias[j].astype(F32), batch, seq, tile=CONV_TILE)
            y = _b_gdn(q, k, v, z, beta, g, gt, row(b_g_o[j]), batch, seq)
            w_out_all = b_w_out
        x2 = _out_ple(x2, y, p2, i, w_out_all, j, ple_norm.reshape(depth, 1, d_model).astype(F32), ple_w_gate,
                      ple_w_proj, tile=PROJ_TILE)
    return x2.reshape(batch, seq, d_model)
```

```python
import functools
import math

import jax
import jax.numpy as jnp
from jax import lax
from jax.experimental import pallas as pl
from jax.experimental.pallas import tpu as pltpu

F32 = jnp.float32
BF16 = jnp.bfloat16
I32 = jnp.int32
I16 = jnp.int16

EPS = 1e-6
NEG_INF = -1e30
LOG2_E = math.log2(math.e)
INT_MIN = -(2 ** 31)
HALF_BITS = 16
I16_MIN = -(1 << (HALF_BITS - 1))

LANES = 128
PACKED_ROWS = 16
VMEM_LIMIT_BYTES = 56 * 1024 * 1024

H_A = 8
D_C = 256
D_CQ = 256
D_I = 128
H_I = 8
D_NOPE = 128
D_V = 128
TOPK_MAX = 256
N_BUCKETS = 32
MAX_DISTANCE = 128
Q_TILE = 256
K_TILE = 256
BIAS_CLASSES = -(-(MAX_DISTANCE + K_TILE - 1) // Q_TILE) + 1
H_B = 8
D_K = 128
D_VB = 128
CONV_W = 4
CHUNK = 64
GDN_TILE = 256
GDN_HEADS = 8
PROJ_TILE = 1024
CONV_TILE = 512


def _dot(a, b):
    return jnp.dot(a, b, preferred_element_type=F32)


def _dot_nt(a, b):
    return lax.dot_general(a, b, (((1,), (1,)), ((), ())), preferred_element_type=F32)


def _dot_tn(a, b):
    return lax.dot_general(a, b, (((0,), (0,)), ((), ())), preferred_element_type=F32)


def _rms(x, gain=None):
    y = x * lax.rsqrt(jnp.mean(x * x, axis=-1, keepdims=True) + EPS)
    return y if gain is None else y * gain


def _sigmoid(x):
    return 1.0 / (1.0 + jnp.exp2(x * -LOG2_E))


def _softplus(x):
    return jnp.maximum(x, 0.0) + jnp.log1p(jnp.exp(-jnp.abs(x)))


def _params(*semantics):
    return pltpu.CompilerParams(dimension_semantics=semantics, vmem_limit_bytes=VMEM_LIMIT_BYTES)


def _const_spec(shape):
    nd = len(shape)
    return pl.BlockSpec(shape, lambda *_: (0,) * nd)


def _a_proj_kernel(x_ref, nw_ref, win_ref, gcq_ref, gkv_ref, wuq32_ref, wuk32_ref, gq_ref, wiq32_ref,
                   qlat_ref, qidx_ref, wt_ref, ckv_ref, kidx_ref, z_ref,
                   wlat_ref, wz_ref, wuq_ref, wuk_ref, wiq_ref):
    n_lat = D_CQ + D_C + D_I

    @pl.when(pl.program_id(0) == 0)
    def _():
        wlat_ref[...] = win_ref[:, :n_lat + LANES].astype(BF16)
        wz_ref[...] = win_ref[:, n_lat + H_I:].astype(BF16)
        for hd in range(H_A):
            wuq_ref[:, hd * D_NOPE:(hd + 1) * D_NOPE] = wuq32_ref[:, hd, :].astype(BF16)
        wuk_ref[...] = wuk32_ref[...].astype(BF16)
        for hd in range(H_I):
            wiq_ref[:, hd * D_I:(hd + 1) * D_I] = wiq32_ref[:, hd, :].astype(BF16)

    h = _rms(x_ref[...], nw_ref[...]).astype(BF16)
    lat = _dot(h, wlat_ref[...])
    cq = _rms(lat[:, :D_CQ], gcq_ref[...]).astype(BF16)
    ckv_ref[...] = _rms(lat[:, D_CQ:D_CQ + D_C], gkv_ref[...]).astype(BF16)
    kidx_ref[...] = _rms(lat[:, D_CQ + D_C:n_lat]).astype(BF16)
    wt_ref[...] = lat[:, n_lat:].T[:H_I] * (H_I ** -0.5)
    z_ref[...] = _dot(h, wz_ref[...])
    qn = _dot(cq, wuq_ref[...]).astype(BF16)
    for hh in range(H_A):
        ql = _dot(qn[:, hh * D_NOPE:(hh + 1) * D_NOPE], wuk_ref[hh])
        qlat_ref[hh] = (_rms(ql, gq_ref[...]) * (D_C ** -0.5 * LOG2_E)).astype(BF16)
    qi = _dot(cq, wiq_ref[...]) * (D_I ** -0.5)
    for hh in range(H_I):
        qidx_ref[hh] = qi[:, hh * D_I:(hh + 1) * D_I].astype(BF16)


def _a_proj(x2, nw, win_all, mixer_layer, gcq, gkv, wuq_all, wuk_all, gq, wiq_all, tile):
    n, d = x2.shape
    once = lambda arr: pl.BlockSpec((None,) + arr.shape[1:], lambda i: (mixer_layer,) + (0,) * (arr.ndim - 1),
                                    pipeline_mode=pl.Buffered(1))
    grid = (n // tile,)
    tok = lambda w: pl.BlockSpec((tile, w), lambda i: (i, 0))
    return pl.pallas_call(
        _a_proj_kernel,
        grid=grid,
        in_specs=[tok(d), _const_spec(nw.shape), once(win_all), _const_spec(gcq.shape), _const_spec(gkv.shape),
                  once(wuq_all), once(wuk_all), _const_spec(gq.shape), once(wiq_all)],
        out_specs=[pl.BlockSpec((H_A, tile, D_C), lambda i: (0, i, 0)),
                   pl.BlockSpec((H_I, tile, D_I), lambda i: (0, i, 0)),
                   pl.BlockSpec((H_I, tile), lambda i: (0, i)),
                   tok(D_C), tok(D_I), tok(H_A * D_V)],
        out_shape=[jax.ShapeDtypeStruct((H_A, n, D_C), BF16),
                   jax.ShapeDtypeStruct((H_I, n, D_I), BF16),
                   jax.ShapeDtypeStruct((H_I, n), F32),
                   jax.ShapeDtypeStruct((n, D_C), BF16),
                   jax.ShapeDtypeStruct((n, D_I), BF16),
                   jax.ShapeDtypeStruct((n, H_A * D_V), F32)],
        scratch_shapes=[pltpu.VMEM((d, D_CQ + D_C + D_I + LANES), BF16), pltpu.VMEM((d, H_A * D_V), BF16),
                        pltpu.VMEM((D_CQ, H_A * D_NOPE), BF16), pltpu.VMEM(wuk_all.shape[1:], BF16),
                        pltpu.VMEM((D_CQ, H_I * D_I), BF16)],
        compiler_params=_params("arbitrary"),
        name="a_proj",
    )(x2, nw, win_all, gcq, gkv, wuq_all, wuk_all, gq, wiq_all)


def _a_attn_kernel(qidx_ref, wt_ref, kidx_ref, qlat_ref, ckv_ref, z_ref, bt_ref, wuv_ref,
                   y_ref, keys_ref, hi_ref, lo_ref, lg_ref, mx_ref, ss_ref, oacc_ref, *, topk, n_q):
    qb = pl.program_id(1)
    n_kt = qb + 1
    rows = H_A * Q_TILE
    half = K_TILE // 2

    def for_tile_groups(body):
        def quad(i, carry):
            body(4 * i, 4)
            return carry

        n_quads = lax.shift_right_logical(n_kt, 2)
        lax.fori_loop(0, n_quads, quad, 0)

        @pl.when((n_kt & 2) == 2)
        def _():
            body(4 * n_quads, 2)

        @pl.when((n_kt & 1) == 1)
        def _():
            body(n_kt - 1, 1)

    def for_tile_groups_init(body):
        odd = (n_kt & 1) == 1

        @pl.when(odd)
        def _():
            body(0, 1, True)

        @pl.when(jnp.logical_not(odd))
        def _():
            body(0, 2, True)

        start = jnp.where(odd, 1, 2)

        def pair(i, carry):
            body(start + 2 * i, 2, False)
            return carry

        lax.fori_loop(0, lax.shift_right_logical(n_kt - start, 1), pair, 0)

    def idx_tiles(c0, cnt):
        k0 = pl.multiple_of(c0 * K_TILE, K_TILE)
        qidx = qidx_ref[...].reshape(H_I * Q_TILE, D_I)
        lg = _dot_nt(kidx_ref[pl.ds(k0, cnt * K_TILE), :], qidx)
        sc = jnp.maximum(lg[:, :Q_TILE], 0.0) * wt_ref[0:1, :]
        for hh in range(1, H_I):
            sc = sc + jnp.maximum(lg[:, hh * Q_TILE:(hh + 1) * Q_TILE], 0.0) * wt_ref[hh:hh + 1, :]
        bits = lax.bitcast_convert_type(sc, I32)
        sign = lax.shift_right_arithmetic(bits, 31)
        skey = (bits ^ (sign & 0x7FFFFFFF)) - sign
        key_pos = k0 + lax.broadcasted_iota(I32, (cnt * K_TILE, Q_TILE), 0)
        q_pos = qb * Q_TILE + lax.broadcasted_iota(I32, (cnt * K_TILE, Q_TILE), 1)
        skey = jnp.where(key_pos <= q_pos, skey, INT_MIN)
        hi = lax.shift_right_arithmetic(skey, HALF_BITS).astype(I16)
        lo = ((skey & ((1 << HALF_BITS) - 1)) + I16_MIN).astype(I16)
        for u in range(cnt):
            ks = slice(u * K_TILE, (u + 1) * K_TILE)
            keys_ref[c0 + u] = skey[ks]
            hi_ref[c0 + u] = hi[ks].reshape(K_TILE // PACKED_ROWS, PACKED_ROWS, Q_TILE)
            lo_ref[c0 + u] = lo[ks].reshape(K_TILE // PACKED_ROWS, PACKED_ROWS, Q_TILE)

    for_tile_groups(idx_tiles)

    one = jnp.ones((), BF16)
    zero = jnp.zeros((), BF16)

    def threshold(n_tiles):
        if n_tiles * Q_TILE <= topk:
            return jnp.full((1, Q_TILE), INT_MIN + 1, I32), jnp.zeros((1, Q_TILE), F32)

        def count(ref16, bound16, strict):
            acc = jnp.zeros((PACKED_ROWS, Q_TILE), BF16)
            for c in range(n_tiles):
                tile = ref16[c]
                hit = jnp.where(tile > bound16[None] if strict else tile >= bound16[None], one, zero)
                parts = [hit[u] for u in range(K_TILE // PACKED_ROWS)]
                while len(parts) > 1:
                    parts = [parts[u] + parts[u + 1] for u in range(0, len(parts), 2)]
                acc = acc + parts[0]
            return jnp.sum(acc.astype(F32), axis=0, keepdims=True)

        def kth_largest(ref16, need, n_all):
            def bit_body(i, carry):
                thr, n_ge = carry
                cand = thr + lax.shift_left(jnp.int32(1), HALF_BITS - 1 - i)
                n_cand = count(ref16, cand.astype(I16), False)
                ok = n_cand >= need
                return jnp.where(ok, cand, thr), jnp.where(ok, n_cand, n_ge)

            return lax.fori_loop(0, HALF_BITS, bit_body, (jnp.full((PACKED_ROWS, Q_TILE), I16_MIN, I32), n_all))

        n_all = jnp.full((1, Q_TILE), float(n_tiles * K_TILE), F32)
        hi_thr, n_ge_hi = kth_largest(hi_ref, float(topk), n_all)
        hi_thr16 = hi_thr.astype(I16)
        n_gt_hi = count(hi_ref, hi_thr16, True)
        for c in range(n_tiles):
            lo_ref[c] = jnp.where(hi_ref[c] == hi_thr16[None], lo_ref[c], jnp.int16(I16_MIN))
        lo_thr, n_ge_lo = kth_largest(lo_ref, float(topk) - n_gt_hi, n_ge_hi - n_gt_hi)
        thr = hi_thr[0:1] * (1 << HALF_BITS) + (lo_thr[0:1] - I16_MIN)
        thr = jnp.maximum(thr, INT_MIN + 1)
        return thr, n_gt_hi + n_ge_lo - float(topk)

    thr, excess = lax.switch(qb, [functools.partial(threshold, n) for n in range(1, n_q + 1)])

    @pl.when(jnp.max(excess) > 0.0)
    def _():
        kr = lax.broadcasted_iota(I32, (K_TILE, K_TILE), 0)
        kc = lax.broadcasted_iota(I32, (K_TILE, K_TILE), 1)
        later = jnp.where(kc >= kr, 1.0, 0.0).astype(BF16)

        def tie_body(i, seen):
            c = n_kt - 1 - i
            keys = keys_ref[c]
            tie = keys == thr
            rank = _dot(later, jnp.where(tie, 1.0, 0.0).astype(BF16)) + seen
            keys_ref[c] = jnp.where(tie & (rank <= excess), thr - 1, keys)
            return rank[0:1]

        lax.fori_loop(0, n_kt, tie_body, jnp.zeros((1, Q_TILE), F32))

    def qk_tiles(c0, cnt, init):
        k0 = pl.multiple_of(c0 * K_TILE, K_TILE)
        lg = _dot_nt(qlat_ref[...].reshape(rows, D_C), ckv_ref[pl.ds(k0, cnt * K_TILE), :])
        for u in range(cnt):
            c = c0 + u
            mb = jnp.where(keys_ref[c] >= thr, 0.0, NEG_INF).T
            t_idx = jnp.minimum(qb - c, BIAS_CLASSES - 1)
            for hh in range(H_A):
                l = lg[hh * Q_TILE:(hh + 1) * Q_TILE, u * K_TILE:(u + 1) * K_TILE] + (mb + bt_ref[t_idx, hh])
                lg_ref[c, hh] = l
                top = jnp.maximum(l[:, :half], l[:, half:])
                mx_ref[hh] = top if (init and u == 0) else jnp.maximum(mx_ref[hh], top)

    for_tile_groups_init(qk_tiles)

    m = jnp.max(mx_ref[...], axis=-1, keepdims=True)

    def pv_tiles(c0, cnt, init):
        k0 = pl.multiple_of(c0 * K_TILE, K_TILE)
        ps = [jnp.exp2(lg_ref[c0 + u] - m) for u in range(cnt)]
        part = ps[0][:, :, :half] + ps[0][:, :, half:]
        for p in ps[1:]:
            part = part + (p[:, :, :half] + p[:, :, half:])
        p16 = jnp.concatenate([p.reshape(rows, K_TILE).astype(BF16) for p in ps], axis=1)
        pv = _dot(p16, ckv_ref[pl.ds(k0, cnt * K_TILE), :])
        if init:
            ss_ref[...] = part
            oacc_ref[...] = pv
        else:
            ss_ref[...] += part
            oacc_ref[...] += pv

    for_tile_groups_init(pv_tiles)
    inv_denom = 1.0 / jnp.sum(ss_ref[...], axis=-1, keepdims=True)

    for hh in range(H_A):
        oh = _dot(oacc_ref[hh * Q_TILE:(hh + 1) * Q_TILE, :].astype(BF16), wuv_ref[hh])
        zz = z_ref[:, hh * D_V:(hh + 1) * D_V]
        y_ref[:, hh * D_V:(hh + 1) * D_V] = (oh * inv_denom[hh] * (zz * _sigmoid(zz))).astype(BF16)


def _a_attn(qidx, wt, kidx, qlat, ckv, z, bt, wuv, batch, seq):
    n = batch * seq
    nq = seq // Q_TILE
    n_kt = seq // K_TILE
    topk = min(TOPK_MAX, seq // 4)
    grid = (batch, nq)
    kern = functools.partial(_a_attn_kernel, topk=topk, n_q=nq)
    return pl.pallas_call(
        kern,
        grid=grid,
        in_specs=[pl.BlockSpec((H_I, Q_TILE, D_I), lambda b, q: (0, b * nq + q, 0)),
                  pl.BlockSpec((H_I, Q_TILE), lambda b, q: (0, b * nq + q)),
                  pl.BlockSpec((seq, D_I), lambda b, q: (b, 0)),
                  pl.BlockSpec((H_A, Q_TILE, D_C), lambda b, q: (0, b * nq + q, 0)),
                  pl.BlockSpec((seq, D_C), lambda b, q: (b, 0)),
                  pl.BlockSpec((Q_TILE, H_A * D_V), lambda b, q: (b * nq + q, 0)),
                  pl.BlockSpec(bt.shape, lambda b, q: (0, 0, 0, 0), pipeline_mode=pl.Buffered(1)),
                  pl.BlockSpec(wuv.shape, lambda b, q: (0, 0, 0), pipeline_mode=pl.Buffered(1))],
        out_specs=pl.BlockSpec((Q_TILE, H_A * D_V), lambda b, q: (b * nq + q, 0)),
        out_shape=jax.ShapeDtypeStruct((n, H_A * D_V), BF16),
        scratch_shapes=[pltpu.VMEM((n_kt, K_TILE, Q_TILE), I32),
                        pltpu.VMEM((n_kt, K_TILE // PACKED_ROWS, PACKED_ROWS, Q_TILE), I16),
                        pltpu.VMEM((n_kt, K_TILE // PACKED_ROWS, PACKED_ROWS, Q_TILE), I16),
                        pltpu.VMEM((n_kt, H_A, Q_TILE, K_TILE), F32),
                        pltpu.VMEM((H_A, Q_TILE, K_TILE // 2), F32),
                        pltpu.VMEM((H_A, Q_TILE, K_TILE // 2), F32),
                        pltpu.VMEM((H_A * Q_TILE, D_C), F32)],
        compiler_params=_params("parallel", "arbitrary"),
        name="a_attn",
    )(qidx, wt, kidx, qlat, ckv, z, bt, wuv)


def _b_proj_kernel(x_ref, nw_ref, win_ref, cw_ref, alog_ref, dtb_ref, alogc_ref, dtbc_ref,
                   q_ref, k_ref, v_ref, beta_ref, g_ref, gt_ref, z_ref,
                   buf_ref, wqkv_ref, wba_ref, wbat_ref, wz_ref, *, tile):
    s = pl.program_id(1)
    b_qkv = H_B * (2 * D_K + D_VB)

    @pl.when((pl.program_id(0) == 0) & (s == 0))
    def _():
        wqkv_ref[...] = win_ref[:, :b_qkv].astype(BF16)
        slab = win_ref[:, b_qkv:b_qkv + LANES]
        wba_ref[...] = slab.astype(BF16)
        wbat_ref[...] = slab.T.astype(BF16)
        wz_ref[...] = win_ref[:, b_qkv + 2 * H_B:].astype(BF16)

    h = _rms(x_ref[...], nw_ref[...]).astype(BF16)
    z_ref[...] = _dot(h, wz_ref[...])
    ba = _dot(h, wba_ref[...])
    beta_ref[...] = _sigmoid(ba[:, :H_B])
    g_ref[...] = -jnp.exp(alog_ref[...]) * _softplus(ba[:, H_B:2 * H_B] + dtb_ref[...])
    bat = _dot_nt(wbat_ref[0:2 * H_B, :], h)
    gt_ref[...] = -jnp.exp(alogc_ref[...]) * _softplus(bat[H_B:, :] + dtbc_ref[...])

    @pl.when(s == 0)
    def _():
        buf_ref[0:8, :] = jnp.zeros((8, buf_ref.shape[1]), F32)

    width = H_B * D_K
    for sec, out_ref in enumerate((q_ref, k_ref, v_ref)):
        cols = slice(sec * width, (sec + 1) * width)
        pre = _dot(h, wqkv_ref[:, cols])
        buf_ref[8:8 + tile, cols] = pre
        acc = pre * cw_ref[CONV_W - 1:CONV_W, cols]
        for w in range(CONV_W - 1):
            acc = acc + buf_ref[8 - (CONV_W - 1) + w:8 - (CONV_W - 1) + w + tile, cols] * cw_ref[w:w + 1, cols]
        buf_ref[0:8, cols] = buf_ref[tile:tile + 8, cols]
        y = acc * _sigmoid(acc)
        if sec == 2:
            out_ref[...] = y
        else:
            scale = (D_K ** -0.5) if sec == 0 else 1.0
            for hh in range(H_B):
                yy = y[:, hh * D_K:(hh + 1) * D_K]
                nrm = lax.rsqrt(jnp.sum(yy * yy, axis=-1, keepdims=True) + EPS)
                out_ref[:, hh * D_K:(hh + 1) * D_K] = yy * (nrm * scale)


def _b_proj(x2, nw, win_all, mixer_layer, cw, alog, dtb, batch, seq, tile):
    n, d = x2.shape
    ns = seq // tile
    grid = (batch, ns)
    tok = lambda w: pl.BlockSpec((tile, w), lambda b, s: (b * ns + s, 0))
    width = H_B * D_K
    kern = functools.partial(_b_proj_kernel, tile=tile)
    return pl.pallas_call(
        kern,
        grid=grid,
        in_specs=[tok(d), _const_spec(nw.shape),
                  pl.BlockSpec((None,) + win_all.shape[1:], lambda b, s: (mixer_layer, 0, 0),
                               pipeline_mode=pl.Buffered(1)),
                  _const_spec(cw.shape),
                  _const_spec((1, H_B)), _const_spec((1, H_B)), _const_spec((H_B, 1)), _const_spec((H_B, 1))],
        out_specs=[tok(width), tok(width), tok(width), tok(H_B), tok(H_B),
                   pl.BlockSpec((H_B, tile), lambda b, s: (0, b * ns + s)), tok(width)],
        out_shape=[jax.ShapeDtypeStruct((n, width), F32)] * 3
        + [jax.ShapeDtypeStruct((n, H_B), F32)] * 2
        + [jax.ShapeDtypeStruct((H_B, n), F32), jax.ShapeDtypeStruct((n, width), F32)],
        scratch_shapes=[pltpu.VMEM((tile + 8, 3 * width), F32),
                        pltpu.VMEM((d, 3 * width), BF16), pltpu.VMEM((d, LANES), BF16),
                        pltpu.VMEM((LANES, d), BF16), pltpu.VMEM((d, width), BF16)],
        compiler_params=_params("arbitrary", "arbitrary"),
        name="b_proj",
    )(x2, nw, win_all, cw, alog.reshape(1, H_B), dtb.reshape(1, H_B),
      alog.reshape(H_B, 1), dtb.reshape(H_B, 1))


def _b_gdn_kernel(q_ref, k_ref, v_ref, z_ref, beta_ref, g_ref, gt_ref, go_ref, y_ref, state_ref):
    hb = pl.program_id(1)
    s = pl.program_id(2)
    t = GDN_TILE
    n_chunks = t // CHUNK
    log_c = int(math.log2(CHUNK))
    heads = range(GDN_HEADS)

    @pl.when(s == 0)
    def _():
        state_ref[...] = jnp.zeros(state_ref.shape, F32)

    r = lax.broadcasted_iota(I32, (t, t), 0)
    c = lax.broadcasted_iota(I32, (t, t), 1)
    xs = r ^ c
    same = xs < CHUNK
    incl = same & (r >= c)
    upper = same & (r <= c)
    eye = (r == c).astype(F32)
    level = [(lax.shift_right_logical(xs, lb) == 1) & ((r & (1 << lb)) != 0) for lb in range(log_c)]
    lane8 = lax.broadcasted_iota(I32, (t, H_B), 1)
    sub8 = lax.broadcasted_iota(I32, (H_B, t), 0)

    b_col, decay, eg_col, ekl_col, egl_col = [], [], [], [], []
    for j in heads:
        head = hb * GDN_HEADS + j
        g_col = jnp.sum(jnp.where(lane8 == head, g_ref[...], 0.0), axis=1, keepdims=True)
        b_col.append(jnp.sum(jnp.where(lane8 == head, beta_ref[...], 0.0), axis=1, keepdims=True))
        g_row = jnp.sum(jnp.where(sub8 == head, gt_ref[...], 0.0), axis=0, keepdims=True)
        g_rows = jnp.broadcast_to(g_row, (t, t))
        gc_col = jnp.sum(jnp.where(incl, g_rows, 0.0), axis=1, keepdims=True)
        gl_col = jnp.sum(jnp.where(same, g_rows, 0.0), axis=1, keepdims=True)
        gc_row = jnp.sum(jnp.where(upper, jnp.broadcast_to(g_col, (t, t)), 0.0), axis=0, keepdims=True)
        decay.append(jnp.where(incl, jnp.exp(jnp.minimum(gc_col - gc_row, 0.0)), 0.0))
        eg_col.append(jnp.exp(gc_col))
        ekl_col.append(jnp.exp(gl_col - gc_col))
        egl_col.append(jnp.exp(gl_col))

    lmat, aqk, rhs, q_dec, k_dec = [], [], [], [], []
    for j in heads:
        hs = slice(j * D_K, (j + 1) * D_K)
        qf, kf, vf = q_ref[:, hs], k_ref[:, hs], v_ref[:, hs]
        kb = kf * b_col[j]
        k16 = kf.astype(BF16)
        gram = _dot_nt(jnp.concatenate([kb.astype(BF16), qf.astype(BF16)], axis=0), k16)
        lmat.append(gram[:t] * decay[j])
        aqk.append(jnp.where(incl, gram[t:] * decay[j], 0.0).astype(BF16))
        rhs.append(jnp.concatenate([(vf * b_col[j]).astype(BF16), (kb * eg_col[j]).astype(BF16)], axis=1))
        q_dec.append(qf * eg_col[j])
        k_dec.append((kf * ekl_col[j]).astype(BF16))

    tinv = [eye - jnp.where(level[0], lmat[j], 0.0) for j in heads]
    for lb in range(1, log_c):
        half = 1 << lb
        t16 = [tinv[j].astype(BF16) for j in heads]
        if half < 8:
            y16 = [_dot(jnp.where(level[lb], lmat[j], 0.0).astype(BF16), t16[j]).astype(BF16) for j in heads]
            tinv = [tinv[j] - _dot(t16[j], y16[j]) for j in heads]
            continue
        pairs = t // (2 * half)
        split = lambda a: a.reshape(pairs, 2, half, t)
        lower = lambda a: split(a)[:, 1].reshape(t // 2, t)
        i = lax.broadcasted_iota(I32, (t // 2, t), 0)
        r_low = lax.shift_left(lax.shift_right_logical(i, lb), lb + 1) + half + (i & (half - 1))
        in_b = lax.shift_right_logical(r_low ^ lax.broadcasted_iota(I32, (t // 2, t), 1), lb) == 1
        y_low = [_dot(jnp.where(in_b, lower(lmat[j]), 0.0).astype(BF16), t16[j]) for j in heads]
        zeros = jnp.zeros((pairs, half, t), F32)
        y16 = [jnp.stack([zeros, y_low[j].reshape(pairs, half, t)], axis=1).reshape(t, t).astype(BF16) for j in heads]
        t_low = [lower(tinv[j]) for j in heads]
        t_low = [t_low[j] - _dot(t_low[j].astype(BF16), y16[j]) for j in heads]
        tinv = [jnp.stack([split(tinv[j])[:, 0], t_low[j].reshape(pairs, half, t)], axis=1).reshape(t, t)
                for j in heads]

    sol16 = [_dot(tinv[j].astype(BF16), rhs[j]).astype(BF16) for j in heads]
    aux = [_dot(aqk[j], sol16[j]) for j in heads]
    q_eff = [(q_dec[j] - aux[j][:, D_VB:]).astype(BF16) for j in heads]
    kw = [[_dot_tn(k_dec[j][ci * CHUNK:(ci + 1) * CHUNK], sol16[j][ci * CHUNK:(ci + 1) * CHUNK])
           for ci in range(n_chunks)] for j in heads]

    state = [state_ref[j] for j in heads]
    outs = [[] for _ in heads]
    for ci in range(n_chunks):
        rs = slice(ci * CHUNK, (ci + 1) * CHUNK)
        for j in heads:
            s16 = state[j].astype(BF16)
            both = _dot(jnp.concatenate([q_eff[j][rs], kw[j][ci][:, D_VB:].astype(BF16)], axis=0), s16)
            outs[j].append(both[:CHUNK] + aux[j][rs, :D_VB])
            state[j] = state[j] * egl_col[j][ci * CHUNK:ci * CHUNK + 1, :] + kw[j][ci][:, :D_VB] - both[CHUNK:]

    for j in heads:
        hs = slice(j * D_K, (j + 1) * D_K)
        state_ref[j] = state[j]
        o = jnp.concatenate(outs[j], axis=0)
        zz = z_ref[:, hs]
        y_ref[:, hs] = (_rms(o, go_ref[...]) * (zz * _sigmoid(zz))).astype(BF16)


def _b_gdn(q, k, v, z, beta, g, gt, go, batch, seq):
    n = batch * seq
    ns = seq // GDN_TILE
    nh = H_B // GDN_HEADS
    grid = (batch, nh, ns)
    wide = pl.BlockSpec((GDN_TILE, GDN_HEADS * D_K), lambda b, h, s: (b * ns + s, h))
    narrow = pl.BlockSpec((GDN_TILE, H_B), lambda b, h, s: (b * ns + s, 0))
    return pl.pallas_call(
        _b_gdn_kernel,
        grid=grid,
        in_specs=[wide, wide, wide, wide, narrow, narrow,
                  pl.BlockSpec((H_B, GDN_TILE), lambda b, h, s: (0, b * ns + s)),
                  _const_spec(go.shape)],
        out_specs=wide,
        out_shape=jax.ShapeDtypeStruct((n, H_B * D_VB), BF16),
        scratch_shapes=[pltpu.VMEM((GDN_HEADS, D_K, D_VB), F32)],
        compiler_params=_params("parallel", "parallel", "arbitrary"),
        name="b_gdn",
    )(q, k, v, z, beta, g, gt, go)


def _out_ple_kernel(x_ref, y_ref, p_ref, wout_ref, pn_ref, wgate_ref, wproj_ref, o_ref, wout16, wgate16, wproj16):
    @pl.when(pl.program_id(0) == 0)
    def _():
        wout16[...] = wout_ref[...].astype(BF16)
        wgate16[...] = wgate_ref[...].astype(BF16)
        wproj16[...] = wproj_ref[...].astype(BF16)

    x1 = x_ref[...] + _dot(y_ref[...], wout16[...])
    hn = _rms(x1, pn_ref[...]).astype(BF16)
    gate = _sigmoid(_dot(hn, wgate16[...]))
    o_ref[...] = x1 + gate * _dot(p_ref[...].astype(BF16), wproj16[...])


def _out_ple(x2, y, p2, layer, wout_all, mixer_layer, pn_all, wgate_all, wproj_all, tile):
    n, d = x2.shape
    tok = lambda w: pl.BlockSpec((tile, w), lambda i: (i, 0))
    once = lambda arr, idx: pl.BlockSpec((None,) + arr.shape[1:], lambda i: (idx,) + (0,) * (arr.ndim - 1),
                                         pipeline_mode=pl.Buffered(1))
    return pl.pallas_call(
        _out_ple_kernel,
        grid=(n // tile,),
        in_specs=[tok(d), tok(y.shape[1]),
                  pl.BlockSpec((tile, p2.shape[1]), lambda i: (layer * (n // tile) + i, 0)),
                  once(wout_all, mixer_layer), once(pn_all, layer), once(wgate_all, layer), once(wproj_all, layer)],
        out_specs=tok(d),
        out_shape=jax.ShapeDtypeStruct((n, d), F32),
        scratch_shapes=[pltpu.VMEM(wout_all.shape[1:], BF16), pltpu.VMEM(wgate_all.shape[1:], BF16),
                        pltpu.VMEM(wproj_all.shape[1:], BF16)],
        compiler_params=_params("arbitrary"),
        name="out_ple",
    )(x2, y, p2, wout_all, pn_all, wgate_all, wproj_all)


def _t5_bucket(rel):
    max_exact = N_BUCKETS // 2
    rel = jnp.maximum(rel, 0)
    rel_f = jnp.maximum(rel, 1).astype(F32)
    log_ratio = jnp.log(rel_f / max_exact) / math.log(MAX_DISTANCE / max_exact)
    large = max_exact + (log_ratio * (N_BUCKETS - max_exact)).astype(I32)
    large = jnp.minimum(large, N_BUCKETS - 1)
    return jnp.where(rel < max_exact, rel, large)


def _bias_tiles(rel_bias):
    span = Q_TILE + K_TILE
    m = jnp.arange(span, dtype=I32)
    key_minus_query = jnp.where(m < K_TILE, m, m - span)
    scaled = rel_bias.astype(F32) * LOG2_E
    tiles = []
    for cls in range(BIAS_CLASSES):
        w = scaled[_t5_bucket(cls * Q_TILE - key_minus_query)].T
        skew = jnp.tile(w, (1, Q_TILE))[:, :Q_TILE * (span - 1)].reshape(H_A, Q_TILE, span - 1)
        tiles.append(skew[:, :, :K_TILE])
    return jnp.stack(tiles)


def kernel(x, p, norm_w, a_w_in, a_g_cq, a_w_uq, a_w_uk, a_g_q, a_g_kv, a_w_iq, a_w_uv, a_w_out, rel_bias, b_w_in, b_conv_w, b_a_log, b_dt_bias, b_g_o, b_w_out, ple_norm, ple_w_gate, ple_w_proj):
    batch, seq, d_model = x.shape
    depth = p.shape[0]
    n = batch * seq
    assert seq % Q_TILE == 0 and seq % K_TILE == 0 and seq % GDN_TILE == 0 and H_B % GDN_HEADS == 0
    assert n % PROJ_TILE == 0 and seq % CONV_TILE == 0
    x2 = x.reshape(n, d_model)
    p2 = p.reshape(depth * n, p.shape[-1])
    bt = _bias_tiles(rel_bias)
    row = lambda a: a.reshape(1, -1).astype(F32)
    for i in range(depth):
        j = i // 2
        if i % 2 == 0:
            qlat, qidx, wt, ckv, kidx, z = _a_proj(
                x2, row(norm_w[i]), a_w_in, j, row(a_g_cq[j]), row(a_g_kv[j]),
                a_w_uq, a_w_uk, row(a_g_q[j]), a_w_iq, tile=PROJ_TILE)
            y = _a_attn(qidx, wt, kidx, qlat, ckv, z, bt, a_w_uv[j].astype(BF16), batch, seq)
            w_out_all = a_w_out
        else:
            q, k, v, beta, g, gt, z = _b_proj(
                x2, row(norm_w[i]), b_w_in, j, b_conv_w[j].astype(F32),
                b_a_log[j].astype(F32), b_dt_bias[j].astype(F32), batch, seq, tile=CONV_TILE)
            y = _b_gdn(q, k, v, z, beta, g, gt, row(b_g_o[j]), batch, seq)
            w_out_all = b_w_out
        x2 = _out_ple(x2, y, p2, i, w_out_all, j, ple_norm.reshape(depth, 1, d_model).astype(F32), ple_w_gate,
                      ple_w_proj, tile=PROJ_TILE)
    return x2.reshape(batch, seq, d_model)
```

```python
import functools
import math

import jax
import jax.numpy as jnp
from jax import lax
from jax.experimental import pallas as pl
from jax.experimental.pallas import tpu as pltpu

F32 = jnp.float32
BF16 = jnp.bfloat16
I32 = jnp.int32
I16 = jnp.int16

EPS = 1e-6
NEG_INF = -1e30
LOG2_E = math.log2(math.e)
INT_MIN = -(2 ** 31)
HALF_BITS = 16
I16_MIN = -(1 << (HALF_BITS - 1))

LANES = 128
PACKED_ROWS = 16
VMEM_LIMIT_BYTES = 56 * 1024 * 1024

H_A = 8
D_C = 256
D_CQ = 256
D_I = 128
H_I = 8
D_NOPE = 128
D_V = 128
TOPK_MAX = 256
N_BUCKETS = 32
MAX_DISTANCE = 128
Q_TILE = 256
K_TILE = 256
BIAS_CLASSES = -(-(MAX_DISTANCE + K_TILE - 1) // Q_TILE) + 1
H_B = 8
D_K = 128
D_VB = 128
CONV_W = 4
CHUNK = 64
GDN_TILE = 256
GDN_HEADS = 8
PROJ_TILE = 1024
CONV_TILE = 512


def _dot(a, b):
    return jnp.dot(a, b, preferred_element_type=F32)


def _dot_nt(a, b):
    return lax.dot_general(a, b, (((1,), (1,)), ((), ())), preferred_element_type=F32)


def _dot_tn(a, b):
    return lax.dot_general(a, b, (((0,), (0,)), ((), ())), preferred_element_type=F32)


def _rms(x, gain=None):
    y = x * lax.rsqrt(jnp.mean(x * x, axis=-1, keepdims=True) + EPS)
    return y if gain is None else y * gain


def _sigmoid(x):
    return 1.0 / (1.0 + jnp.exp2(x * -LOG2_E))


def _softplus(x):
    return jnp.maximum(x, 0.0) + jnp.log1p(jnp.exp(-jnp.abs(x)))


def _params(*semantics):
    return pltpu.CompilerParams(dimension_semantics=semantics, vmem_limit_bytes=VMEM_LIMIT_BYTES)


def _const_spec(shape):
    nd = len(shape)
    return pl.BlockSpec(shape, lambda *_: (0,) * nd)


def _a_proj_kernel(x_ref, nw_ref, win_ref, gcq_ref, gkv_ref, wuq32_ref, wuk32_ref, gq_ref, wiq32_ref,
                   qlat_ref, qidx_ref, wt_ref, ckv_ref, kidx_ref, z_ref,
                   wlat_ref, wz_ref, wuq_ref, wuk_ref, wiq_ref):
    n_lat = D_CQ + D_C + D_I

    @pl.when(pl.program_id(0) == 0)
    def _():
        wlat_ref[...] = win_ref[:n_lat + LANES, :].T.astype(BF16)
        for c in range(0, H_A * D_V, 256):
            wz_ref[:, c:c + 256] = win_ref[n_lat + H_I + c:n_lat + H_I + c + 256, :].T.astype(BF16)
        wuq_ref[...] = wuq32_ref[...].astype(BF16)
        wuk_ref[...] = wuk32_ref[...].astype(BF16)
        wiq_ref[...] = wiq32_ref[...].astype(BF16)

    h = _rms(x_ref[...], nw_ref[...]).astype(BF16)
    lat = _dot(h, wlat_ref[...])
    cq = _rms(lat[:, :D_CQ], gcq_ref[...]).astype(BF16)
    ckv_ref[...] = _rms(lat[:, D_CQ:D_CQ + D_C], gkv_ref[...]).astype(BF16)
    kidx_ref[...] = _rms(lat[:, D_CQ + D_C:n_lat]).astype(BF16)
    wt_ref[...] = lat[:, n_lat:].T[:H_I] * (H_I ** -0.5)
    z_ref[...] = _dot(h, wz_ref[...])
    qn = _dot(cq, wuq_ref[...]).astype(BF16)
    for hh in range(H_A):
        ql = _dot(qn[:, hh * D_NOPE:(hh + 1) * D_NOPE], wuk_ref[hh])
        qlat_ref[hh] = (_rms(ql, gq_ref[...]) * (D_C ** -0.5 * LOG2_E)).astype(BF16)
    qi = _dot(cq, wiq_ref[...]) * (D_I ** -0.5)
    for hh in range(H_I):
        qidx_ref[hh] = qi[:, hh * D_I:(hh + 1) * D_I].astype(BF16)


def _a_proj(x2, nw, win_all, mixer_layer, gcq, gkv, wuq_all, wuk_all, gq, wiq_all, tile):
    n, d = x2.shape
    once = lambda arr: pl.BlockSpec((None,) + arr.shape[1:], lambda i: (mixer_layer,) + (0,) * (arr.ndim - 1),
                                    pipeline_mode=pl.Buffered(1))
    grid = (n // tile,)
    tok = lambda w: pl.BlockSpec((tile, w), lambda i: (i, 0))
    return pl.pallas_call(
        _a_proj_kernel,
        grid=grid,
        in_specs=[tok(d), _const_spec(nw.shape), once(win_all), _const_spec(gcq.shape), _const_spec(gkv.shape),
                  once(wuq_all), once(wuk_all), _const_spec(gq.shape), once(wiq_all)],
        out_specs=[pl.BlockSpec((H_A, tile, D_C), lambda i: (0, i, 0)),
                   pl.BlockSpec((H_I, tile, D_I), lambda i: (0, i, 0)),
                   pl.BlockSpec((H_I, tile), lambda i: (0, i)),
                   tok(D_C), tok(D_I), tok(H_A * D_V)],
        out_shape=[jax.ShapeDtypeStruct((H_A, n, D_C), BF16),
                   jax.ShapeDtypeStruct((H_I, n, D_I), BF16),
                   jax.ShapeDtypeStruct((H_I, n), F32),
                   jax.ShapeDtypeStruct((n, D_C), BF16),
                   jax.ShapeDtypeStruct((n, D_I), BF16),
                   jax.ShapeDtypeStruct((n, H_A * D_V), F32)],
        scratch_shapes=[pltpu.VMEM((d, D_CQ + D_C + D_I + LANES), BF16), pltpu.VMEM((d, H_A * D_V), BF16),
                        pltpu.VMEM(wuq_all.shape[1:], BF16), pltpu.VMEM(wuk_all.shape[1:], BF16),
                        pltpu.VMEM(wiq_all.shape[1:], BF16)],
        compiler_params=_params("arbitrary"),
        name="a_proj",
    )(x2, nw, win_all, gcq, gkv, wuq_all, wuk_all, gq, wiq_all)


def _a_attn_kernel(qidx_ref, wt_ref, kidx_ref, qlat_ref, ckv_ref, z_ref, bt_ref, wuv_ref,
                   y_ref, keys_ref, hi_ref, lo_ref, lg_ref, mx_ref, ss_ref, oacc_ref, *, topk, n_q):
    qb = pl.program_id(1)
    n_kt = qb + 1
    rows = H_A * Q_TILE
    half = K_TILE // 2

    def for_tile_groups(body):
        def quad(i, carry):
            body(4 * i, 4)
            return carry

        n_quads = lax.shift_right_logical(n_kt, 2)
        lax.fori_loop(0, n_quads, quad, 0)

        @pl.when((n_kt & 2) == 2)
        def _():
            body(4 * n_quads, 2)

        @pl.when((n_kt & 1) == 1)
        def _():
            body(n_kt - 1, 1)

    def for_tile_groups_init(body):
        odd = (n_kt & 1) == 1

        @pl.when(odd)
        def _():
            body(0, 1, True)

        @pl.when(jnp.logical_not(odd))
        def _():
            body(0, 2, True)

        start = jnp.where(odd, 1, 2)

        def pair(i, carry):
            body(start + 2 * i, 2, False)
            return carry

        lax.fori_loop(0, lax.shift_right_logical(n_kt - start, 1), pair, 0)

    def idx_tiles(c0, cnt):
        k0 = pl.multiple_of(c0 * K_TILE, K_TILE)
        qidx = qidx_ref[...].reshape(H_I * Q_TILE, D_I)
        lg = _dot_nt(kidx_ref[pl.ds(k0, cnt * K_TILE), :], qidx)
        sc = jnp.maximum(lg[:, :Q_TILE], 0.0) * wt_ref[0:1, :]
        for hh in range(1, H_I):
            sc = sc + jnp.maximum(lg[:, hh * Q_TILE:(hh + 1) * Q_TILE], 0.0) * wt_ref[hh:hh + 1, :]
        bits = lax.bitcast_convert_type(sc, I32)
        sign = lax.shift_right_arithmetic(bits, 31)
        skey = (bits ^ (sign & 0x7FFFFFFF)) - sign
        key_pos = k0 + lax.broadcasted_iota(I32, (cnt * K_TILE, Q_TILE), 0)
        q_pos = qb * Q_TILE + lax.broadcasted_iota(I32, (cnt * K_TILE, Q_TILE), 1)
        skey = jnp.where(key_pos <= q_pos, skey, INT_MIN)
        hi = lax.shift_right_arithmetic(skey, HALF_BITS).astype(I16)
        lo = ((skey & ((1 << HALF_BITS) - 1)) + I16_MIN).astype(I16)
        for u in range(cnt):
            ks = slice(u * K_TILE, (u + 1) * K_TILE)
            keys_ref[c0 + u] = skey[ks]
            hi_ref[c0 + u] = hi[ks].reshape(K_TILE // PACKED_ROWS, PACKED_ROWS, Q_TILE)
            lo_ref[c0 + u] = lo[ks].reshape(K_TILE // PACKED_ROWS, PACKED_ROWS, Q_TILE)

    for_tile_groups(idx_tiles)

    one = jnp.ones((), BF16)
    zero = jnp.zeros((), BF16)

    def threshold(n_tiles):
        if n_tiles * Q_TILE <= topk:
            return jnp.full((1, Q_TILE), INT_MIN + 1, I32), jnp.zeros((1, Q_TILE), F32)

        def count(ref16, bound16, strict):
            acc = jnp.zeros((PACKED_ROWS, Q_TILE), BF16)
            for c in range(n_tiles):
                tile = ref16[c]
                hit = jnp.where(tile > bound16[None] if strict else tile >= bound16[None], one, zero)
                parts = [hit[u] for u in range(K_TILE // PACKED_ROWS)]
                while len(parts) > 1:
                    parts = [parts[u] + parts[u + 1] for u in range(0, len(parts), 2)]
                acc = acc + parts[0]
            return jnp.sum(acc.astype(F32), axis=0, keepdims=True)

        def kth_largest(ref16, need, n_all):
            def bit_body(i, carry):
                thr, n_ge = carry
                cand = thr + lax.shift_left(jnp.int32(1), HALF_BITS - 1 - i)
                n_cand = count(ref16, cand.astype(I16), False)
                ok = n_cand >= need
                return jnp.where(ok, cand, thr), jnp.where(ok, n_cand, n_ge)

            return lax.fori_loop(0, HALF_BITS, bit_body, (jnp.full((PACKED_ROWS, Q_TILE), I16_MIN, I32), n_all))

        n_all = jnp.full((1, Q_TILE), float(n_tiles * K_TILE), F32)
        hi_thr, n_ge_hi = kth_largest(hi_ref, float(topk), n_all)
        hi_thr16 = hi_thr.astype(I16)
        n_gt_hi = count(hi_ref, hi_thr16, True)
        for c in range(n_tiles):
            lo_ref[c] = jnp.where(hi_ref[c] == hi_thr16[None], lo_ref[c], jnp.int16(I16_MIN))
        lo_thr, n_ge_lo = kth_largest(lo_ref, float(topk) - n_gt_hi, n_ge_hi - n_gt_hi)
        thr = hi_thr[0:1] * (1 << HALF_BITS) + (lo_thr[0:1] - I16_MIN)
        thr = jnp.maximum(thr, INT_MIN + 1)
        return thr, n_gt_hi + n_ge_lo - float(topk)

    thr, excess = lax.switch(qb, [functools.partial(threshold, n) for n in range(1, n_q + 1)])

    @pl.when(jnp.max(excess) > 0.0)
    def _():
        kr = lax.broadcasted_iota(I32, (K_TILE, K_TILE), 0)
        kc = lax.broadcasted_iota(I32, (K_TILE, K_TILE), 1)
        later = jnp.where(kc >= kr, 1.0, 0.0).astype(BF16)

        def tie_body(i, seen):
            c = n_kt - 1 - i
            keys = keys_ref[c]
            tie = keys == thr
            rank = _dot(later, jnp.where(tie, 1.0, 0.0).astype(BF16)) + seen
            keys_ref[c] = jnp.where(tie & (rank <= excess), thr - 1, keys)
            return rank[0:1]

        lax.fori_loop(0, n_kt, tie_body, jnp.zeros((1, Q_TILE), F32))

    def qk_tiles(c0, cnt, init):
        k0 = pl.multiple_of(c0 * K_TILE, K_TILE)
        lg = _dot_nt(qlat_ref[...].reshape(rows, D_C), ckv_ref[pl.ds(k0, cnt * K_TILE), :])
        for u in range(cnt):
            c = c0 + u
            mb = jnp.where(keys_ref[c] >= thr, 0.0, NEG_INF).T
            t_idx = jnp.minimum(qb - c, BIAS_CLASSES - 1)
            for hh in range(H_A):
                l = lg[hh * Q_TILE:(hh + 1) * Q_TILE, u * K_TILE:(u + 1) * K_TILE] + (mb + bt_ref[t_idx, hh])
                lg_ref[c, hh] = l
                top = jnp.maximum(l[:, :half], l[:, half:])
                mx_ref[hh] = top if (init and u == 0) else jnp.maximum(mx_ref[hh], top)

    for_tile_groups_init(qk_tiles)

    m = jnp.max(mx_ref[...], axis=-1, keepdims=True)

    def pv_tiles(c0, cnt, init):
        k0 = pl.multiple_of(c0 * K_TILE, K_TILE)
        ps = [jnp.exp2(lg_ref[c0 + u] - m) for u in range(cnt)]
        part = ps[0][:, :, :half] + ps[0][:, :, half:]
        for p in ps[1:]:
            part = part + (p[:, :, :half] + p[:, :, half:])
        p16 = jnp.concatenate([p.reshape(rows, K_TILE).astype(BF16) for p in ps], axis=1)
        pv = _dot(p16, ckv_ref[pl.ds(k0, cnt * K_TILE), :])
        if init:
            ss_ref[...] = part
            oacc_ref[...] = pv
        else:
            ss_ref[...] += part
            oacc_ref[...] += pv

    for_tile_groups_init(pv_tiles)
    inv_denom = 1.0 / jnp.sum(ss_ref[...], axis=-1, keepdims=True)

    for hh in range(H_A):
        oh = _dot(oacc_ref[hh * Q_TILE:(hh + 1) * Q_TILE, :].astype(BF16), wuv_ref[hh])
        zz = z_ref[:, hh * D_V:(hh + 1) * D_V]
        y_ref[:, hh * D_V:(hh + 1) * D_V] = (oh * inv_denom[hh] * (zz * _sigmoid(zz))).astype(BF16)


def _a_attn(qidx, wt, kidx, qlat, ckv, z, bt, wuv, batch, seq):
    n = batch * seq
    nq = seq // Q_TILE
    n_kt = seq // K_TILE
    topk = min(TOPK_MAX, seq // 4)
    grid = (batch, nq)
    kern = functools.partial(_a_attn_kernel, topk=topk, n_q=nq)
    return pl.pallas_call(
        kern,
        grid=grid,
        in_specs=[pl.BlockSpec((H_I, Q_TILE, D_I), lambda b, q: (0, b * nq + q, 0)),
                  pl.BlockSpec((H_I, Q_TILE), lambda b, q: (0, b * nq + q)),
                  pl.BlockSpec((seq, D_I), lambda b, q: (b, 0)),
                  pl.BlockSpec((H_A, Q_TILE, D_C), lambda b, q: (0, b * nq + q, 0)),
                  pl.BlockSpec((seq, D_C), lambda b, q: (b, 0)),
                  pl.BlockSpec((Q_TILE, H_A * D_V), lambda b, q: (b * nq + q, 0)),
                  pl.BlockSpec(bt.shape, lambda b, q: (0, 0, 0, 0), pipeline_mode=pl.Buffered(1)),
                  pl.BlockSpec(wuv.shape, lambda b, q: (0, 0, 0), pipeline_mode=pl.Buffered(1))],
        out_specs=pl.BlockSpec((Q_TILE, H_A * D_V), lambda b, q: (b * nq + q, 0)),
        out_shape=jax.ShapeDtypeStruct((n, H_A * D_V), BF16),
        scratch_shapes=[pltpu.VMEM((n_kt, K_TILE, Q_TILE), I32),
                        pltpu.VMEM((n_kt, K_TILE // PACKED_ROWS, PACKED_ROWS, Q_TILE), I16),
                        pltpu.VMEM((n_kt, K_TILE // PACKED_ROWS, PACKED_ROWS, Q_TILE), I16),
                        pltpu.VMEM((n_kt, H_A, Q_TILE, K_TILE), F32),
                        pltpu.VMEM((H_A, Q_TILE, K_TILE // 2), F32),
                        pltpu.VMEM((H_A, Q_TILE, K_TILE // 2), F32),
                        pltpu.VMEM((H_A * Q_TILE, D_C), F32)],
        compiler_params=_params("parallel", "arbitrary"),
        name="a_attn",
    )(qidx, wt, kidx, qlat, ckv, z, bt, wuv)


def _b_proj_kernel(x_ref, nw_ref, win_ref, cw_ref, alog_ref, dtb_ref, alogc_ref, dtbc_ref,
                   q_ref, k_ref, v_ref, beta_ref, g_ref, gt_ref, z_ref,
                   buf_ref, wqkv_ref, wba_ref, wbat_ref, wz_ref, *, tile):
    s = pl.program_id(1)
    b_qkv = H_B * (2 * D_K + D_VB)

    @pl.when((pl.program_id(0) == 0) & (s == 0))
    def _():
        for c in range(0, b_qkv, 256):
            wqkv_ref[:, c:c + 256] = win_ref[c:c + 256, :].T.astype(BF16)
        slab = win_ref[b_qkv:b_qkv + LANES, :]
        wbat_ref[...] = slab.astype(BF16)
        wba_ref[...] = slab.T.astype(BF16)
        for c in range(0, wz_ref.shape[1], 256):
            wz_ref[:, c:c + 256] = win_ref[b_qkv + 2 * H_B + c:b_qkv + 2 * H_B + c + 256, :].T.astype(BF16)

    h = _rms(x_ref[...], nw_ref[...]).astype(BF16)
    z_ref[...] = _dot(h, wz_ref[...])
    ba = _dot(h, wba_ref[...])
    beta_ref[...] = _sigmoid(ba[:, :H_B])
    g_ref[...] = -jnp.exp(alog_ref[...]) * _softplus(ba[:, H_B:2 * H_B] + dtb_ref[...])
    bat = _dot_nt(wbat_ref[0:2 * H_B, :], h)
    gt_ref[...] = -jnp.exp(alogc_ref[...]) * _softplus(bat[H_B:, :] + dtbc_ref[...])

    @pl.when(s == 0)
    def _():
        buf_ref[0:8, :] = jnp.zeros((8, buf_ref.shape[1]), F32)

    width = H_B * D_K
    for sec, out_ref in enumerate((q_ref, k_ref, v_ref)):
        cols = slice(sec * width, (sec + 1) * width)
        pre = _dot(h, wqkv_ref[:, cols])
        buf_ref[8:8 + tile, cols] = pre
        acc = pre * cw_ref[CONV_W - 1:CONV_W, cols]
        for w in range(CONV_W - 1):
            acc = acc + buf_ref[8 - (CONV_W - 1) + w:8 - (CONV_W - 1) + w + tile, cols] * cw_ref[w:w + 1, cols]
        buf_ref[0:8, cols] = buf_ref[tile:tile + 8, cols]
        y = acc * _sigmoid(acc)
        if sec == 2:
            out_ref[...] = y
        else:
            scale = (D_K ** -0.5) if sec == 0 else 1.0
            for hh in range(H_B):
                yy = y[:, hh * D_K:(hh + 1) * D_K]
                nrm = lax.rsqrt(jnp.sum(yy * yy, axis=-1, keepdims=True) + EPS)
                out_ref[:, hh * D_K:(hh + 1) * D_K] = yy * (nrm * scale)


def _b_proj(x2, nw, win_all, mixer_layer, cw, alog, dtb, batch, seq, tile):
    n, d = x2.shape
    ns = seq // tile
    grid = (batch, ns)
    tok = lambda w: pl.BlockSpec((tile, w), lambda b, s: (b * ns + s, 0))
    width = H_B * D_K
    kern = functools.partial(_b_proj_kernel, tile=tile)
    return pl.pallas_call(
        kern,
        grid=grid,
        in_specs=[tok(d), _const_spec(nw.shape),
                  pl.BlockSpec((None,) + win_all.shape[1:], lambda b, s: (mixer_layer, 0, 0),
                               pipeline_mode=pl.Buffered(1)),
                  _const_spec(cw.shape),
                  _const_spec((1, H_B)), _const_spec((1, H_B)), _const_spec((H_B, 1)), _const_spec((H_B, 1))],
        out_specs=[tok(width), tok(width), tok(width), tok(H_B), tok(H_B),
                   pl.BlockSpec((H_B, tile), lambda b, s: (0, b * ns + s)), tok(width)],
        out_shape=[jax.ShapeDtypeStruct((n, width), F32)] * 3
        + [jax.ShapeDtypeStruct((n, H_B), F32)] * 2
        + [jax.ShapeDtypeStruct((H_B, n), F32), jax.ShapeDtypeStruct((n, width), F32)],
        scratch_shapes=[pltpu.VMEM((tile + 8, 3 * width), F32),
                        pltpu.VMEM((d, 3 * width), BF16), pltpu.VMEM((d, LANES), BF16),
                        pltpu.VMEM((LANES, d), BF16), pltpu.VMEM((d, width), BF16)],
        compiler_params=_params("arbitrary", "arbitrary"),
        name="b_proj",
    )(x2, nw, win_all, cw, alog.reshape(1, H_B), dtb.reshape(1, H_B),
      alog.reshape(H_B, 1), dtb.reshape(H_B, 1))


def _b_gdn_kernel(q_ref, k_ref, v_ref, z_ref, beta_ref, g_ref, gt_ref, go_ref, y_ref, state_ref):
    hb = pl.program_id(1)
    s = pl.program_id(2)
    t = GDN_TILE
    n_chunks = t // CHUNK
    log_c = int(math.log2(CHUNK))
    heads = range(GDN_HEADS)

    @pl.when(s == 0)
    def _():
        state_ref[...] = jnp.zeros(state_ref.shape, F32)

    r = lax.broadcasted_iota(I32, (t, t), 0)
    c = lax.broadcasted_iota(I32, (t, t), 1)
    xs = r ^ c
    same = xs < CHUNK
    incl = same & (r >= c)
    upper = same & (r <= c)
    eye = (r == c).astype(F32)
    level = [(lax.shift_right_logical(xs, lb) == 1) & ((r & (1 << lb)) != 0) for lb in range(log_c)]
    lane8 = lax.broadcasted_iota(I32, (t, H_B), 1)
    sub8 = lax.broadcasted_iota(I32, (H_B, t), 0)

    b_col, decay, eg_col, ekl_col, egl_col = [], [], [], [], []
    for j in heads:
        head = hb * GDN_HEADS + j
        g_col = jnp.sum(jnp.where(lane8 == head, g_ref[...], 0.0), axis=1, keepdims=True)
        b_col.append(jnp.sum(jnp.where(lane8 == head, beta_ref[...], 0.0), axis=1, keepdims=True))
        g_row = jnp.sum(jnp.where(sub8 == head, gt_ref[...], 0.0), axis=0, keepdims=True)
        g_rows = jnp.broadcast_to(g_row, (t, t))
        gc_col = jnp.sum(jnp.where(incl, g_rows, 0.0), axis=1, keepdims=True)
        gl_col = jnp.sum(jnp.where(same, g_rows, 0.0), axis=1, keepdims=True)
        gc_row = jnp.sum(jnp.where(upper, jnp.broadcast_to(g_col, (t, t)), 0.0), axis=0, keepdims=True)
        decay.append(jnp.where(incl, jnp.exp(jnp.minimum(gc_col - gc_row, 0.0)), 0.0))
        eg_col.append(jnp.exp(gc_col))
        ekl_col.append(jnp.exp(gl_col - gc_col))
        egl_col.append(jnp.exp(gl_col))

    lmat, aqk, rhs, q_dec, k_dec = [], [], [], [], []
    for j in heads:
        hs = slice(j * D_K, (j + 1) * D_K)
        qf, kf, vf = q_ref[:, hs], k_ref[:, hs], v_ref[:, hs]
        kb = kf * b_col[j]
        k16 = kf.astype(BF16)
        gram = _dot_nt(jnp.concatenate([kb.astype(BF16), qf.astype(BF16)], axis=0), k16)
        lmat.append(gram[:t] * decay[j])
        aqk.append(jnp.where(incl, gram[t:] * decay[j], 0.0).astype(BF16))
        rhs.append(jnp.concatenate([(vf * b_col[j]).astype(BF16), (kb * eg_col[j]).astype(BF16)], axis=1))
        q_dec.append(qf * eg_col[j])
        k_dec.append((kf * ekl_col[j]).astype(BF16))

    tinv = [eye - jnp.where(level[0], lmat[j], 0.0) for j in heads]
    for lb in range(1, log_c):
        half = 1 << lb
        t16 = [tinv[j].astype(BF16) for j in heads]
        if half < 8:
            y16 = [_dot(jnp.where(level[lb], lmat[j], 0.0).astype(BF16), t16[j]).astype(BF16) for j in heads]
            tinv = [tinv[j] - _dot(t16[j], y16[j]) for j in heads]
            continue
        pairs = t // (2 * half)
        split = lambda a: a.reshape(pairs, 2, half, t)
        lower = lambda a: split(a)[:, 1].reshape(t // 2, t)
        i = lax.broadcasted_iota(I32, (t // 2, t), 0)
        r_low = lax.shift_left(lax.shift_right_logical(i, lb), lb + 1) + half + (i & (half - 1))
        in_b = lax.shift_right_logical(r_low ^ lax.broadcasted_iota(I32, (t // 2, t), 1), lb) == 1
        y_low = [_dot(jnp.where(in_b, lower(lmat[j]), 0.0).astype(BF16), t16[j]) for j in heads]
        zeros = jnp.zeros((pairs, half, t), F32)
        y16 = [jnp.stack([zeros, y_low[j].reshape(pairs, half, t)], axis=1).reshape(t, t).astype(BF16) for j in heads]
        t_low = [lower(tinv[j]) for j in heads]
        t_low = [t_low[j] - _dot(t_low[j].astype(BF16), y16[j]) for j in heads]
        tinv = [jnp.stack([split(tinv[j])[:, 0], t_low[j].reshape(pairs, half, t)], axis=1).reshape(t, t)
                for j in heads]

    sol16 = [_dot(tinv[j].astype(BF16), rhs[j]).astype(BF16) for j in heads]
    aux = [_dot(aqk[j], sol16[j]) for j in heads]
    q_eff = [(q_dec[j] - aux[j][:, D_VB:]).astype(BF16) for j in heads]
    kw = [[_dot_tn(k_dec[j][ci * CHUNK:(ci + 1) * CHUNK], sol16[j][ci * CHUNK:(ci + 1) * CHUNK])
           for ci in range(n_chunks)] for j in heads]

    state = [state_ref[j] for j in heads]
    outs = [[] for _ in heads]
    for ci in range(n_chunks):
        rs = slice(ci * CHUNK, (ci + 1) * CHUNK)
        for j in heads:
            s16 = state[j].astype(BF16)
            both = _dot(jnp.concatenate([q_eff[j][rs], kw[j][ci][:, D_VB:].astype(BF16)], axis=0), s16)
            outs[j].append(both[:CHUNK] + aux[j][rs, :D_VB])
            state[j] = state[j] * egl_col[j][ci * CHUNK:ci * CHUNK + 1, :] + kw[j][ci][:, :D_VB] - both[CHUNK:]

    for j in heads:
        hs = slice(j * D_K, (j + 1) * D_K)
        state_ref[j] = state[j]
        o = jnp.concatenate(outs[j], axis=0)
        zz = z_ref[:, hs]
        y_ref[:, hs] = (_rms(o, go_ref[...]) * (zz * _sigmoid(zz))).astype(BF16)


def _b_gdn(q, k, v, z, beta, g, gt, go, batch, seq):
    n = batch * seq
    ns = seq // GDN_TILE
    nh = H_B // GDN_HEADS
    grid = (batch, nh, ns)
    wide = pl.BlockSpec((GDN_TILE, GDN_HEADS * D_K), lambda b, h, s: (b * ns + s, h))
    narrow = pl.BlockSpec((GDN_TILE, H_B), lambda b, h, s: (b * ns + s, 0))
    return pl.pallas_call(
        _b_gdn_kernel,
        grid=grid,
        in_specs=[wide, wide, wide, wide, narrow, narrow,
                  pl.BlockSpec((H_B, GDN_TILE), lambda b, h, s: (0, b * ns + s)),
                  _const_spec(go.shape)],
        out_specs=wide,
        out_shape=jax.ShapeDtypeStruct((n, H_B * D_VB), BF16),
        scratch_shapes=[pltpu.VMEM((GDN_HEADS, D_K, D_VB), F32)],
        compiler_params=_params("parallel", "parallel", "arbitrary"),
        name="b_gdn",
    )(q, k, v, z, beta, g, gt, go)


def _out_ple_kernel(x_ref, y_ref, p_ref, wout_ref, pn_ref, wgate_ref, wproj_ref, o_ref, wout16, wgate16, wproj16):
    @pl.when(pl.program_id(0) == 0)
    def _():
        wout16[...] = wout_ref[...].astype(BF16)
        wgate16[...] = wgate_ref[...].astype(BF16)
        wproj16[...] = wproj_ref[...].astype(BF16)

    x1 = x_ref[...] + _dot(y_ref[...], wout16[...])
    hn = _rms(x1, pn_ref[...]).astype(BF16)
    gate = _sigmoid(_dot(hn, wgate16[...]))
    o_ref[...] = x1 + gate * _dot(p_ref[...].astype(BF16), wproj16[...])


def _out_ple(x2, y, p2, layer, wout_all, mixer_layer, pn_all, wgate_all, wproj_all, tile):
    n, d = x2.shape
    tok = lambda w: pl.BlockSpec((tile, w), lambda i: (i, 0))
    once = lambda arr, idx: pl.BlockSpec((None,) + arr.shape[1:], lambda i: (idx,) + (0,) * (arr.ndim - 1),
                                         pipeline_mode=pl.Buffered(1))
    return pl.pallas_call(
        _out_ple_kernel,
        grid=(n // tile,),
        in_specs=[tok(d), tok(y.shape[1]),
                  pl.BlockSpec((tile, p2.shape[1]), lambda i: (layer * (n // tile) + i, 0)),
                  once(wout_all, mixer_layer), once(pn_all, layer), once(wgate_all, layer), once(wproj_all, layer)],
        out_specs=tok(d),
        out_shape=jax.ShapeDtypeStruct((n, d), F32),
        scratch_shapes=[pltpu.VMEM(wout_all.shape[1:], BF16), pltpu.VMEM(wgate_all.shape[1:], BF16),
                        pltpu.VMEM(wproj_all.shape[1:], BF16)],
        compiler_params=_params("arbitrary"),
        name="out_ple",
    )(x2, y, p2, wout_all, pn_all, wgate_all, wproj_all)


def _t5_bucket(rel):
    max_exact = N_BUCKETS // 2
    rel = jnp.maximum(rel, 0)
    rel_f = jnp.maximum(rel, 1).astype(F32)
    log_ratio = jnp.log(rel_f / max_exact) / math.log(MAX_DISTANCE / max_exact)
    large = max_exact + (log_ratio * (N_BUCKETS - max_exact)).astype(I32)
    large = jnp.minimum(large, N_BUCKETS - 1)
    return jnp.where(rel < max_exact, rel, large)


def _bias_tiles(rel_bias):
    span = Q_TILE + K_TILE
    m = jnp.arange(span, dtype=I32)
    key_minus_query = jnp.where(m < K_TILE, m, m - span)
    scaled = rel_bias.astype(F32) * LOG2_E
    tiles = []
    for cls in range(BIAS_CLASSES):
        w = scaled[_t5_bucket(cls * Q_TILE - key_minus_query)].T
        skew = jnp.tile(w, (1, Q_TILE))[:, :Q_TILE * (span - 1)].reshape(H_A, Q_TILE, span - 1)
        tiles.append(skew[:, :, :K_TILE])
    return jnp.stack(tiles)


def kernel(x, p, norm_w, a_w_in, a_g_cq, a_w_uq, a_w_uk, a_g_q, a_g_kv, a_w_iq, a_w_uv, a_w_out, rel_bias, b_w_in, b_conv_w, b_a_log, b_dt_bias, b_g_o, b_w_out, ple_norm, ple_w_gate, ple_w_proj):
    batch, seq, d_model = x.shape
    depth = p.shape[0]
    n = batch * seq
    assert seq % Q_TILE == 0 and seq % K_TILE == 0 and seq % GDN_TILE == 0 and H_B % GDN_HEADS == 0
    assert n % PROJ_TILE == 0 and seq % CONV_TILE == 0
    x2 = x.reshape(n, d_model)
    p2 = p.reshape(depth * n, p.shape[-1])
    bt = _bias_tiles(rel_bias)
    row = lambda a: a.reshape(1, -1).astype(F32)
    for i in range(depth):
        j = i // 2
        if i % 2 == 0:
            qlat, qidx, wt, ckv, kidx, z = _a_proj(
                x2, row(norm_w[i]), jnp.swapaxes(a_w_in, 1, 2), j, row(a_g_cq[j]), row(a_g_kv[j]),
                a_w_uq.reshape(-1, D_CQ, H_A * D_NOPE), a_w_uk, row(a_g_q[j]),
                a_w_iq.reshape(-1, D_CQ, H_I * D_I), tile=PROJ_TILE)
            y = _a_attn(qidx, wt, kidx, qlat, ckv, z, bt, a_w_uv[j].astype(BF16), batch, seq)
            w_out_all = a_w_out
        else:
            q, k, v, beta, g, gt, z = _b_proj(
                x2, row(norm_w[i]), jnp.swapaxes(b_w_in, 1, 2), j, b_conv_w[j].astype(F32),
                b_a_log[j].astype(F32), b_dt_bias[j].astype(F32), batch, seq, tile=CONV_TILE)
            y = _b_gdn(q, k, v, z, beta, g, gt, row(b_g_o[j]), batch, seq)
            w_out_all = b_w_out
        x2 = _out_ple(x2, y, p2, i, w_out_all, j, ple_norm.reshape(depth, 1, d_model).astype(F32), ple_w_gate,
                      ple_w_proj, tile=PROJ_TILE)
    return x2.reshape(batch, seq, d_model)
```
